```python
import math
import jax, jax.numpy as jnp
from jax import lax
import numpy as np

D_MODEL = 2048
BATCH = 4
SEQ = 4096
DEPTH = 4

HEAD_DIM = 128
ROPE_THETA = 10000.0
NEG_INF = -1e30
LN_EPS = 1e-5
SUBLN_EPS = 1e-5
DIL_PAIRS = ((128, 1), (512, 4), (2048, 16))
A_HEADS_PER_GROUP = 6
A_HEADS = len(DIL_PAIRS) * A_HEADS_PER_GROUP
DIL_BLOCK = 64
B_HEADS = 8
B_DIM = 64
DENSE_Q_BLOCK = 128
C_Q_HEADS = 8
C_KV_HEADS = 2
C_HALF_WINDOW = 128
C_BLOCK = 128
D_HEADS = 8
GRID_W = 64
NA_WIN_H = 8
NA_WIN_W = 16
N_BRANCHES = 4
N_EXPERTS = 16
EC_CAPACITY_FACTOR = 2
D_EXPERT = D_MODEL // 2

A_W = A_HEADS * HEAD_DIM
B_W = B_HEADS * 2 * B_DIM
C_Q_W = C_Q_HEADS * HEAD_DIM
C_KV_W = C_KV_HEADS * HEAD_DIM
D_W = D_HEADS * HEAD_DIM
IN_WIDTHS = (A_W, A_W, A_W, B_W, B_W, B_W, C_Q_W, C_KV_W, C_KV_W, D_W, D_W, D_W, N_BRANCHES * D_MODEL)
VALUE_SLOTS = (2, 5, 8, 11)
IN_TOTAL = sum(IN_WIDTHS)
BRANCH_WIDTHS = (A_HEADS_PER_GROUP * HEAD_DIM, B_W, C_Q_W, D_W)
BRANCH_TOTAL = sum(BRANCH_WIDTHS)

kernel_name = "hybrid_dilated_diff_window_na_ec_encoder"


def _offsets(widths):
    offs = [0]
    for w in widths:
        offs.append(offs[-1] + w)
    return offs


def layer_norm(x, g, b):
    xf = x.astype(jnp.float32)
    mu = jnp.mean(xf, axis=-1, keepdims=True)
    var = jnp.mean(jnp.square(xf - mu), axis=-1, keepdims=True)
    return ((xf - mu) * lax.rsqrt(var + LN_EPS) * g.astype(jnp.float32) + b.astype(jnp.float32)).astype(x.dtype)


def rope(x, pos):
    half = x.shape[-1] // 2
    inv = ROPE_THETA ** (-jnp.arange(half, dtype=jnp.float32) / half)
    ang = pos.astype(jnp.float32)[:, None] * inv[None, :]
    cos = jnp.cos(ang)[:, None, :]
    sin = jnp.sin(ang)[:, None, :]
    xf = x.astype(jnp.float32)
    x1, x2 = xf[..., :half], xf[..., half:]
    return jnp.concatenate([x1 * cos - x2 * sin, x2 * cos + x1 * sin], axis=-1).astype(x.dtype)


def banded_attention(q, k, v, half_width, block, sink=None):
    N, L, Hq, d = q.shape
    Hkv = k.shape[2]
    G = Hq // Hkv
    nb = -(-L // block)
    Lp = nb * block
    qb = jnp.pad(q, ((0, 0), (0, Lp - L), (0, 0), (0, 0))).reshape(N, nb, block, Hkv, G, d)

    def band(t):
        tp = jnp.pad(t, ((0, 0), (block, Lp - L + block), (0, 0), (0, 0))).reshape(N, nb + 2, block, Hkv, d)
        return jnp.concatenate([tp[:, :nb], tp[:, 1:nb + 1], tp[:, 2:]], axis=2)

    kb, vb = band(k), band(v)
    s = jnp.einsum('nbqhgd,nbkhd->nbhgqk', qb, kb, preferred_element_type=jnp.float32) * (d ** -0.5)
    qpos = jnp.arange(nb)[:, None] * block + jnp.arange(block)[None, :]
    kpos = (jnp.arange(nb)[:, None] - 1) * block + jnp.arange(3 * block)[None, :]
    valid = ((jnp.abs(qpos[:, :, None] - kpos[:, None, :]) <= half_width)
             & (kpos[:, None, :] >= 0) & (kpos[:, None, :] < L))
    s = jnp.where(valid[None, :, None, None], s, NEG_INF)
    m = jnp.max(s, axis=-1)
    if sink is not None:
        sk = sink.astype(jnp.float32).reshape(Hkv, G)[None, None, :, :, None]
        m = jnp.maximum(m, sk)
    p = jnp.exp(s - m[..., None])
    denom = jnp.sum(p, axis=-1)
    if sink is not None:
        denom = denom + jnp.exp(sk - m)
    o = jnp.einsum('nbhgqk,nbkhd->nbqhgd', (p / denom[..., None]).astype(v.dtype), vb)
    o = o.reshape(N, Lp, Hq, d)[:, :L]
    lse = (m + jnp.log(denom)).transpose(0, 1, 4, 2, 3).reshape(N, Lp, Hq)[:, :L]
    return o, lse


def dilated_attention(q, k, v):
    B, S = q.shape[:2]
    hg = A_HEADS_PER_GROUP
    outs, lses = [], []
    for g, (w, r) in enumerate(DIL_PAIRS):
        hs = slice(g * hg, (g + 1) * hg)
        n_side = (w // 2) // r
        L = S // r

        def to_sub(t):
            t = t[:, :, hs].reshape(B, L, r, hg, HEAD_DIM).transpose(0, 2, 1, 3, 4)
            return t.reshape(B * r, L, hg, HEAD_DIM)

        o, lse = banded_attention(to_sub(q), to_sub(k), to_sub(v), n_side, DIL_BLOCK)
        outs.append(o.reshape(B, r, L, hg, HEAD_DIM).transpose(0, 2, 1, 3, 4).reshape(B, S, hg, HEAD_DIM))
        lses.append(lse.reshape(B, r, L, hg).transpose(0, 2, 1, 3).reshape(B, S, hg))
    wts = jax.nn.softmax(jnp.stack(lses, axis=0), axis=0)
    out = jnp.einsum('gbsh,gbshd->bshd', wts.astype(q.dtype), jnp.stack(outs, axis=0))
    return out.reshape(B, S, hg * HEAD_DIM)


def diff_attention(q, k, v, lam, subln_w, lambda_init):
    B, S = q.shape[:2]
    lamf = lam.astype(jnp.float32)
    lmbda = jnp.exp(jnp.sum(lamf[0] * lamf[1])) - jnp.exp(jnp.sum(lamf[2] * lamf[3])) + lambda_init
    nq = S // DENSE_Q_BLOCK
    qblocks = q.reshape(B, nq, DENSE_Q_BLOCK, B_HEADS, 2, B_DIM).transpose(1, 0, 2, 3, 4, 5)
    scale = B_DIM ** -0.5

    def block_fn(qblk):
        s = jnp.einsum('bqhmd,bkhmd->bhmqk', qblk, k, preferred_element_type=jnp.float32) * scale
        a = jax.nn.softmax(s, axis=-1)
        a = a[:, :, 0] - lmbda * a[:, :, 1]
        return jnp.einsum('bhqk,bkhe->bqhe', a.astype(v.dtype), v)

    o = lax.map(block_fn, qblocks)
    of = o.transpose(1, 0, 2, 3, 4).reshape(B, S, B_HEADS, 2 * B_DIM).astype(jnp.float32)
    of = of * lax.rsqrt(jnp.mean(of * of, axis=-1, keepdims=True) + SUBLN_EPS) * subln_w.astype(jnp.float32)
    return (of * (1.0 - lambda_init)).astype(q.dtype).reshape(B, S, B_W)


def neighborhood_attention(q, k, v, rpb):
    B, S, H, d = q.shape
    rows = S // GRID_W
    kh = min(NA_WIN_H, rows)
    kw = NA_WIN_W
    r = jnp.arange(rows)
    c = jnp.arange(GRID_W)
    row_idx = jnp.clip(r - kh // 2, 0, rows - kh)[:, None] + jnp.arange(kh)[None, :]
    col_start = jnp.clip(c - kw // 2, 0, GRID_W - kw)
    col_ok = (c[None, :] >= col_start[:, None]) & (c[None, :] < col_start[:, None] + kw)
    qg = q.reshape(B, rows, GRID_W, H, d)
    kband = k.reshape(B, rows, GRID_W, H, d)[:, row_idx]
    vband = v.reshape(B, rows, GRID_W, H, d)[:, row_idx]
    s = jnp.einsum('brchd,brkwhd->brhckw', qg, kband, preferred_element_type=jnp.float32) * (d ** -0.5)
    roff = row_idx - r[:, None] + (NA_WIN_H - 1)
    coff = jnp.clip(c[None, :] - c[:, None], -(kw - 1), kw - 1) + (NA_WIN_W - 1)
    bias = rpb.astype(jnp.float32)[:, roff[:, None, :, None], coff[None, :, None, :]]
    s = s + bias.transpose(1, 0, 2, 3, 4)[None]
    s = jnp.where(col_ok[:, None, :], s, NEG_INF)
    p = jax.nn.softmax(s.reshape(B, rows, H, GRID_W, kh * GRID_W), axis=-1)
    o = jnp.einsum('brhcK,brKhd->brchd', p.astype(v.dtype), vband.reshape(B, rows, kh * GRID_W, H, d))
    return o.reshape(B, S, H * d)


def expert_choice_ffn(x, w_router, w_gate, w_up, w_down):
    B, N, D = x.shape
    cap = EC_CAPACITY_FACTOR * N // N_EXPERTS
    aff = jax.nn.softmax(jnp.einsum('bnd,de->bne', x, w_router, preferred_element_type=jnp.float32), axis=-1)
    g, idx = lax.top_k(aff.transpose(0, 2, 1), cap)
    xin = jax.vmap(lambda xb, ib: xb[ib])(x, idx)
    h = jax.nn.silu(jnp.einsum('becd,edf->becf', xin, w_gate)) * jnp.einsum('becd,edf->becf', xin, w_up)
    y = jnp.einsum('becf,efd->becd', h, w_down) * g[..., None].astype(x.dtype)
    return jax.vmap(lambda yb, ib: jnp.zeros((N, D), yb.dtype).at[ib.reshape(-1)].add(yb.reshape(-1, D)))(y, idx)


def setup_inputs(seed: int = 0) -> dict:
    key = jax.random.key(seed)
    ks = jax.random.split(key, 15)
    f32 = jnp.float32
    beta = (8.0 * DEPTH) ** -0.25
    in_offs = _offsets(IN_WIDTHS)
    col_scale = np.full((IN_TOTAL,), D_MODEL ** -0.5, dtype=np.float32)
    for slot in VALUE_SLOTS:
        col_scale[in_offs[slot]:in_offs[slot + 1]] *= beta
    br_offs = _offsets(BRANCH_WIDTHS)
    row_scale = np.zeros((BRANCH_TOTAL,), dtype=np.float32)
    for i, w in enumerate(BRANCH_WIDTHS):
        row_scale[br_offs[i]:br_offs[i + 1]] = beta * w ** -0.5
    x = jax.random.normal(ks[0], (BATCH, SEQ, D_MODEL), f32)
    w_in = jax.random.normal(ks[1], (DEPTH, D_MODEL, IN_TOTAL), f32) * jnp.asarray(col_scale)
    b_gate = 0.02 * jax.random.normal(ks[2], (DEPTH, N_BRANCHES * D_MODEL), f32)
    w_branch = jax.random.normal(ks[3], (DEPTH, BRANCH_TOTAL, D_MODEL), f32) * jnp.asarray(row_scale)[:, None]
    w_out = jax.random.normal(ks[4], (DEPTH, D_MODEL, D_MODEL), f32) * (beta * D_MODEL ** -0.5)
    diff_lambda = 0.1 * jax.random.normal(ks[5], (DEPTH, 4, B_DIM), f32)
    diff_subln = 1.0 + 0.02 * jax.random.normal(ks[6], (DEPTH, 2 * B_DIM), f32)
    sink_logit = 0.5 * jax.random.normal(ks[7], (DEPTH, C_Q_HEADS), f32)
    na_rpb = 0.1 * jax.random.normal(ks[8], (DEPTH, D_HEADS, 2 * NA_WIN_H - 1, 2 * NA_WIN_W - 1), f32)
    w_router = jax.random.normal(ks[9], (DEPTH, D_MODEL, N_EXPERTS), f32) * D_MODEL ** -0.5
    w_exp_gate = jax.random.normal(ks[10], (DEPTH, N_EXPERTS, D_MODEL, D_EXPERT), f32) * D_MODEL ** -0.5
    w_exp_up = jax.random.normal(ks[11], (DEPTH, N_EXPERTS, D_MODEL, D_EXPERT), f32) * (beta * D_MODEL ** -0.5)
    w_exp_down = jax.random.normal(ks[12], (DEPTH, N_EXPERTS, D_EXPERT, D_MODEL), f32) * (beta * D_EXPERT ** -0.5)
    ln_gain = 1.0 + 0.02 * jax.random.normal(ks[13], (DEPTH, 2, D_MODEL), f32)
    ln_bias = 0.02 * jax.random.normal(ks[14], (DEPTH, 2, D_MODEL), f32)
    return {"x": x, "w_in": w_in, "b_gate": b_gate, "w_branch": w_branch, "w_out": w_out,
            "diff_lambda": diff_lambda, "diff_subln": diff_subln, "sink_logit": sink_logit,
            "na_rpb": na_rpb, "w_router": w_router, "w_exp_gate": w_exp_gate, "w_exp_up": w_exp_up,
            "w_exp_down": w_exp_down, "ln_gain": ln_gain, "ln_bias": ln_bias}


def reference(x, w_in, b_gate, w_branch, w_out, diff_lambda, diff_subln, sink_logit, na_rpb,
              w_router, w_exp_gate, w_exp_up, w_exp_down, ln_gain, ln_bias):
    B, S, D = x.shape
    pos = jnp.arange(S, dtype=jnp.int32)
    alpha = (2.0 * DEPTH) ** 0.25
    in_offs = _offsets(IN_WIDTHS)
    br_offs = _offsets(BRANCH_WIDTHS)
    for l in range(DEPTH):
        lambda_init = 0.8 - 0.6 * math.exp(-0.3 * l)
        proj = jnp.einsum('bsd,dn->bsn', x, w_in[l])
        parts = [proj[..., in_offs[i]:in_offs[i + 1]] for i in range(len(IN_WIDTHS))]
        q_a, k_a, v_a, q_b, k_b, v_b, q_c, k_c, v_c, q_d, k_d, v_d, gate_pre = parts
        o_a = dilated_attention(rope(q_a.reshape(B, S, A_HEADS, HEAD_DIM), pos),
                                rope(k_a.reshape(B, S, A_HEADS, HEAD_DIM), pos),
                                v_a.reshape(B, S, A_HEADS, HEAD_DIM))
        qb = rope(q_b.reshape(B, S, B_HEADS * 2, B_DIM), pos).reshape(B, S, B_HEADS, 2, B_DIM)
        kb = rope(k_b.reshape(B, S, B_HEADS * 2, B_DIM), pos).reshape(B, S, B_HEADS, 2, B_DIM)
        o_b = diff_attention(qb, kb, v_b.reshape(B, S, B_HEADS, 2 * B_DIM), diff_lambda[l], diff_subln[l], lambda_init)
        o_c, _ = banded_attention(rope(q_c.reshape(B, S, C_Q_HEADS, HEAD_DIM), pos),
                                  rope(k_c.reshape(B, S, C_KV_HEADS, HEAD_DIM), pos),
                                  v_c.reshape(B, S, C_KV_HEADS, HEAD_DIM),
                                  C_HALF_WINDOW, C_BLOCK, sink_logit[l])
        o_c = o_c.reshape(B, S, C_Q_W)
        o_d = neighborhood_attention(q_d.reshape(B, S, D_HEADS, HEAD_DIM), k_d.reshape(B, S, D_HEADS, HEAD_DIM),
                                     v_d.reshape(B, S, D_HEADS, HEAD_DIM), na_rpb[l])
        gates = jax.nn.sigmoid(gate_pre + b_gate[l]).reshape(B, S, N_BRANCHES, D)
        mixed = None
        for i, o in enumerate((o_a, o_b, o_c, o_d)):
            term = gates[:, :, i] * jnp.einsum('bsk,kd->bsd', o, w_branch[l, br_offs[i]:br_offs[i + 1]])
            mixed = term if mixed is None else mixed + term
        x = layer_norm(alpha * x + jnp.einsum('bsd,de->bse', mixed, w_out[l]), ln_gain[l, 0], ln_bias[l, 0])
        x = layer_norm(alpha * x + expert_choice_ffn(x, w_router[l], w_exp_gate[l], w_exp_up[l], w_exp_down[l]),
                       ln_gain[l, 1], ln_bias[l, 1])
    return x
```

```python
import functools
import math

import jax
import jax.numpy as jnp
from jax import lax
from jax.experimental import pallas as pl
from jax.experimental.pallas import tpu as pltpu

F32 = jnp.float32
BF16 = jnp.bfloat16

LANES = 128
HEAD_DIM = 128
ROPE_THETA = 10000.0
NEG_INF = -1e30
LN_EPS = 1e-5
SUBLN_EPS = 1e-5
DIL_PAIRS = ((128, 1), (512, 4), (2048, 16))
A_HEADS_PER_GROUP = 6
A_GROUPS = len(DIL_PAIRS)
A_HEADS = A_GROUPS * A_HEADS_PER_GROUP
B_HEADS = 8
B_DIM = 64
C_Q_HEADS = 8
C_KV_HEADS = 2
C_HALF_WINDOW = 128
D_HEADS = 8
GRID_W = 64
NA_WIN_H = 8
NA_WIN_W = 16
NA_Q_ROWS = 8
NA_K_ROWS = 16
N_BRANCHES = 4
N_EXPERTS = 16
EC_CAPACITY_FACTOR = 2
EXPERT_PAD = 128

A_W = A_HEADS * HEAD_DIM
B_W = B_HEADS * 2 * B_DIM
C_Q_W = C_Q_HEADS * HEAD_DIM
C_KV_W = C_KV_HEADS * HEAD_DIM
D_W = D_HEADS * HEAD_DIM
A_OUT_W = A_HEADS_PER_GROUP * HEAD_DIM

VMEM_LIMIT = 56 * 1024 * 1024


def _params(*sem):
    return pltpu.CompilerParams(dimension_semantics=sem, vmem_limit_bytes=VMEM_LIMIT)


def _dot_nt(a, b):
    return lax.dot_general(a, b, (((1,), (1,)), ((), ())), preferred_element_type=F32)


def _proj_body(mode, x_ref, w_ref, *rest):
    acc = jnp.dot(x_ref[...], w_ref[...], preferred_element_type=F32)
    if mode == "plain":
        (o_ref,) = rest
        o_ref[...] = acc.astype(o_ref.dtype)
    elif mode == "gate":
        b_ref, o_ref = rest
        z = acc + b_ref[...]
        o_ref[...] = (1.0 / (1.0 + jnp.exp(-z))).astype(o_ref.dtype)
    else:
        cos_ref, sin_ref, o_ref = rest
        cos = cos_ref[...]
        sin = sin_ref[...]
        lane = lax.broadcasted_iota(jnp.int32, cos.shape, 1)
        for c in range(acc.shape[1] // LANES):
            a = acc[:, c * LANES:(c + 1) * LANES]
            if mode == "rope128":
                rot = pltpu.roll(a, 64, 1)
            else:
                rot = jnp.where((lane % 64) < 32, pltpu.roll(a, 96, 1), pltpu.roll(a, 32, 1))
            o_ref[:, c * LANES:(c + 1) * LANES] = (a * cos + rot * sin).astype(o_ref.dtype)


def _proj(xb, w, mode, tn, seq, extra=()):
    m_tot, k = xb.shape
    n_tot = w.shape[1]
    tm = min(1024, seq)
    pos_blocks = seq // tm
    in_specs = [pl.BlockSpec((tm, k), lambda n, m: (m, 0)),
                pl.BlockSpec((k, tn), lambda n, m: (0, n))]
    if mode == "gate":
        in_specs.append(pl.BlockSpec((1, tn), lambda n, m: (0, n)))
    elif mode != "plain":
        in_specs += [pl.BlockSpec((tm, LANES), lambda n, m: (m % pos_blocks, 0))] * 2
    return pl.pallas_call(
        functools.partial(_proj_body, mode),
        grid=(n_tot // tn, m_tot // tm),
        in_specs=in_specs,
        out_specs=pl.BlockSpec((tm, tn), lambda n, m: (m, n)),
        out_shape=jax.ShapeDtypeStruct((m_tot, n_tot), BF16),
        compiler_params=_params("parallel", "parallel"),
        name="proj_" + mode,
    )(xb, w, *extra)


def _rope_tables(seq, half):
    inv = ROPE_THETA ** (-jnp.arange(half, dtype=F32) / half)
    ang = jnp.arange(seq, dtype=jnp.int32).astype(F32)[:, None] * inv[None, :]
    cos, sin = jnp.cos(ang), jnp.sin(ang)
    reps = LANES // (2 * half)
    return (jnp.tile(jnp.concatenate([cos, cos], axis=1), (1, reps)),
            jnp.tile(jnp.concatenate([-sin, sin], axis=1), (1, reps)))


def _banded_body(hq, group, halo, tq, win, length, has_sink, want_lse, *refs):
    refs = list(refs)
    q_ref, k_ref, v_ref = refs[:3]
    pos = 3
    sink_ref = None
    if has_sink:
        sink_ref = refs[pos]
        pos += 1
    o_ref = refs[pos]
    lse_ref = refs[pos + 1] if want_lse else None

    i = pl.program_id(2)
    start = pl.multiple_of(jnp.clip(i * tq - halo, 0, length - win), 64)
    qpos = i * tq + lax.broadcasted_iota(jnp.int32, (tq, win), 0)
    kpos = start + lax.broadcasted_iota(jnp.int32, (tq, win), 1)
    valid = jnp.abs(qpos - kpos) <= halo
    scale = HEAD_DIM ** -0.5
    lane = lax.broadcasted_iota(jnp.int32, (tq, LANES), 1)
    lse_acc = jnp.zeros((tq, LANES), F32)
    for h in range(hq):
        kh = h // group
        q = q_ref[:, h * HEAD_DIM:(h + 1) * HEAD_DIM]
        k = k_ref[pl.ds(start, win), kh * HEAD_DIM:(kh + 1) * HEAD_DIM]
        v = v_ref[pl.ds(start, win), kh * HEAD_DIM:(kh + 1) * HEAD_DIM]
        s = _dot_nt(q, k) * scale
        s = jnp.where(valid, s, NEG_INF)
        m = jnp.max(s, axis=-1, keepdims=True)
        if has_sink:
            sk = sink_ref[h]
            m = jnp.maximum(m, sk)
        p = jnp.exp(s - m)
        d = jnp.sum(p, axis=-1, keepdims=True)
        if has_sink:
            d = d + jnp.exp(sk - m)
        o = jnp.dot(p.astype(BF16), v, preferred_element_type=F32) / d
        o_ref[:, h * HEAD_DIM:(h + 1) * HEAD_DIM] = o.astype(o_ref.dtype)
        if want_lse:
            lse_acc = jnp.where(lane == h, m + jnp.log(d), lse_acc)
    if want_lse:
        lse_ref[...] = lse_acc


def _banded(q3, k3, v3, qcol, kcol, vcol, *, dil, hq, group, halo, sink=None, want_lse=False):
    bsz, length, _ = q3.shape
    hkv = hq // group
    tq = min(256, length)
    win = min(length, tq + 2 * halo)
    qw, kw = hq * HEAD_DIM, hkv * HEAD_DIM
    in_specs = [pl.BlockSpec((None, tq, qw), lambda b, j, i: (b, i, qcol(j))),
                pl.BlockSpec((None, length, kw), lambda b, j, i: (b, 0, kcol(j))),
                pl.BlockSpec((None, length, kw), lambda b, j, i: (b, 0, vcol(j)))]
    args = [q3, k3, v3]
    if sink is not None:
        in_specs.append(pl.BlockSpec(memory_space=pltpu.SMEM))
        args.append(sink)
    out_specs = [pl.BlockSpec((None, tq, qw), lambda b, j, i: (b, i, j))]
    out_shape = [jax.ShapeDtypeStruct((bsz, length, dil * qw), BF16)]
    if want_lse:
        out_specs.append(pl.BlockSpec((None, tq, LANES), lambda b, j, i: (b, i, j)))
        out_shape.append(jax.ShapeDtypeStruct((bsz, length, dil * LANES), F32))
    return pl.pallas_call(
        functools.partial(_banded_body, hq, group, halo, tq, win, length, sink is not None, want_lse),
        grid=(bsz, dil, length // tq),
        in_specs=in_specs,
        out_specs=out_specs,
        out_shape=out_shape,
        compiler_params=_params("parallel", "parallel", "parallel"),
        name="banded_attention",
    )(*args)


def _group_mix_body(o0_ref, o1_ref, o2_ref, l0_ref, l1_ref, l2_ref, out_ref):
    l0, l1, l2 = l0_ref[...], l1_ref[...], l2_ref[...]
    m = jnp.maximum(jnp.maximum(l0, l1), l2)
    e0, e1, e2 = jnp.exp(l0 - m), jnp.exp(l1 - m), jnp.exp(l2 - m)
    tot = e0 + e1 + e2
    w0, w1, w2 = e0 / tot, e1 / tot, e2 / tot
    for h in range(A_HEADS_PER_GROUP):
        cs = slice(h * HEAD_DIM, (h + 1) * HEAD_DIM)
        acc = (w0[:, h:h + 1] * o0_ref[:, cs].astype(F32)
               + w1[:, h:h + 1] * o1_ref[:, cs].astype(F32)
               + w2[:, h:h + 1] * o2_ref[:, cs].astype(F32))
        out_ref[:, cs] = acc.astype(out_ref.dtype)


def _group_mix(outs, lses, seq):
    m_tot = outs[0].shape[0]
    tm = min(1024, seq)
    o_spec = pl.BlockSpec((tm, A_OUT_W), lambda m: (m, 0))
    l_spec = pl.BlockSpec((tm, LANES), lambda m: (m, 0))
    return pl.pallas_call(
        _group_mix_body,
        grid=(m_tot // tm,),
        in_specs=[o_spec] * 3 + [l_spec] * 3,
        out_specs=o_spec,
        out_shape=jax.ShapeDtypeStruct((m_tot, A_OUT_W), BF16),
        compiler_params=_params("parallel"),
        name="dilation_group_mix",
    )(*outs, *lses)


def _diff_body(lambda_init, lam_ref, subln_ref, q_ref, k_ref, v_ref, o_ref):
    lam = lam_ref[...]
    dot1 = jnp.sum(lam[0:1, :] * lam[1:2, :], axis=1, keepdims=True)
    dot2 = jnp.sum(lam[2:3, :] * lam[3:4, :], axis=1, keepdims=True)
    lmbda = jnp.exp(dot1) - jnp.exp(dot2) + lambda_init
    q = q_ref[...]
    k = k_ref[...]
    lane = lax.broadcasted_iota(jnp.int32, q.shape, 1)
    zero = jnp.zeros_like(q)
    scale = B_DIM ** -0.5

    def softmax_map(qm):
        s = _dot_nt(qm, k) * scale
        p = jnp.exp(s - jnp.max(s, axis=-1, keepdims=True))
        return p / jnp.sum(p, axis=-1, keepdims=True)

    a = softmax_map(jnp.where(lane < B_DIM, q, zero)) - lmbda * softmax_map(jnp.where(lane >= B_DIM, q, zero))
    o = jnp.dot(a.astype(BF16), v_ref[...], preferred_element_type=F32)
    o = o * lax.rsqrt(jnp.mean(o * o, axis=-1, keepdims=True) + SUBLN_EPS) * subln_ref[...]
    o_ref[...] = (o * (1.0 - lambda_init)).astype(o_ref.dtype)


def _diff_attention(qk3, v3, lam, subln, lambda_init):
    bsz, seq, _ = qk3.shape
    tq = min(256, seq)
    hw = 2 * B_DIM
    return pl.pallas_call(
        functools.partial(_diff_body, lambda_init),
        grid=(bsz, B_HEADS, seq // tq),
        in_specs=[pl.BlockSpec((4, B_DIM), lambda b, h, i: (0, 0)),
                  pl.BlockSpec((1, hw), lambda b, h, i: (0, 0)),
                  pl.BlockSpec((None, tq, hw), lambda b, h, i: (b, i, h)),
                  pl.BlockSpec((None, seq, hw), lambda b, h, i: (b, 0, B_HEADS + h)),
                  pl.BlockSpec((None, seq, hw), lambda b, h, i: (b, 0, h))],
        out_specs=pl.BlockSpec((None, tq, hw), lambda b, h, i: (b, i, h)),
        out_shape=jax.ShapeDtypeStruct((bsz, seq, B_W), BF16),
        compiler_params=_params("parallel", "parallel", "parallel"),
        name="diff_attention",
    )(lam, subln, qk3, qk3, v3)


def _na_body(rows, k_rows, q_ref, k_ref, v_ref, bias_ref, o_ref):
    rb = pl.program_id(2)
    n_rb = rows // NA_Q_ROWS
    w0 = jnp.clip(rb * NA_Q_ROWS - NA_WIN_H // 2, 0, rows - k_rows) * GRID_W
    w0 = pl.multiple_of(w0, 256)
    kind = jnp.where(rb == 0, 0, jnp.where(rb == n_rb - 1, 2, 1))
    k = k_ref[pl.ds(w0, k_rows * GRID_W), :]
    v = v_ref[pl.ds(w0, k_rows * GRID_W), :]
    s = _dot_nt(q_ref[...], k) * (HEAD_DIM ** -0.5) + bias_ref[kind]
    p = jnp.exp(s - jnp.max(s, axis=-1, keepdims=True))
    d = jnp.sum(p, axis=-1, keepdims=True)
    o = jnp.dot(p.astype(BF16), v, preferred_element_type=F32) / d
    o_ref[...] = o.astype(o_ref.dtype)


def _na_bias_tables(rpb, rows):
    kh = min(NA_WIN_H, rows)
    k_rows = min(NA_K_ROWS, rows)
    n_rb = rows // NA_Q_ROWS
    c = jnp.arange(GRID_W)
    col_start = jnp.clip(c - NA_WIN_W // 2, 0, GRID_W - NA_WIN_W)
    col_ok = (c[None, :] >= col_start[:, None]) & (c[None, :] < col_start[:, None] + NA_WIN_W)
    coff = jnp.clip(c[None, :] - c[:, None], -(NA_WIN_W - 1), NA_WIN_W - 1) + (NA_WIN_W - 1)
    rpb = rpb.astype(F32)
    tabs = []
    for rb in (0, min(1, n_rb - 1), n_rb - 1):
        r0 = rb * NA_Q_ROWS
        w0 = min(max(r0 - NA_WIN_H // 2, 0), rows - k_rows)
        qrow = r0 + jnp.arange(NA_Q_ROWS)
        krow = w0 + jnp.arange(k_rows)
        start = jnp.clip(qrow - kh // 2, 0, rows - kh)
        row_ok = (krow[None, :] >= start[:, None]) & (krow[None, :] < start[:, None] + kh)
        roff = jnp.clip(krow[None, :] - qrow[:, None] + (NA_WIN_H - 1), 0, 2 * NA_WIN_H - 2)
        bias = rpb[:, roff[:, None, :, None], coff[None, :, None, :]]
        ok = row_ok[:, None, :, None] & col_ok[None, :, None, :]
        bias = jnp.where(ok[None], bias, NEG_INF)
        tabs.append(bias.reshape(rpb.shape[0], NA_Q_ROWS * GRID_W, k_rows * GRID_W))
    return jnp.stack(tabs, axis=1)


def _neighborhood_attention(qkv3, rpb):
    bsz, seq, _ = qkv3.shape
    rows = seq // GRID_W
    k_rows = min(NA_K_ROWS, rows)
    tq = NA_Q_ROWS * GRID_W
    bias = _na_bias_tables(rpb, rows)
    return pl.pallas_call(
        functools.partial(_na_body, rows, k_rows),
        grid=(bsz, D_HEADS, rows // NA_Q_ROWS),
        in_specs=[pl.BlockSpec((None, tq, HEAD_DIM), lambda b, h, r: (b, r, h)),
                  pl.BlockSpec((None, seq, HEAD_DIM), lambda b, h, r: (b, 0, D_HEADS + h)),
                  pl.BlockSpec((None, seq, HEAD_DIM), lambda b, h, r: (b, 0, 2 * D_HEADS + h)),
                  pl.BlockSpec((None, 3, tq, k_rows * GRID_W), lambda b, h, r: (h, 0, 0, 0))],
        out_specs=pl.BlockSpec((None, tq, HEAD_DIM), lambda b, h, r: (b, r, h)),
        out_shape=jax.ShapeDtypeStruct((bsz, seq, D_W), BF16),
        compiler_params=_params("parallel", "parallel", "parallel"),
        name="neighborhood_attention",
    )(qkv3, qkv3, qkv3, bias)


def _merge_body(oa_ref, ob_ref, oc_ref, od_ref, wa_ref, wb_ref, wc_ref, wd_ref,
                ga_ref, gb_ref, gc_ref, gd_ref, out_ref):
    acc = ga_ref[...].astype(F32) * jnp.dot(oa_ref[...], wa_ref[...], preferred_element_type=F32)
    acc = acc + gb_ref[...].astype(F32) * jnp.dot(ob_ref[...], wb_ref[...], preferred_element_type=F32)
    acc = acc + gc_ref[...].astype(F32) * jnp.dot(oc_ref[...], wc_ref[...], preferred_element_type=F32)
    acc = acc + gd_ref[...].astype(F32) * jnp.dot(od_ref[...], wd_ref[...], preferred_element_type=F32)
    out_ref[...] = acc.astype(out_ref.dtype)


def _merge(branches, weights, gates, seq):
    m_tot = branches[0].shape[0]
    d = weights[0].shape[1]
    tm = min(1024, seq)
    tn = min(512, d)
    n_blocks = d // tn
    in_specs = [pl.BlockSpec((tm, o.shape[1]), lambda m, n: (m, 0)) for o in branches]
    in_specs += [pl.BlockSpec((w.shape[0], tn), lambda m, n: (0, n)) for w in weights]
    in_specs += [pl.BlockSpec((tm, tn), functools.partial(lambda m, n, i: (m, i * n_blocks + n), i=i))
                 for i in range(N_BRANCHES)]
    return pl.pallas_call(
        _merge_body,
        grid=(m_tot // tm, n_blocks),
        in_specs=in_specs,
        out_specs=pl.BlockSpec((tm, tn), lambda m, n: (m, n)),
        out_shape=jax.ShapeDtypeStruct((m_tot, d), BF16),
        compiler_params=_params("parallel", "parallel"),
        name="branch_merge",
    )(*branches, *weights, *([gates] * N_BRANCHES))


def _layer_norm_rows(y, gain, bias):
    mu = jnp.mean(y, axis=-1, keepdims=True)
    var = jnp.mean(jnp.square(y - mu), axis=-1, keepdims=True)
    return (y - mu) * lax.rsqrt(var + LN_EPS) * gain + bias


def _out_proj_body(alpha, mixed_ref, w_ref, x_ref, gain_ref, bias_ref, xo_ref, xb_ref):
    y = alpha * x_ref[...] + jnp.dot(mixed_ref[...], w_ref[...], preferred_element_type=F32)
    out = _layer_norm_rows(y, gain_ref[...], bias_ref[...])
    xo_ref[...] = out
    xb_ref[...] = out.astype(BF16)


def _out_proj(mixed, w, x, gain, bias, alpha, seq):
    m_tot, d = x.shape
    tm = min(256, seq)
    row = pl.BlockSpec((tm, d), lambda m: (m, 0))
    vec = pl.BlockSpec((1, d), lambda m: (0, 0))
    return pl.pallas_call(
        functools.partial(_out_proj_body, alpha),
        grid=(m_tot // tm,),
        in_specs=[row, pl.BlockSpec((d, d), lambda m: (0, 0)), row, vec, vec],
        out_specs=[row, row],
        out_shape=[jax.ShapeDtypeStruct((m_tot, d), F32), jax.ShapeDtypeStruct((m_tot, d), BF16)],
        compiler_params=_params("parallel"),
        name="out_proj_layernorm",
    )(mixed, w, x, gain, bias)


def _router_body(cap, x_ref, wrt_ref, slot_t_ref, slot_ref, gate_ref):
    seq = x_ref.shape[0]
    logits = _dot_nt(wrt_ref[...], x_ref[...])
    e = jnp.exp(logits - jnp.max(logits, axis=0, keepdims=True))
    aff = e / jnp.sum(e, axis=0, keepdims=True)
    bits = lax.bitcast_convert_type(aff, jnp.int32)
    thr = jnp.zeros((N_EXPERTS, 1), jnp.int32)
    for bit in range(30, -1, -1):
        cand = thr | (1 << bit)
        cnt = jnp.sum((bits >= cand).astype(F32), axis=1, keepdims=True)
        thr = jnp.where(cnt >= cap, cand, thr)
    above = bits > thr
    tied = bits == thr
    need = cap - jnp.sum(above.astype(F32), axis=1, keepdims=True)

    chunk = min(512, seq)
    upper = (lax.broadcasted_iota(jnp.int32, (chunk, chunk), 0)
             <= lax.broadcasted_iota(jnp.int32, (chunk, chunk), 1)).astype(BF16)

    def prefix_count(mask):
        parts, carry = [], jnp.zeros((N_EXPERTS, 1), F32)
        for c in range(seq // chunk):
            part = jnp.dot(mask[:, c * chunk:(c + 1) * chunk].astype(BF16), upper,
                           preferred_element_type=F32) + carry
            parts.append(part)
            carry = part[:, chunk - 1:chunk]
        return jnp.concatenate(parts, axis=1)

    sel = above | (tied & (prefix_count(tied) <= need))
    slot = jnp.where(sel, prefix_count(sel) - 1.0, -1.0)
    gate = jnp.where(sel, aff, 0.0)
    slot_t_ref[...] = slot.astype(jnp.int32)
    pad = EXPERT_PAD - N_EXPERTS
    slot_ref[...] = jnp.concatenate([slot, jnp.full((pad, seq), -1.0, F32)], axis=0).T
    gate_ref[...] = jnp.concatenate([gate, jnp.zeros((pad, seq), F32)], axis=0).T


def _router(xb3, wrt, cap):
    bsz, seq, d = xb3.shape
    return pl.pallas_call(
        functools.partial(_router_body, cap),
        grid=(bsz,),
        in_specs=[pl.BlockSpec((None, seq, d), lambda b: (b, 0, 0)),
                  pl.BlockSpec((N_EXPERTS, d), lambda b: (0, 0))],
        out_specs=[pl.BlockSpec((None, N_EXPERTS, seq), lambda b: (b, 0, 0)),
                   pl.BlockSpec((None, seq, EXPERT_PAD), lambda b: (b, 0, 0)),
                   pl.BlockSpec((None, seq, EXPERT_PAD), lambda b: (b, 0, 0))],
        out_shape=[jax.ShapeDtypeStruct((bsz, N_EXPERTS, seq), jnp.int32),
                   jax.ShapeDtypeStruct((bsz, seq, EXPERT_PAD), F32),
                   jax.ShapeDtypeStruct((bsz, seq, EXPERT_PAD), F32)],
        compiler_params=_params("parallel"),
        name="router_topk",
    )(xb3, wrt)


def _gather_body(cap, x_ref, slot_t_ref, o_ref):
    e = pl.program_id(1)
    seq = x_ref.shape[0]
    row = slot_t_ref[pl.ds(e, 1), :]
    onehot = (lax.broadcasted_iota(jnp.int32, (cap, seq), 0) == row).astype(BF16)
    o_ref[...] = jnp.dot(onehot, x_ref[...], preferred_element_type=F32).astype(o_ref.dtype)


def _gather_tokens(xb3, slot_t, cap):
    bsz, seq, d = xb3.shape
    return pl.pallas_call(
        functools.partial(_gather_body, cap),
        grid=(bsz, N_EXPERTS),
        in_specs=[pl.BlockSpec((None, seq, d), lambda b, e: (b, 0, 0)),
                  pl.BlockSpec((None, N_EXPERTS, seq), lambda b, e: (b, 0, 0))],
        out_specs=pl.BlockSpec((None, cap, d), lambda b, e: (e, b, 0)),
        out_shape=jax.ShapeDtypeStruct((N_EXPERTS, bsz * cap, d), BF16),
        compiler_params=_params("parallel", "parallel"),
        name="expert_gather",
    )(xb3, slot_t)


def _expert_body(x_ref, wg_ref, wu_ref, wd_ref, o_ref):
    x = x_ref[...]
    g = jnp.dot(x, wg_ref[...], preferred_element_type=F32)
    u = jnp.dot(x, wu_ref[...], preferred_element_type=F32)
    h = (g / (1.0 + jnp.exp(-g))) * u
    o_ref[...] = jnp.dot(h.astype(BF16), wd_ref[...], preferred_element_type=F32).astype(o_ref.dtype)


def _expert_ffn(xin, wg, wu, wd):
    n_exp, rows, d = xin.shape
    f = wg.shape[2]
    tm = min(512, rows)
    return pl.pallas_call(
        _expert_body,
        grid=(n_exp, rows // tm),
        in_specs=[pl.BlockSpec((None, tm, d), lambda e, m: (e, m, 0)),
                  pl.BlockSpec((None, d, f), lambda e, m: (e, 0, 0)),
                  pl.BlockSpec((None, d, f), lambda e, m: (e, 0, 0)),
                  pl.BlockSpec((None, f, d), lambda e, m: (e, 0, 0))],
        out_specs=pl.BlockSpec((None, tm, d), lambda e, m: (e, m, 0)),
        out_shape=jax.ShapeDtypeStruct((n_exp, rows, d), BF16),
        compiler_params=_params("parallel", "parallel"),
        name="expert_swiglu",
    )(xin, wg, wu, wd)


def _combine_body(alpha, cap, y_ref, slot_ref, gate_ref, x_ref, gain_ref, bias_ref, xo_ref, xb_ref, acc_ref):
    e = pl.program_id(2)

    @pl.when(e == 0)
    def _():
        acc_ref[...] = jnp.zeros_like(acc_ref)

    tt = slot_ref.shape[0]
    lane = lax.broadcasted_iota(jnp.int32, (tt, EXPERT_PAD), 1)
    pick = lane == e
    slot_col = jnp.sum(jnp.where(pick, slot_ref[...], 0.0), axis=1, keepdims=True)
    gate_col = jnp.sum(jnp.where(pick, gate_ref[...], 0.0), axis=1, keepdims=True)
    onehot = (lax.broadcasted_iota(jnp.int32, (tt, cap), 1).astype(F32) == slot_col).astype(BF16)
    acc_ref[...] += gate_col * jnp.dot(onehot, y_ref[...], preferred_element_type=F32)

    @pl.when(e == N_EXPERTS - 1)
    def _():
        out = _layer_norm_rows(alpha * x_ref[...] + acc_ref[...], gain_ref[...], bias_ref[...])
        xo_ref[...] = out
        xb_ref[...] = out.astype(BF16)


def _combine(y, slot, gate, x3, gain, bias, alpha, cap):
    bsz, seq, d = x3.shape
    tt = min(512, seq)
    tok = pl.BlockSpec((None, tt, d), lambda b, t, e: (b, t, 0))
    sel = pl.BlockSpec((None, tt, EXPERT_PAD), lambda b, t, e: (b, t, 0))
    vec = pl.BlockSpec((1, d), lambda b, t, e: (0, 0))
    return pl.pallas_call(
        functools.partial(_combine_body, alpha, cap),
        grid=(bsz, seq // tt, N_EXPERTS),
        in_specs=[pl.BlockSpec((None, cap, d), lambda b, t, e: (e, b, 0)), sel, sel, tok, vec, vec],
        out_specs=[tok, tok],
        out_shape=[jax.ShapeDtypeStruct((bsz, seq, d), F32), jax.ShapeDtypeStruct((bsz, seq, d), BF16)],
        scratch_shapes=[pltpu.VMEM((tt, d), F32)],
        compiler_params=_params("parallel", "parallel", "arbitrary"),
        name="expert_combine_layernorm",
    )(y, slot, gate, x3, gain, bias)


def _offsets(widths):
    offs = [0]
    for w in widths:
        offs.append(offs[-1] + w)
    return offs


def kernel(x, w_in, b_gate, w_branch, w_out, diff_lambda, diff_subln, sink_logit, na_rpb,
           w_router, w_exp_gate, w_exp_up, w_exp_down, ln_gain, ln_bias):
    bsz, seq, d = x.shape
    depth = w_in.shape[0]
    m_tot = bsz * seq
    alpha = (2.0 * depth) ** 0.25
    cap = EC_CAPACITY_FACTOR * seq // N_EXPERTS
    in_widths = (A_W, A_W, A_W, B_W, B_W, B_W, C_Q_W, C_KV_W, C_KV_W, D_W, D_W, D_W, N_BRANCHES * d)
    io = _offsets(in_widths)
    bo = _offsets((A_OUT_W, B_W, C_Q_W, D_W))
    cos128, sin128 = _rope_tables(seq, HEAD_DIM // 2)
    cos64, sin64 = _rope_tables(seq, B_DIM // 2)

    xf = x.reshape(m_tot, d)
    xb = xf.astype(BF16)
    for l in range(depth):
        lambda_init = 0.8 - 0.6 * math.exp(-0.3 * l)
        wl = w_in[l]

        def cols(a, b):
            return wl[:, io[a]:io[b]].astype(BF16)

        qk_a = _proj(xb, cols(0, 2), "rope128", A_OUT_W, seq, (cos128, sin128))
        v_a = _proj(xb, cols(2, 3), "plain", A_OUT_W, seq)
        qk_b = _proj(xb, cols(3, 5), "rope64", min(1024, 2 * B_W), seq, (cos64, sin64))
        v_b = _proj(xb, cols(5, 6), "plain", B_W, seq)
        qk_c = _proj(xb, cols(6, 8), "rope128", C_Q_W + C_KV_W, seq, (cos128, sin128))
        v_c = _proj(xb, cols(8, 9), "plain", C_KV_W, seq)
        qkv_d = _proj(xb, cols(9, 12), "plain", D_W, seq)
        gates = _proj(xb, cols(12, 13), "gate", min(1024, d), seq, (b_gate[l].reshape(1, -1),))

        outs, lses = [], []
        for g, (w, r) in enumerate(DIL_PAIRS):
            length = seq // r
            o_g, lse_g = _banded(
                qk_a.reshape(bsz, length, r * 2 * A_W), qk_a.reshape(bsz, length, r * 2 * A_W),
                v_a.reshape(bsz, length, r * A_W),
                functools.partial(lambda j, g: j * 2 * A_GROUPS + g, g=g),
                functools.partial(lambda j, g: j * 2 * A_GROUPS + A_GROUPS + g, g=g),
                functools.partial(lambda j, g: j * A_GROUPS + g, g=g),
                dil=r, hq=A_HEADS_PER_GROUP, group=1, halo=(w // 2) // r, want_lse=True)
            outs.append(o_g.reshape(m_tot, A_OUT_W))
            lses.append(lse_g.reshape(m_tot, LANES))
        o_a = _group_mix(outs, lses, seq)

        o_b = _diff_attention(qk_b.reshape(bsz, seq, 2 * B_W), v_b.reshape(bsz, seq, B_W),
                              diff_lambda[l], diff_subln[l].reshape(1, -1), lambda_init)

        (o_c,) = _banded(
            qk_c.reshape(bsz, seq, C_Q_W + C_KV_W), qk_c.reshape(bsz, seq, C_Q_W + C_KV_W),
            v_c.reshape(bsz, seq, C_KV_W),
            lambda j: 0, lambda j: C_Q_W // C_KV_W, lambda j: 0,
            dil=1, hq=C_Q_HEADS, group=C_Q_HEADS // C_KV_HEADS, halo=C_HALF_WINDOW, sink=sink_logit[l])

        o_d = _neighborhood_attention(qkv_d.reshape(bsz, seq, 3 * D_W), na_rpb[l])

        wb = w_branch[l].astype(BF16)
        mixed = _merge([o_a, o_b.reshape(m_tot, B_W), o_c.reshape(m_tot, C_Q_W), o_d.reshape(m_tot, D_W)],
                       [wb[bo[i]:bo[i + 1]] for i in range(N_BRANCHES)], gates, seq)
        xf, xb = _out_proj(mixed, w_out[l].astype(BF16), xf, ln_gain[l, 0].reshape(1, -1),
                           ln_bias[l, 0].reshape(1, -1), alpha, seq)

        xb3 = xb.reshape(bsz, seq, d)
        slot_t, slot, gate = _router(xb3, w_router[l].T.astype(BF16), cap)
        xin = _gather_tokens(xb3, slot_t, cap)
        y = _expert_ffn(xin, w_exp_gate[l].astype(BF16), w_exp_up[l].astype(BF16), w_exp_down[l].astype(BF16))
        xf3, xb3 = _combine(y, slot, gate, xf.reshape(bsz, seq, d), ln_gain[l, 1].reshape(1, -1),
                            ln_bias[l, 1].reshape(1, -1), alpha, cap)
        xf, xb = xf3.reshape(m_tot, d), xb3.reshape(m_tot, d)
    return xf.reshape(bsz, seq, d)
```

```python
import functools
import math

import jax
import jax.numpy as jnp
from jax import lax
from jax.experimental import pallas as pl
from jax.experimental.pallas import tpu as pltpu

F32 = jnp.float32
BF16 = jnp.bfloat16

LANES = 128
HEAD_DIM = 128
ROPE_THETA = 10000.0
NEG_INF = -1e30
LN_EPS = 1e-5
SUBLN_EPS = 1e-5
DIL_PAIRS = ((128, 1), (512, 4), (2048, 16))
A_HEADS_PER_GROUP = 6
A_GROUPS = len(DIL_PAIRS)
A_HEADS = A_GROUPS * A_HEADS_PER_GROUP
B_HEADS = 8
B_DIM = 64
C_Q_HEADS = 8
C_KV_HEADS = 2
C_HALF_WINDOW = 128
D_HEADS = 8
GRID_W = 64
NA_WIN_H = 8
NA_WIN_W = 16
NA_Q_ROWS = 8
NA_K_ROWS = 16
N_BRANCHES = 4
N_EXPERTS = 16
EC_CAPACITY_FACTOR = 2
EXPERT_PAD = 128

A_W = A_HEADS * HEAD_DIM
B_W = B_HEADS * 2 * B_DIM
C_Q_W = C_Q_HEADS * HEAD_DIM
C_KV_W = C_KV_HEADS * HEAD_DIM
D_W = D_HEADS * HEAD_DIM
A_OUT_W = A_HEADS_PER_GROUP * HEAD_DIM

VMEM_LIMIT = 56 * 1024 * 1024


def _params(*sem):
    return pltpu.CompilerParams(dimension_semantics=sem, vmem_limit_bytes=VMEM_LIMIT)


def _dot_nt(a, b):
    return lax.dot_general(a, b, (((1,), (1,)), ((), ())), preferred_element_type=F32)


def _proj_body(mode, x_ref, w_ref, *rest):
    acc = jnp.dot(x_ref[...], w_ref[...], preferred_element_type=F32)
    if mode == "plain":
        (o_ref,) = rest
        o_ref[...] = acc.astype(o_ref.dtype)
    elif mode == "gate":
        b_ref, o_ref = rest
        z = acc + b_ref[...]
        o_ref[...] = (1.0 / (1.0 + jnp.exp(-z))).astype(o_ref.dtype)
    else:
        cos_ref, sin_ref, o_ref = rest
        cos = cos_ref[...]
        sin = sin_ref[...]
        lane = lax.broadcasted_iota(jnp.int32, cos.shape, 1)
        for c in range(acc.shape[1] // LANES):
            a = acc[:, c * LANES:(c + 1) * LANES]
            if mode == "rope128":
                rot = pltpu.roll(a, 64, 1)
            else:
                rot = jnp.where((lane % 64) < 32, pltpu.roll(a, 96, 1), pltpu.roll(a, 32, 1))
            o_ref[:, c * LANES:(c + 1) * LANES] = (a * cos + rot * sin).astype(o_ref.dtype)


def _proj(xb, w, mode, tn, seq, extra=()):
    m_tot, k = xb.shape
    n_tot = w.shape[1]
    tm = min(1024, seq)
    pos_blocks = seq // tm
    in_specs = [pl.BlockSpec((tm, k), lambda n, m: (m, 0)),
                pl.BlockSpec((k, tn), lambda n, m: (0, n))]
    if mode == "gate":
        in_specs.append(pl.BlockSpec((1, tn), lambda n, m: (0, n)))
    elif mode != "plain":
        in_specs += [pl.BlockSpec((tm, LANES), lambda n, m: (m % pos_blocks, 0))] * 2
    return pl.pallas_call(
        functools.partial(_proj_body, mode),
        grid=(n_tot // tn, m_tot // tm),
        in_specs=in_specs,
        out_specs=pl.BlockSpec((tm, tn), lambda n, m: (m, n)),
        out_shape=jax.ShapeDtypeStruct((m_tot, n_tot), BF16),
        compiler_params=_params("parallel", "parallel"),
        name="proj_" + mode,
    )(xb, w, *extra)


def _proj_dilated_body(dil, x_ref, w_ref, cos_ref, sin_ref, o_ref, *scratch):
    n = pl.program_id(0)
    tm = x_ref.shape[0]
    rows = tm // dil
    acc = jnp.dot(x_ref[...], w_ref[...], preferred_element_type=F32)
    dst = scratch[0] if dil > 1 else None

    def emit(c, val):
        if dil > 1:
            dst[c] = val
        else:
            o_ref[0, :, c * LANES:(c + 1) * LANES] = val.astype(o_ref.dtype)

    @pl.when(n < 2)
    def _():
        cos = cos_ref[...]
        sin = sin_ref[...]
        for c in range(acc.shape[1] // LANES):
            a = acc[:, c * LANES:(c + 1) * LANES]
            emit(c, a * cos + pltpu.roll(a, 64, 1) * sin)

    @pl.when(n >= 2)
    def _():
        for c in range(acc.shape[1] // LANES):
            emit(c, acc[:, c * LANES:(c + 1) * LANES])

    if dil > 1:
        for c in range(acc.shape[1] // LANES):
            for j in range(dil):
                o_ref[j, :, c * LANES:(c + 1) * LANES] = (
                    dst.at[c][pl.ds(j, rows, stride=dil), :].astype(o_ref.dtype))


def _proj_dilated(xb, w, dil, bsz, seq, cos, sin):
    m_tot, k = xb.shape
    tm = min(1024, seq)
    tn = A_OUT_W
    pos_blocks = seq // tm
    scratch = [pltpu.VMEM((tn // LANES, tm, LANES), F32)] if dil > 1 else []
    return pl.pallas_call(
        functools.partial(_proj_dilated_body, dil),
        grid=(3, m_tot // tm),
        in_specs=[pl.BlockSpec((tm, k), lambda n, m: (m, 0)),
                  pl.BlockSpec((k, tn), lambda n, m: (0, n)),
                  pl.BlockSpec((tm, LANES), lambda n, m: (m % pos_blocks, 0)),
                  pl.BlockSpec((tm, LANES), lambda n, m: (m % pos_blocks, 0))],
        out_specs=pl.BlockSpec((None, dil, tm // dil, tn), lambda n, m: (m // pos_blocks, 0, m % pos_blocks, n)),
        out_shape=jax.ShapeDtypeStruct((bsz, dil, seq // dil, 3 * tn), BF16),
        scratch_shapes=scratch,
        compiler_params=_params("parallel", "parallel"),
        name="proj_dilated",
    )(xb, w, cos, sin)


def _rope_tables(seq, half):
    inv = ROPE_THETA ** (-jnp.arange(half, dtype=F32) / half)
    ang = jnp.arange(seq, dtype=jnp.int32).astype(F32)[:, None] * inv[None, :]
    cos, sin = jnp.cos(ang), jnp.sin(ang)
    reps = LANES // (2 * half)
    return (jnp.tile(jnp.concatenate([cos, cos], axis=1), (1, reps)),
            jnp.tile(jnp.concatenate([-sin, sin], axis=1), (1, reps)))


def _banded_body(hq, group, halo, tq, win, length, has_sink, want_lse, *refs):
    refs = list(refs)
    q_ref, k_ref, v_ref = refs[:3]
    pos = 3
    sink_ref = None
    if has_sink:
        sink_ref = refs[pos]
        pos += 1
    o_ref = refs[pos]
    lse_ref = refs[pos + 1] if want_lse else None

    i = pl.program_id(2)
    start = pl.multiple_of(jnp.clip(i * tq - halo, 0, length - win), 64)
    qpos = i * tq + lax.broadcasted_iota(jnp.int32, (tq, win), 0)
    kpos = start + lax.broadcasted_iota(jnp.int32, (tq, win), 1)
    valid = jnp.abs(qpos - kpos) <= halo
    scale = HEAD_DIM ** -0.5
    lane = lax.broadcasted_iota(jnp.int32, (tq, LANES), 1)
    lse_acc = jnp.zeros((tq, LANES), F32)
    for h in range(hq):
        kh = h // group
        q = q_ref[:, h * HEAD_DIM:(h + 1) * HEAD_DIM]
        k = k_ref[pl.ds(start, win), kh * HEAD_DIM:(kh + 1) * HEAD_DIM]
        v = v_ref[pl.ds(start, win), kh * HEAD_DIM:(kh + 1) * HEAD_DIM]
        s = _dot_nt(q, k) * scale
        s = jnp.where(valid, s, NEG_INF)
        m = jnp.max(s, axis=-1, keepdims=True)
        if has_sink:
            sk = sink_ref[h]
            m = jnp.maximum(m, sk)
        p = jnp.exp(s - m)
        d = jnp.sum(p, axis=-1, keepdims=True)
        if has_sink:
            d = d + jnp.exp(sk - m)
        o = jnp.dot(p.astype(BF16), v, preferred_element_type=F32) / d
        o_ref[:, h * HEAD_DIM:(h + 1) * HEAD_DIM] = o.astype(o_ref.dtype)
        if want_lse:
            lse_acc = jnp.where(lane == h, m + jnp.log(d), lse_acc)
    if want_lse:
        lse_ref[...] = lse_acc


def _banded(q4, k4, v4, qcol, kcol, vcol, *, hq, group, halo, sink=None, want_lse=False):
    bsz, dil, length, _ = q4.shape
    hkv = hq // group
    tq = min(256, length)
    win = min(length, tq + 2 * halo)
    qw, kw = hq * HEAD_DIM, hkv * HEAD_DIM
    in_specs = [pl.BlockSpec((None, None, tq, qw), lambda b, j, i: (b, j, i, qcol)),
                pl.BlockSpec((None, None, length, kw), lambda b, j, i: (b, j, 0, kcol)),
                pl.BlockSpec((None, None, length, kw), lambda b, j, i: (b, j, 0, vcol))]
    args = [q4, k4, v4]
    if sink is not None:
        in_specs.append(pl.BlockSpec(memory_space=pltpu.SMEM))
        args.append(sink)
    out_specs = [pl.BlockSpec((None, None, tq, qw), lambda b, j, i: (b, j, i, 0))]
    out_shape = [jax.ShapeDtypeStruct((bsz, dil, length, qw), BF16)]
    if want_lse:
        out_specs.append(pl.BlockSpec((None, None, tq, LANES), lambda b, j, i: (b, j, i, 0)))
        out_shape.append(jax.ShapeDtypeStruct((bsz, dil, length, LANES), F32))
    return pl.pallas_call(
        functools.partial(_banded_body, hq, group, halo, tq, win, length, sink is not None, want_lse),
        grid=(bsz, dil, length // tq),
        in_specs=in_specs,
        out_specs=out_specs,
        out_shape=out_shape,
        compiler_params=_params("parallel", "parallel", "parallel"),
        name="banded_attention",
    )(*args)


def _group_mix_body(dils, *refs):
    n = len(dils)
    o_refs, l_refs, out_ref = refs[:n], refs[n:2 * n], refs[2 * n]
    scratch = list(refs[2 * n + 1:])
    tm = out_ref.shape[0]
    heads, lses = [], []
    for g, dil in enumerate(dils):
        if dil == 1:
            heads.append(functools.partial(
                lambda h, ref: ref[0, :, h * HEAD_DIM:(h + 1) * HEAD_DIM].astype(F32), ref=o_refs[g]))
            lses.append(l_refs[g][0])
            continue
        o_scr, l_scr = scratch.pop(0), scratch.pop(0)
        rows = tm // dil
        for j in range(dil):
            l_scr[pl.ds(j, rows, stride=dil), :] = l_refs[g][j]
            for h in range(A_HEADS_PER_GROUP):
                o_scr.at[h][pl.ds(j, rows, stride=dil), :] = (
                    o_refs[g][j, :, h * HEAD_DIM:(h + 1) * HEAD_DIM].astype(F32))
        heads.append(functools.partial(lambda h, ref: ref[h], ref=o_scr))
        lses.append(l_scr[...])
    m = functools.reduce(jnp.maximum, lses)
    es = [jnp.exp(l - m) for l in lses]
    tot = functools.reduce(lambda a, b: a + b, es)
    ws = [e / tot for e in es]
    for h in range(A_HEADS_PER_GROUP):
        acc = ws[0][:, h:h + 1] * heads[0](h)
        for g in range(1, n):
            acc = acc + ws[g][:, h:h + 1] * heads[g](h)
        out_ref[:, h * HEAD_DIM:(h + 1) * HEAD_DIM] = acc.astype(out_ref.dtype)


def _group_mix(outs, lses, seq):
    bsz = outs[0].shape[0]
    dils = tuple(o.shape[1] for o in outs)
    tm = min(512, seq)
    pos_blocks = seq // tm

    def spec(dil, width):
        return pl.BlockSpec((None, dil, tm // dil, width), lambda m: (m // pos_blocks, 0, m % pos_blocks, 0))

    scratch = []
    for dil in dils:
        if dil > 1:
            scratch += [pltpu.VMEM((A_HEADS_PER_GROUP, tm, HEAD_DIM), F32), pltpu.VMEM((tm, LANES), F32)]
    return pl.pallas_call(
        functools.partial(_group_mix_body, dils),
        grid=(bsz * pos_blocks,),
        in_specs=[spec(dil, A_OUT_W) for dil in dils] + [spec(dil, LANES) for dil in dils],
        out_specs=pl.BlockSpec((tm, A_OUT_W), lambda m: (m, 0)),
        out_shape=jax.ShapeDtypeStruct((bsz * seq, A_OUT_W), BF16),
        scratch_shapes=scratch,
        compiler_params=_params("parallel"),
        name="dilation_group_mix",
    )(*outs, *lses)


def _diff_body(lambda_init, lam_ref, subln_ref, q_ref, k_ref, v_ref, o_ref):
    lam = lam_ref[...]
    dot1 = jnp.sum(lam[0:1, :] * lam[1:2, :], axis=1, keepdims=True)
    dot2 = jnp.sum(lam[2:3, :] * lam[3:4, :], axis=1, keepdims=True)
    lmbda = jnp.exp(dot1) - jnp.exp(dot2) + lambda_init
    q = (q_ref[...].astype(F32) * (B_DIM ** -0.5)).astype(BF16)
    k = k_ref[...]
    v = v_ref[...]
    lane = lax.broadcasted_iota(jnp.int32, q.shape, 1)
    zero = jnp.zeros_like(q)

    def softmax_map_times_v(qm):
        s = _dot_nt(qm, k)
        p = jnp.exp(s - jnp.max(s, axis=-1, keepdims=True))
        return jnp.dot(p.astype(BF16), v, preferred_element_type=F32) / jnp.sum(p, axis=-1, keepdims=True)

    o = (softmax_map_times_v(jnp.where(lane < B_DIM, q, zero))
         - lmbda * softmax_map_times_v(jnp.where(lane >= B_DIM, q, zero)))
    o = o * lax.rsqrt(jnp.mean(o * o, axis=-1, keepdims=True) + SUBLN_EPS) * subln_ref[...]
    o_ref[...] = (o * (1.0 - lambda_init)).astype(o_ref.dtype)


def _diff_attention(qk3, v3, lam, subln, lambda_init):
    bsz, seq, _ = qk3.shape
    tq = min(256, seq)
    hw = 2 * B_DIM
    return pl.pallas_call(
        functools.partial(_diff_body, lambda_init),
        grid=(bsz, B_HEADS, seq // tq),
        in_specs=[pl.BlockSpec((4, B_DIM), lambda b, h, i: (0, 0)),
                  pl.BlockSpec((1, hw), lambda b, h, i: (0, 0)),
                  pl.BlockSpec((None, tq, hw), lambda b, h, i: (b, i, h)),
                  pl.BlockSpec((None, seq, hw), lambda b, h, i: (b, 0, B_HEADS + h)),
                  pl.BlockSpec((None, seq, hw), lambda b, h, i: (b, 0, h))],
        out_specs=pl.BlockSpec((None, tq, hw), lambda b, h, i: (b, i, h)),
        out_shape=jax.ShapeDtypeStruct((bsz, seq, B_W), BF16),
        compiler_params=_params("parallel", "parallel", "parallel"),
        name="diff_attention",
    )(lam, subln, qk3, qk3, v3)


def _na_body(rows, k_rows, q_ref, k_ref, v_ref, bias_ref, o_ref):
    rb = pl.program_id(2)
    n_rb = rows // NA_Q_ROWS
    w0 = jnp.clip(rb * NA_Q_ROWS - NA_WIN_H // 2, 0, rows - k_rows) * GRID_W
    w0 = pl.multiple_of(w0, 256)
    kind = jnp.where(rb == 0, 0, jnp.where(rb == n_rb - 1, 2, 1))
    k = k_ref[pl.ds(w0, k_rows * GRID_W), :]
    v = v_ref[pl.ds(w0, k_rows * GRID_W), :]
    s = _dot_nt(q_ref[...], k) * (HEAD_DIM ** -0.5) + bias_ref[kind]
    p = jnp.exp(s - jnp.max(s, axis=-1, keepdims=True))
    d = jnp.sum(p, axis=-1, keepdims=True)
    o = jnp.dot(p.astype(BF16), v, preferred_element_type=F32) / d
    o_ref[...] = o.astype(o_ref.dtype)


def _na_bias_tables(rpb, rows):
    kh = min(NA_WIN_H, rows)
    k_rows = min(NA_K_ROWS, rows)
    n_rb = rows // NA_Q_ROWS
    n_heads = rpb.shape[0]
    edge = GRID_W - NA_WIN_W
    ext = jnp.pad(rpb.astype(F32), ((0, 0), (0, 0), (edge, edge)), mode="edge")
    toeplitz = jnp.stack([ext[:, :, GRID_W - 1 - qc:2 * GRID_W - 1 - qc] for qc in range(GRID_W)], axis=2)
    c = jnp.arange(GRID_W)
    col_start = jnp.clip(c - NA_WIN_W // 2, 0, GRID_W - NA_WIN_W)
    col_ok = (c[None, :] >= col_start[:, None]) & (c[None, :] < col_start[:, None] + NA_WIN_W)
    toeplitz = jnp.where(col_ok[None, None], toeplitz, NEG_INF)
    tabs = []
    for rb in (0, min(1, n_rb - 1), n_rb - 1):
        r0 = rb * NA_Q_ROWS
        w0 = min(max(r0 - NA_WIN_H // 2, 0), rows - k_rows)
        strips = []
        for a in range(NA_Q_ROWS):
            qrow = r0 + a
            start = min(max(qrow - kh // 2, 0), rows - kh)
            first = start - qrow + (NA_WIN_H - 1)
            blk = toeplitz[:, first:first + kh].transpose(0, 2, 1, 3)
            blk = jnp.pad(blk, ((0, 0), (0, 0), (start - w0, k_rows - kh - (start - w0)), (0, 0)),
                          constant_values=NEG_INF)
            strips.append(blk.reshape(n_heads, GRID_W, k_rows * GRID_W))
        tabs.append(jnp.concatenate(strips, axis=1))
    return jnp.stack(tabs, axis=1)


def _neighborhood_attention(qkv3, rpb):
    bsz, seq, _ = qkv3.shape
    rows = seq // GRID_W
    k_rows = min(NA_K_ROWS, rows)
    tq = NA_Q_ROWS * GRID_W
    bias = _na_bias_tables(rpb, rows)
    return pl.pallas_call(
        functools.partial(_na_body, rows, k_rows),
        grid=(bsz, D_HEADS, rows // NA_Q_ROWS),
        in_specs=[pl.BlockSpec((None, tq, HEAD_DIM), lambda b, h, r: (b, r, h)),
                  pl.BlockSpec((None, seq, HEAD_DIM), lambda b, h, r: (b, 0, D_HEADS + h)),
                  pl.BlockSpec((None, seq, HEAD_DIM), lambda b, h, r: (b, 0, 2 * D_HEADS + h)),
                  pl.BlockSpec((None, 3, tq, k_rows * GRID_W), lambda b, h, r: (h, 0, 0, 0))],
        out_specs=pl.BlockSpec((None, tq, HEAD_DIM), lambda b, h, r: (b, r, h)),
        out_shape=jax.ShapeDtypeStruct((bsz, seq, D_W), BF16),
        compiler_params=_params("parallel", "parallel", "parallel"),
        name="neighborhood_attention",
    )(qkv3, qkv3, qkv3, bias)


def _merge_body(oa_ref, ob_ref, oc_ref, od_ref, wa_ref, wb_ref, wc_ref, wd_ref,
                ga_ref, gb_ref, gc_ref, gd_ref, out_ref):
    acc = ga_ref[...].astype(F32) * jnp.dot(oa_ref[...], wa_ref[...], preferred_element_type=F32)
    acc = acc + gb_ref[...].astype(F32) * jnp.dot(ob_ref[...], wb_ref[...], preferred_element_type=F32)
    acc = acc + gc_ref[...].astype(F32) * jnp.dot(oc_ref[...], wc_ref[...], preferred_element_type=F32)
    acc = acc + gd_ref[...].astype(F32) * jnp.dot(od_ref[...], wd_ref[...], preferred_element_type=F32)
    out_ref[...] = acc.astype(out_ref.dtype)


def _merge(branches, weights, gates, seq):
    m_tot = branches[0].shape[0]
    d = weights[0].shape[1]
    tm = min(1024, seq)
    tn = min(512, d)
    n_blocks = d // tn
    in_specs = [pl.BlockSpec((tm, o.shape[1]), lambda m, n: (m, 0)) for o in branches]
    in_specs += [pl.BlockSpec((w.shape[0], tn), lambda m, n: (0, n)) for w in weights]
    in_specs += [pl.BlockSpec((tm, tn), functools.partial(lambda m, n, i: (m, i * n_blocks + n), i=i))
                 for i in range(N_BRANCHES)]
    return pl.pallas_call(
        _merge_body,
        grid=(m_tot // tm, n_blocks),
        in_specs=in_specs,
        out_specs=pl.BlockSpec((tm, tn), lambda m, n: (m, n)),
        out_shape=jax.ShapeDtypeStruct((m_tot, d), BF16),
        compiler_params=_params("parallel", "parallel"),
        name="branch_merge",
    )(*branches, *weights, *([gates] * N_BRANCHES))


def _layer_norm_rows(y, gain, bias):
    mu = jnp.mean(y, axis=-1, keepdims=True)
    var = jnp.mean(jnp.square(y - mu), axis=-1, keepdims=True)
    return (y - mu) * lax.rsqrt(var + LN_EPS) * gain + bias


def _out_proj_body(alpha, mixed_ref, w_ref, x_ref, gain_ref, bias_ref, xo_ref, xb_ref):
    y = alpha * x_ref[...] + jnp.dot(mixed_ref[...], w_ref[...], preferred_element_type=F32)
    out = _layer_norm_rows(y, gain_ref[...], bias_ref[...])
    xo_ref[...] = out
    xb_ref[...] = out.astype(BF16)


def _out_proj(mixed, w, x, gain, bias, alpha, seq):
    m_tot, d = x.shape
    tm = min(256, seq)
    row = pl.BlockSpec((tm, d), lambda m: (m, 0))
    vec = pl.BlockSpec((1, d), lambda m: (0, 0))
    return pl.pallas_call(
        functools.partial(_out_proj_body, alpha),
        grid=(m_tot // tm,),
        in_specs=[row, pl.BlockSpec((d, d), lambda m: (0, 0)), row, vec, vec],
        out_specs=[row, row],
        out_shape=[jax.ShapeDtypeStruct((m_tot, d), F32), jax.ShapeDtypeStruct((m_tot, d), BF16)],
        compiler_params=_params("parallel"),
        name="out_proj_layernorm",
    )(mixed, w, x, gain, bias)


def _router_body(cap, x_ref, wrt_ref, slot_t_ref, slot_ref, gate_ref):
    seq = x_ref.shape[0]
    logits = _dot_nt(wrt_ref[...], x_ref[...])
    e = jnp.exp(logits - jnp.max(logits, axis=0, keepdims=True))
    aff = e / jnp.sum(e, axis=0, keepdims=True)
    bits = lax.bitcast_convert_type(aff, jnp.int32)
    thr = jnp.zeros((N_EXPERTS, 1), jnp.int32)
    for bit in range(30, -1, -1):
        cand = thr | (1 << bit)
        cnt = jnp.sum((bits >= cand).astype(F32), axis=1, keepdims=True)
        thr = jnp.where(cnt >= cap, cand, thr)
    above = bits > thr
    tied = bits == thr
    need = cap - jnp.sum(above.astype(F32), axis=1, keepdims=True)

    chunk = min(512, seq)
    upper = (lax.broadcasted_iota(jnp.int32, (chunk, chunk), 0)
             <= lax.broadcasted_iota(jnp.int32, (chunk, chunk), 1)).astype(BF16)

    def prefix_count(mask):
        parts, carry = [], jnp.zeros((N_EXPERTS, 1), F32)
        for c in range(seq // chunk):
            part = jnp.dot(mask[:, c * chunk:(c + 1) * chunk].astype(BF16), upper,
                           preferred_element_type=F32) + carry
            parts.append(part)
            carry = part[:, chunk - 1:chunk]
        return jnp.concatenate(parts, axis=1)

    sel = above | (tied & (prefix_count(tied) <= need))
    slot = jnp.where(sel, prefix_count(sel) - 1.0, -1.0)
    gate = jnp.where(sel, aff, 0.0)
    slot_t_ref[...] = slot.astype(jnp.int32)
    pad = EXPERT_PAD - N_EXPERTS
    slot_ref[...] = jnp.concatenate([slot, jnp.full((pad, seq), -1.0, F32)], axis=0).T
    gate_ref[...] = jnp.concatenate([gate, jnp.zeros((pad, seq), F32)], axis=0).T


def _router(xb3, wrt, cap):
    bsz, seq, d = xb3.shape
    return pl.pallas_call(
        functools.partial(_router_body, cap),
        grid=(bsz,),
        in_specs=[pl.BlockSpec((None, seq, d), lambda b: (b, 0, 0)),
                  pl.BlockSpec((N_EXPERTS, d), lambda b: (0, 0))],
        out_specs=[pl.BlockSpec((None, N_EXPERTS, seq), lambda b: (b, 0, 0)),
                   pl.BlockSpec((None, seq, EXPERT_PAD), lambda b: (b, 0, 0)),
                   pl.BlockSpec((None, seq, EXPERT_PAD), lambda b: (b, 0, 0))],
        out_shape=[jax.ShapeDtypeStruct((bsz, N_EXPERTS, seq), jnp.int32),
                   jax.ShapeDtypeStruct((bsz, seq, EXPERT_PAD), F32),
                   jax.ShapeDtypeStruct((bsz, seq, EXPERT_PAD), F32)],
        compiler_params=_params("parallel"),
        name="router_topk",
    )(xb3, wrt)


def _gather_body(cap, x_ref, slot_t_ref, o_ref):
    e = pl.program_id(1)
    seq = x_ref.shape[0]
    row = slot_t_ref[pl.ds(e, 1), :]
    onehot = (lax.broadcasted_iota(jnp.int32, (cap, seq), 0) == row).astype(BF16)
    o_ref[...] = jnp.dot(onehot, x_ref[...], preferred_element_type=F32).astype(o_ref.dtype)


def _gather_tokens(xb3, slot_t, cap):
    bsz, seq, d = xb3.shape
    return pl.pallas_call(
        functools.partial(_gather_body, cap),
        grid=(bsz, N_EXPERTS),
        in_specs=[pl.BlockSpec((None, seq, d), lambda b, e: (b, 0, 0)),
                  pl.BlockSpec((None, N_EXPERTS, seq), lambda b, e: (b, 0, 0))],
        out_specs=pl.BlockSpec((None, cap, d), lambda b, e: (e, b, 0)),
        out_shape=jax.ShapeDtypeStruct((N_EXPERTS, bsz * cap, d), BF16),
        compiler_params=_params("parallel", "parallel"),
        name="expert_gather",
    )(xb3, slot_t)


def _expert_body(x_ref, wg_ref, wu_ref, wd_ref, o_ref):
    x = x_ref[...]
    g = jnp.dot(x, wg_ref[...], preferred_element_type=F32)
    u = jnp.dot(x, wu_ref[...], preferred_element_type=F32)
    h = (g / (1.0 + jnp.exp(-g))) * u
    o_ref[...] = jnp.dot(h.astype(BF16), wd_ref[...], preferred_element_type=F32).astype(o_ref.dtype)


def _expert_ffn(xin, wg, wu, wd):
    n_exp, rows, d = xin.shape
    f = wg.shape[2]
    tm = min(512, rows)
    return pl.pallas_call(
        _expert_body,
        grid=(n_exp, rows // tm),
        in_specs=[pl.BlockSpec((None, tm, d), lambda e, m: (e, m, 0)),
                  pl.BlockSpec((None, d, f), lambda e, m: (e, 0, 0)),
                  pl.BlockSpec((None, d, f), lambda e, m: (e, 0, 0)),
                  pl.BlockSpec((None, f, d), lambda e, m: (e, 0, 0))],
        out_specs=pl.BlockSpec((None, tm, d), lambda e, m: (e, m, 0)),
        out_shape=jax.ShapeDtypeStruct((n_exp, rows, d), BF16),
        compiler_params=_params("parallel", "parallel"),
        name="expert_swiglu",
    )(xin, wg, wu, wd)


def _combine_body(alpha, cap, y_ref, slot_ref, gate_ref, x_ref, gain_ref, bias_ref, xo_ref, xb_ref, acc_ref):
    e = pl.program_id(2)

    @pl.when(e == 0)
    def _():
        acc_ref[...] = jnp.zeros_like(acc_ref)

    tt = slot_ref.shape[0]
    lane = lax.broadcasted_iota(jnp.int32, (tt, EXPERT_PAD), 1)
    pick = lane == e
    slot_col = jnp.sum(jnp.where(pick, slot_ref[...], 0.0), axis=1, keepdims=True)
    gate_col = jnp.sum(jnp.where(pick, gate_ref[...], 0.0), axis=1, keepdims=True)
    onehot = (lax.broadcasted_iota(jnp.int32, (tt, cap), 1).astype(F32) == slot_col).astype(BF16)
    acc_ref[...] += gate_col * jnp.dot(onehot, y_ref[...], preferred_element_type=F32)

    @pl.when(e == N_EXPERTS - 1)
    def _():
        out = _layer_norm_rows(alpha * x_ref[...] + acc_ref[...], gain_ref[...], bias_ref[...])
        xo_ref[...] = out
        xb_ref[...] = out.astype(BF16)


def _combine(y, slot, gate, x3, gain, bias, alpha, cap):
    bsz, seq, d = x3.shape
    tt = min(512, seq)
    tok = pl.BlockSpec((None, tt, d), lambda b, t, e: (b, t, 0))
    sel = pl.BlockSpec((None, tt, EXPERT_PAD), lambda b, t, e: (b, t, 0))
    vec = pl.BlockSpec((1, d), lambda b, t, e: (0, 0))
    return pl.pallas_call(
        functools.partial(_combine_body, alpha, cap),
        grid=(bsz, seq // tt, N_EXPERTS),
        in_specs=[pl.BlockSpec((None, cap, d), lambda b, t, e: (e, b, 0)), sel, sel, tok, vec, vec],
        out_specs=[tok, tok],
        out_shape=[jax.ShapeDtypeStruct((bsz, seq, d), F32), jax.ShapeDtypeStruct((bsz, seq, d), BF16)],
        scratch_shapes=[pltpu.VMEM((tt, d), F32)],
        compiler_params=_params("parallel", "parallel", "arbitrary"),
        name="expert_combine_layernorm",
    )(y, slot, gate, x3, gain, bias)


def _offsets(widths):
    offs = [0]
    for w in widths:
        offs.append(offs[-1] + w)
    return offs


def kernel(x, w_in, b_gate, w_branch, w_out, diff_lambda, diff_subln, sink_logit, na_rpb,
           w_router, w_exp_gate, w_exp_up, w_exp_down, ln_gain, ln_bias):
    bsz, seq, d = x.shape
    depth = w_in.shape[0]
    m_tot = bsz * seq
    alpha = (2.0 * depth) ** 0.25
    cap = EC_CAPACITY_FACTOR * seq // N_EXPERTS
    in_widths = (A_W, A_W, A_W, B_W, B_W, B_W, C_Q_W, C_KV_W, C_KV_W, D_W, D_W, D_W, N_BRANCHES * d)
    io = _offsets(in_widths)
    bo = _offsets((A_OUT_W, B_W, C_Q_W, D_W))
    cos128, sin128 = _rope_tables(seq, HEAD_DIM // 2)
    cos64, sin64 = _rope_tables(seq, B_DIM // 2)

    xf = x.reshape(m_tot, d)
    xb = xf.astype(BF16)
    for l in range(depth):
        lambda_init = 0.8 - 0.6 * math.exp(-0.3 * l)
        wl = w_in[l]

        def cols(a, b):
            return wl[:, io[a]:io[b]].astype(BF16)

        qk_b = _proj(xb, cols(3, 5), "rope64", min(1024, 2 * B_W), seq, (cos64, sin64))
        v_b = _proj(xb, cols(5, 6), "plain", B_W, seq)
        qk_c = _proj(xb, cols(6, 8), "rope128", C_Q_W + C_KV_W, seq, (cos128, sin128))
        v_c = _proj(xb, cols(8, 9), "plain", C_KV_W, seq)
        qkv_d = _proj(xb, cols(9, 12), "plain", D_W, seq)
        gates = _proj(xb, cols(12, 13), "gate", min(1024, d), seq, (b_gate[l].reshape(1, -1),))

        outs, lses = [], []
        for g, (w, r) in enumerate(DIL_PAIRS):
            w_g = jnp.concatenate([wl[:, io[t] + g * A_OUT_W:io[t] + (g + 1) * A_OUT_W] for t in range(3)],
                                  axis=1).astype(BF16)
            qkv_g = _proj_dilated(xb, w_g, r, bsz, seq, cos128, sin128)
            o_g, lse_g = _banded(qkv_g, qkv_g, qkv_g, 0, 1, 2, hq=A_HEADS_PER_GROUP, group=1,
                                 halo=(w // 2) // r, want_lse=True)
            outs.append(o_g)
            lses.append(lse_g)
        o_a = _group_mix(outs, lses, seq)

        o_b = _diff_attention(qk_b.reshape(bsz, seq, 2 * B_W), v_b.reshape(bsz, seq, B_W),
                              diff_lambda[l], diff_subln[l].reshape(1, -1), lambda_init)

        qk_c4 = qk_c.reshape(bsz, 1, seq, C_Q_W + C_KV_W)
        (o_c,) = _banded(qk_c4, qk_c4, v_c.reshape(bsz, 1, seq, C_KV_W), 0, C_Q_W // C_KV_W, 0,
                         hq=C_Q_HEADS, group=C_Q_HEADS // C_KV_HEADS, halo=C_HALF_WINDOW, sink=sink_logit[l])

        o_d = _neighborhood_attention(qkv_d.reshape(bsz, seq, 3 * D_W), na_rpb[l])

        wb = w_branch[l].astype(BF16)
        mixed = _merge([o_a, o_b.reshape(m_tot, B_W), o_c.reshape(m_tot, C_Q_W), o_d.reshape(m_tot, D_W)],
                       [wb[bo[i]:bo[i + 1]] for i in range(N_BRANCHES)], gates, seq)
        xf, xb = _out_proj(mixed, w_out[l].astype(BF16), xf, ln_gain[l, 0].reshape(1, -1),
                           ln_bias[l, 0].reshape(1, -1), alpha, seq)

        xb3 = xb.reshape(bsz, seq, d)
        slot_t, slot, gate = _router(xb3, w_router[l].T.astype(BF16), cap)
        xin = _gather_tokens(xb3, slot_t, cap)
        y = _expert_ffn(xin, w_exp_gate[l].astype(BF16), w_exp_up[l].astype(BF16), w_exp_down[l].astype(BF16))
        xf3, xb3 = _combine(y, slot, gate, xf.reshape(bsz, seq, d), ln_gain[l, 1].reshape(1, -1),
                            ln_bias[l, 1].reshape(1, -1), alpha, cap)
        xf, xb = xf3.reshape(m_tot, d), xb3.reshape(m_tot, d)
    return xf.reshape(bsz, seq, d)
```

```python
import functools
import math

import jax
import jax.numpy as jnp
from jax import lax
from jax.experimental import pallas as pl
from jax.experimental.pallas import tpu as pltpu

F32 = jnp.float32
BF16 = jnp.bfloat16

LANES = 128
HEAD_DIM = 128
ROPE_THETA = 10000.0
NEG_INF = -1e30
LN_EPS = 1e-5
SUBLN_EPS = 1e-5
DIL_PAIRS = ((128, 1), (512, 4), (2048, 16))
A_HEADS_PER_GROUP = 6
A_GROUPS = len(DIL_PAIRS)
A_HEADS = A_GROUPS * A_HEADS_PER_GROUP
B_HEADS = 8
B_DIM = 64
C_Q_HEADS = 8
C_KV_HEADS = 2
C_HALF_WINDOW = 128
D_HEADS = 8
GRID_W = 64
NA_WIN_H = 8
NA_WIN_W = 16
NA_Q_ROWS = 8
NA_K_ROWS = 16
NA_BLOCKS_PER_STEP = 2
DIFF_Q_ROWS = 512
DIFF_SUB_ROWS = 256
DIFF_Q_SCALE = (B_DIM ** -0.5) * math.log2(math.e)
N_BRANCHES = 4
N_EXPERTS = 16
EC_CAPACITY_FACTOR = 2
EXPERT_PAD = 128

A_W = A_HEADS * HEAD_DIM
B_W = B_HEADS * 2 * B_DIM
C_Q_W = C_Q_HEADS * HEAD_DIM
C_KV_W = C_KV_HEADS * HEAD_DIM
D_W = D_HEADS * HEAD_DIM
A_OUT_W = A_HEADS_PER_GROUP * HEAD_DIM

VMEM_LIMIT = 56 * 1024 * 1024


def _params(*sem):
    return pltpu.CompilerParams(dimension_semantics=sem, vmem_limit_bytes=VMEM_LIMIT)


def _dot_nt(a, b):
    return lax.dot_general(a, b, (((1,), (1,)), ((), ())), preferred_element_type=F32)


def _proj_body(mode, x_ref, w_ref, *rest):
    acc = jnp.dot(x_ref[...], w_ref[...], preferred_element_type=F32)
    if mode == "plain":
        (o_ref,) = rest
        o_ref[...] = acc.astype(o_ref.dtype)
    elif mode == "gate":
        b_ref, o_ref = rest
        z = acc + b_ref[...]
        o_ref[...] = (1.0 / (1.0 + jnp.exp(-z))).astype(o_ref.dtype)
    else:
        cos_ref, sin_ref, o_ref = rest
        cos = cos_ref[...]
        sin = sin_ref[...]
        if mode == "rope64":
            qs = jnp.where(pl.program_id(0) == 0, DIFF_Q_SCALE, 1.0).astype(F32)
            cos = cos * qs
            sin = sin * qs
        lane = lax.broadcasted_iota(jnp.int32, cos.shape, 1)
        for c in range(acc.shape[1] // LANES):
            a = acc[:, c * LANES:(c + 1) * LANES]
            if mode == "rope128":
                rot = pltpu.roll(a, 64, 1)
            else:
                rot = jnp.where((lane % 64) < 32, pltpu.roll(a, 96, 1), pltpu.roll(a, 32, 1))
            o_ref[:, c * LANES:(c + 1) * LANES] = (a * cos + rot * sin).astype(o_ref.dtype)


def _proj(xb, w, mode, tn, seq, extra=()):
    m_tot, k = xb.shape
    n_tot = w.shape[1]
    tm = min(1024, seq)
    pos_blocks = seq // tm
    in_specs = [pl.BlockSpec((tm, k), lambda n, m: (m, 0)),
                pl.BlockSpec((k, tn), lambda n, m: (0, n))]
    if mode == "gate":
        in_specs.append(pl.BlockSpec((1, tn), lambda n, m: (0, n)))
    elif mode != "plain":
        in_specs += [pl.BlockSpec((tm, LANES), lambda n, m: (m % pos_blocks, 0))] * 2
    return pl.pallas_call(
        functools.partial(_proj_body, mode),
        grid=(n_tot // tn, m_tot // tm),
        in_specs=in_specs,
        out_specs=pl.BlockSpec((tm, tn), lambda n, m: (m, n)),
        out_shape=jax.ShapeDtypeStruct((m_tot, n_tot), BF16),
        compiler_params=_params("parallel", "parallel"),
        name="proj_" + mode,
    )(xb, w, *extra)


def _proj_dilated_body(dil, x_ref, w_ref, cos_ref, sin_ref, o_ref, *scratch):
    n = pl.program_id(0)
    tm = x_ref.shape[0]
    rows = tm // dil
    acc = jnp.dot(x_ref[...], w_ref[...], preferred_element_type=F32)
    dst = scratch[0] if dil > 1 else None

    def emit(c, val):
        if dil > 1:
            dst[c] = val
        else:
            o_ref[0, :, c * LANES:(c + 1) * LANES] = val.astype(o_ref.dtype)

    @pl.when(n < 2)
    def _():
        cos = cos_ref[...]
        sin = sin_ref[...]
        for c in range(acc.shape[1] // LANES):
            a = acc[:, c * LANES:(c + 1) * LANES]
            emit(c, a * cos + pltpu.roll(a, 64, 1) * sin)

    @pl.when(n >= 2)
    def _():
        for c in range(acc.shape[1] // LANES):
            emit(c, acc[:, c * LANES:(c + 1) * LANES])

    if dil > 1:
        for c in range(acc.shape[1] // LANES):
            for j in range(dil):
                o_ref[j, :, c * LANES:(c + 1) * LANES] = (
                    dst.at[c][pl.ds(j, rows, stride=dil), :].astype(o_ref.dtype))


def _proj_dilated(xb, w, dil, bsz, seq, cos, sin):
    m_tot, k = xb.shape
    tm = min(1024, seq)
    tn = A_OUT_W
    pos_blocks = seq // tm
    scratch = [pltpu.VMEM((tn // LANES, tm, LANES), F32)] if dil > 1 else []
    return pl.pallas_call(
        functools.partial(_proj_dilated_body, dil),
        grid=(3, m_tot // tm),
        in_specs=[pl.BlockSpec((tm, k), lambda n, m: (m, 0)),
                  pl.BlockSpec((k, tn), lambda n, m: (0, n)),
                  pl.BlockSpec((tm, LANES), lambda n, m: (m % pos_blocks, 0)),
                  pl.BlockSpec((tm, LANES), lambda n, m: (m % pos_blocks, 0))],
        out_specs=pl.BlockSpec((None, dil, tm // dil, tn), lambda n, m: (m // pos_blocks, 0, m % pos_blocks, n)),
        out_shape=jax.ShapeDtypeStruct((bsz, dil, seq // dil, 3 * tn), BF16),
        scratch_shapes=scratch,
        compiler_params=_params("parallel", "parallel"),
        name="proj_dilated",
    )(xb, w, cos, sin)


def _rope_tables(seq, half):
    inv = ROPE_THETA ** (-jnp.arange(half, dtype=F32) / half)
    ang = jnp.arange(seq, dtype=jnp.int32).astype(F32)[:, None] * inv[None, :]
    cos, sin = jnp.cos(ang), jnp.sin(ang)
    reps = LANES // (2 * half)
    return (jnp.tile(jnp.concatenate([cos, cos], axis=1), (1, reps)),
            jnp.tile(jnp.concatenate([-sin, sin], axis=1), (1, reps)))


def _banded_body(hq, group, halo, tq, win, length, has_sink, want_lse, *refs):
    refs = list(refs)
    q_ref, k_ref, v_ref = refs[:3]
    pos = 3
    sink_ref = None
    if has_sink:
        sink_ref = refs[pos]
        pos += 1
    o_ref = refs[pos]
    lse_ref = refs[pos + 1] if want_lse else None

    i = pl.program_id(2)
    start = pl.multiple_of(jnp.clip(i * tq - halo, 0, length - win), 64)
    qpos = i * tq + lax.broadcasted_iota(jnp.int32, (tq, win), 0)
    kpos = start + lax.broadcasted_iota(jnp.int32, (tq, win), 1)
    valid = jnp.abs(qpos - kpos) <= halo
    scale = HEAD_DIM ** -0.5
    lane = lax.broadcasted_iota(jnp.int32, (tq, LANES), 1)
    lse_acc = jnp.zeros((tq, LANES), F32)
    for h in range(hq):
        kh = h // group
        q = q_ref[:, h * HEAD_DIM:(h + 1) * HEAD_DIM]
        k = k_ref[pl.ds(start, win), kh * HEAD_DIM:(kh + 1) * HEAD_DIM]
        v = v_ref[pl.ds(start, win), kh * HEAD_DIM:(kh + 1) * HEAD_DIM]
        s = _dot_nt(q, k) * scale
        s = jnp.where(valid, s, NEG_INF)
        m = jnp.max(s, axis=-1, keepdims=True)
        if has_sink:
            sk = sink_ref[h]
            m = jnp.maximum(m, sk)
        p = jnp.exp(s - m)
        d = jnp.sum(p, axis=-1, keepdims=True)
        if has_sink:
            d = d + jnp.exp(sk - m)
        o = jnp.dot(p.astype(BF16), v, preferred_element_type=F32) / d
        o_ref[:, h * HEAD_DIM:(h + 1) * HEAD_DIM] = o.astype(o_ref.dtype)
        if want_lse:
            lse_acc = jnp.where(lane == h, m + jnp.log(d), lse_acc)
    if want_lse:
        lse_ref[...] = lse_acc


def _banded(q4, k4, v4, qcol, kcol, vcol, *, hq, group, halo, sink=None, want_lse=False):
    bsz, dil, length, _ = q4.shape
    hkv = hq // group
    tq = min(256, length)
    win = min(length, tq + 2 * halo)
    qw, kw = hq * HEAD_DIM, hkv * HEAD_DIM
    in_specs = [pl.BlockSpec((None, None, tq, qw), lambda b, j, i: (b, j, i, qcol)),
                pl.BlockSpec((None, None, length, kw), lambda b, j, i: (b, j, 0, kcol)),
                pl.BlockSpec((None, None, length, kw), lambda b, j, i: (b, j, 0, vcol))]
    args = [q4, k4, v4]
    if sink is not None:
        in_specs.append(pl.BlockSpec(memory_space=pltpu.SMEM))
        args.append(sink)
    out_specs = [pl.BlockSpec((None, None, tq, qw), lambda b, j, i: (b, j, i, 0))]
    out_shape = [jax.ShapeDtypeStruct((bsz, dil, length, qw), BF16)]
    if want_lse:
        out_specs.append(pl.BlockSpec((None, None, tq, LANES), lambda b, j, i: (b, j, i, 0)))
        out_shape.append(jax.ShapeDtypeStruct((bsz, dil, length, LANES), F32))
    return pl.pallas_call(
        functools.partial(_banded_body, hq, group, halo, tq, win, length, sink is not None, want_lse),
        grid=(bsz, dil, length // tq),
        in_specs=in_specs,
        out_specs=out_specs,
        out_shape=out_shape,
        compiler_params=_params("parallel", "parallel", "parallel"),
        name="banded_attention",
    )(*args)


def _group_mix_body(dils, *refs):
    n = len(dils)
    o_refs, l_refs, out_ref = refs[:n], refs[n:2 * n], refs[2 * n]
    scratch = list(refs[2 * n + 1:])
    tm = out_ref.shape[0]
    heads, lses = [], []
    for g, dil in enumerate(dils):
        if dil == 1:
            heads.append(functools.partial(
                lambda h, ref: ref[0, :, h * HEAD_DIM:(h + 1) * HEAD_DIM].astype(F32), ref=o_refs[g]))
            lses.append(l_refs[g][0])
            continue
        o_scr, l_scr = scratch.pop(0), scratch.pop(0)
        rows = tm // dil
        for j in range(dil):
            l_scr[pl.ds(j, rows, stride=dil), :] = l_refs[g][j]
            for h in range(A_HEADS_PER_GROUP):
                o_scr.at[h][pl.ds(j, rows, stride=dil), :] = (
                    o_refs[g][j, :, h * HEAD_DIM:(h + 1) * HEAD_DIM].astype(F32))
        heads.append(functools.partial(lambda h, ref: ref[h], ref=o_scr))
        lses.append(l_scr[...])
    m = functools.reduce(jnp.maximum, lses)
    es = [jnp.exp(l - m) for l in lses]
    tot = functools.reduce(lambda a, b: a + b, es)
    ws = [e / tot for e in es]
    for h in range(A_HEADS_PER_GROUP):
        acc = ws[0][:, h:h + 1] * heads[0](h)
        for g in range(1, n):
            acc = acc + ws[g][:, h:h + 1] * heads[g](h)
        out_ref[:, h * HEAD_DIM:(h + 1) * HEAD_DIM] = acc.astype(out_ref.dtype)


def _group_mix(outs, lses, seq):
    bsz = outs[0].shape[0]
    dils = tuple(o.shape[1] for o in outs)
    tm = min(512, seq)
    pos_blocks = seq // tm

    def spec(dil, width):
        return pl.BlockSpec((None, dil, tm // dil, width), lambda m: (m // pos_blocks, 0, m % pos_blocks, 0))

    scratch = []
    for dil in dils:
        if dil > 1:
            scratch += [pltpu.VMEM((A_HEADS_PER_GROUP, tm, HEAD_DIM), F32), pltpu.VMEM((tm, LANES), F32)]
    return pl.pallas_call(
        functools.partial(_group_mix_body, dils),
        grid=(bsz * pos_blocks,),
        in_specs=[spec(dil, A_OUT_W) for dil in dils] + [spec(dil, LANES) for dil in dils],
        out_specs=pl.BlockSpec((tm, A_OUT_W), lambda m: (m, 0)),
        out_shape=jax.ShapeDtypeStruct((bsz * seq, A_OUT_W), BF16),
        scratch_shapes=scratch,
        compiler_params=_params("parallel"),
        name="dilation_group_mix",
    )(*outs, *lses)


def _diff_body(lambda_init, lam_ref, subln_ref, q_ref, k_ref, v_ref, o_ref):
    lam = lam_ref[...]
    dot1 = jnp.sum(lam[0:1, :] * lam[1:2, :], axis=1, keepdims=True)
    dot2 = jnp.sum(lam[2:3, :] * lam[3:4, :], axis=1, keepdims=True)
    lmbda = jnp.exp(dot1) - jnp.exp(dot2) + lambda_init
    k = k_ref[...]
    v = v_ref[...]
    sub = min(DIFF_SUB_ROWS, q_ref.shape[0])
    lane = lax.broadcasted_iota(jnp.int32, (sub, 2 * B_DIM), 1)
    zero = jnp.zeros((sub, 2 * B_DIM), BF16)

    def softmax_map_times_v(qm):
        s = _dot_nt(qm, k)
        p = jnp.exp2(s - jnp.max(s, axis=-1, keepdims=True))
        return jnp.dot(p.astype(BF16), v, preferred_element_type=F32) / jnp.sum(p, axis=-1, keepdims=True)

    for r0 in range(0, q_ref.shape[0], sub):
        q = q_ref[r0:r0 + sub, :]
        o = (softmax_map_times_v(jnp.where(lane < B_DIM, q, zero))
             - lmbda * softmax_map_times_v(jnp.where(lane >= B_DIM, q, zero)))
        o = o * lax.rsqrt(jnp.mean(o * o, axis=-1, keepdims=True) + SUBLN_EPS) * subln_ref[...]
        o_ref[r0:r0 + sub, :] = (o * (1.0 - lambda_init)).astype(o_ref.dtype)


def _diff_attention(qk3, v3, lam, subln, lambda_init):
    bsz, seq, _ = qk3.shape
    tq = min(DIFF_Q_ROWS, seq)
    hw = 2 * B_DIM
    return pl.pallas_call(
        functools.partial(_diff_body, lambda_init),
        grid=(bsz, B_HEADS, seq // tq),
        in_specs=[pl.BlockSpec((4, B_DIM), lambda b, h, i: (0, 0)),
                  pl.BlockSpec((1, hw), lambda b, h, i: (0, 0)),
                  pl.BlockSpec((None, tq, hw), lambda b, h, i: (b, i, h)),
                  pl.BlockSpec((None, seq, hw), lambda b, h, i: (b, 0, B_HEADS + h)),
                  pl.BlockSpec((None, seq, hw), lambda b, h, i: (b, 0, h))],
        out_specs=pl.BlockSpec((None, tq, hw), lambda b, h, i: (b, i, h)),
        out_shape=jax.ShapeDtypeStruct((bsz, seq, B_W), BF16),
        compiler_params=_params("parallel", "parallel", "parallel"),
        name="diff_attention",
    )(lam, subln, qk3, qk3, v3)


def _na_body(rows, k_rows, q_ref, k_ref, v_ref, bias_ref, o_ref):
    n_rb = rows // NA_Q_ROWS
    tq = NA_Q_ROWS * GRID_W
    for u in range(NA_BLOCKS_PER_STEP):
        rb = pl.program_id(2) * NA_BLOCKS_PER_STEP + u
        w0 = jnp.clip(rb * NA_Q_ROWS - NA_WIN_H // 2, 0, rows - k_rows) * GRID_W
        w0 = pl.multiple_of(w0, 256)
        kind = jnp.where(rb == 0, 0, jnp.where(rb == n_rb - 1, 2, 1))
        k = k_ref[pl.ds(w0, k_rows * GRID_W), :]
        v = v_ref[pl.ds(w0, k_rows * GRID_W), :]
        s = _dot_nt(q_ref[u * tq:(u + 1) * tq, :], k) * (HEAD_DIM ** -0.5) + bias_ref[kind]
        p = jnp.exp(s - jnp.max(s, axis=-1, keepdims=True))
        d = jnp.sum(p, axis=-1, keepdims=True)
        o = jnp.dot(p.astype(BF16), v, preferred_element_type=F32) / d
        o_ref[u * tq:(u + 1) * tq, :] = o.astype(o_ref.dtype)


def _na_bias_tables(rpb, rows):
    kh = min(NA_WIN_H, rows)
    k_rows = min(NA_K_ROWS, rows)
    n_rb = rows // NA_Q_ROWS
    n_heads = rpb.shape[0]
    edge = GRID_W - NA_WIN_W
    ext = jnp.pad(rpb.astype(F32), ((0, 0), (0, 0), (edge, edge)), mode="edge")
    toeplitz = jnp.stack([ext[:, :, GRID_W - 1 - qc:2 * GRID_W - 1 - qc] for qc in range(GRID_W)], axis=2)
    c = jnp.arange(GRID_W)
    col_start = jnp.clip(c - NA_WIN_W // 2, 0, GRID_W - NA_WIN_W)
    col_ok = (c[None, :] >= col_start[:, None]) & (c[None, :] < col_start[:, None] + NA_WIN_W)
    toeplitz = jnp.where(col_ok[None, None], toeplitz, NEG_INF)
    tabs = []
    for rb in (0, min(1, n_rb - 1), n_rb - 1):
        r0 = rb * NA_Q_ROWS
        w0 = min(max(r0 - NA_WIN_H // 2, 0), rows - k_rows)
        strips = []
        for a in range(NA_Q_ROWS):
            qrow = r0 + a
            start = min(max(qrow - kh // 2, 0), rows - kh)
            first = start - qrow + (NA_WIN_H - 1)
            blk = toeplitz[:, first:first + kh].transpose(0, 2, 1, 3)
            blk = jnp.pad(blk, ((0, 0), (0, 0), (start - w0, k_rows - kh - (start - w0)), (0, 0)),
                          constant_values=NEG_INF)
            strips.append(blk.reshape(n_heads, GRID_W, k_rows * GRID_W))
        tabs.append(jnp.concatenate(strips, axis=1))
    return jnp.stack(tabs, axis=1)


def _neighborhood_attention(qkv3, rpb):
    bsz, seq, _ = qkv3.shape
    rows = seq // GRID_W
    k_rows = min(NA_K_ROWS, rows)
    tq = NA_Q_ROWS * GRID_W
    step_rows = NA_BLOCKS_PER_STEP * tq
    bias = _na_bias_tables(rpb, rows)
    return pl.pallas_call(
        functools.partial(_na_body, rows, k_rows),
        grid=(bsz, D_HEADS, seq // step_rows),
        in_specs=[pl.BlockSpec((None, step_rows, HEAD_DIM), lambda b, h, r: (b, r, h)),
                  pl.BlockSpec((None, seq, HEAD_DIM), lambda b, h, r: (b, 0, D_HEADS + h)),
                  pl.BlockSpec((None, seq, HEAD_DIM), lambda b, h, r: (b, 0, 2 * D_HEADS + h)),
                  pl.BlockSpec((None, 3, tq, k_rows * GRID_W), lambda b, h, r: (h, 0, 0, 0))],
        out_specs=pl.BlockSpec((None, step_rows, HEAD_DIM), lambda b, h, r: (b, r, h)),
        out_shape=jax.ShapeDtypeStruct((bsz, seq, D_W), BF16),
        compiler_params=_params("parallel", "parallel", "parallel"),
        name="neighborhood_attention",
    )(qkv3, qkv3, qkv3, bias)


def _merge_body(oa_ref, ob_ref, oc_ref, od_ref, wa_ref, wb_ref, wc_ref, wd_ref,
                ga_ref, gb_ref, gc_ref, gd_ref, out_ref):
    acc = ga_ref[...].astype(F32) * jnp.dot(oa_ref[...], wa_ref[...], preferred_element_type=F32)
    acc = acc + gb_ref[...].astype(F32) * jnp.dot(ob_ref[...], wb_ref[...], preferred_element_type=F32)
    acc = acc + gc_ref[...].astype(F32) * jnp.dot(oc_ref[...], wc_ref[...], preferred_element_type=F32)
    acc = acc + gd_ref[...].astype(F32) * jnp.dot(od_ref[...], wd_ref[...], preferred_element_type=F32)
    out_ref[...] = acc.astype(out_ref.dtype)


def _merge(branches, weights, gates, seq):
    m_tot = branches[0].shape[0]
    d = weights[0].shape[1]
    tm = min(1024, seq)
    tn = min(512, d)
    n_blocks = d // tn
    in_specs = [pl.BlockSpec((tm, o.shape[1]), lambda m, n: (m, 0)) for o in branches]
    in_specs += [pl.BlockSpec((w.shape[0], tn), lambda m, n: (0, n)) for w in weights]
    in_specs += [pl.BlockSpec((tm, tn), functools.partial(lambda m, n, i: (m, i * n_blocks + n), i=i))
                 for i in range(N_BRANCHES)]
    return pl.pallas_call(
        _merge_body,
        grid=(m_tot // tm, n_blocks),
        in_specs=in_specs,
        out_specs=pl.BlockSpec((tm, tn), lambda m, n: (m, n)),
        out_shape=jax.ShapeDtypeStruct((m_tot, d), BF16),
        compiler_params=_params("parallel", "parallel"),
        name="branch_merge",
    )(*branches, *weights, *([gates] * N_BRANCHES))


def _layer_norm_rows(y, gain, bias):
    mu = jnp.mean(y, axis=-1, keepdims=True)
    var = jnp.mean(jnp.square(y - mu), axis=-1, keepdims=True)
    return (y - mu) * lax.rsqrt(var + LN_EPS) * gain + bias


def _out_proj_body(alpha, mixed_ref, w_ref, x_ref, gain_ref, bias_ref, xo_ref, xb_ref):
    y = alpha * x_ref[...] + jnp.dot(mixed_ref[...], w_ref[...], preferred_element_type=F32)
    out = _layer_norm_rows(y, gain_ref[...], bias_ref[...])
    xo_ref[...] = out
    xb_ref[...] = out.astype(BF16)


def _out_proj(mixed, w, x, gain, bias, alpha, seq):
    m_tot, d = x.shape
    tm = min(256, seq)
    row = pl.BlockSpec((tm, d), lambda m: (m, 0))
    vec = pl.BlockSpec((1, d), lambda m: (0, 0))
    return pl.pallas_call(
        functools.partial(_out_proj_body, alpha),
        grid=(m_tot // tm,),
        in_specs=[row, pl.BlockSpec((d, d), lambda m: (0, 0)), row, vec, vec],
        out_specs=[row, row],
        out_shape=[jax.ShapeDtypeStruct((m_tot, d), F32), jax.ShapeDtypeStruct((m_tot, d), BF16)],
        compiler_params=_params("parallel"),
        name="out_proj_layernorm",
    )(mixed, w, x, gain, bias)


def _router_body(cap, x_ref, wrt_ref, slot_t_ref, slot_ref, gate_ref):
    seq = x_ref.shape[0]
    logits = _dot_nt(wrt_ref[...], x_ref[...])
    e = jnp.exp(logits - jnp.max(logits, axis=0, keepdims=True))
    aff = e / jnp.sum(e, axis=0, keepdims=True)
    bits = lax.bitcast_convert_type(aff, jnp.int32)
    thr = jnp.zeros((N_EXPERTS, 1), jnp.int32)
    for bit in range(30, -1, -1):
        cand = thr | (1 << bit)
        cnt = jnp.sum((bits >= cand).astype(F32), axis=1, keepdims=True)
        thr = jnp.where(cnt >= cap, cand, thr)
    above = bits > thr
    tied = bits == thr
    need = cap - jnp.sum(above.astype(F32), axis=1, keepdims=True)

    chunk = min(512, seq)
    upper = (lax.broadcasted_iota(jnp.int32, (chunk, chunk), 0)
             <= lax.broadcasted_iota(jnp.int32, (chunk, chunk), 1)).astype(BF16)

    def prefix_count(mask):
        parts, carry = [], jnp.zeros((N_EXPERTS, 1), F32)
        for c in range(seq // chunk):
            part = jnp.dot(mask[:, c * chunk:(c + 1) * chunk].astype(BF16), upper,
                           preferred_element_type=F32) + carry
            parts.append(part)
            carry = part[:, chunk - 1:chunk]
        return jnp.concatenate(parts, axis=1)

    sel = above | (tied & (prefix_count(tied) <= need))
    slot = jnp.where(sel, prefix_count(sel) - 1.0, -1.0)
    gate = jnp.where(sel, aff, 0.0)
    slot_t_ref[...] = slot.astype(jnp.int32)
    pad = EXPERT_PAD - N_EXPERTS
    slot_ref[...] = jnp.concatenate([slot, jnp.full((pad, seq), -1.0, F32)], axis=0).T
    gate_ref[...] = jnp.concatenate([gate, jnp.zeros((pad, seq), F32)], axis=0).T


def _router(xb3, wrt, cap):
    bsz, seq, d = xb3.shape
    return pl.pallas_call(
        functools.partial(_router_body, cap),
        grid=(bsz,),
        in_specs=[pl.BlockSpec((None, seq, d), lambda b: (b, 0, 0)),
                  pl.BlockSpec((N_EXPERTS, d), lambda b: (0, 0))],
        out_specs=[pl.BlockSpec((None, N_EXPERTS, seq), lambda b: (b, 0, 0)),
                   pl.BlockSpec((None, seq, EXPERT_PAD), lambda b: (b, 0, 0)),
                   pl.BlockSpec((None, seq, EXPERT_PAD), lambda b: (b, 0, 0))],
        out_shape=[jax.ShapeDtypeStruct((bsz, N_EXPERTS, seq), jnp.int32),
                   jax.ShapeDtypeStruct((bsz, seq, EXPERT_PAD), F32),
                   jax.ShapeDtypeStruct((bsz, seq, EXPERT_PAD), F32)],
        compiler_params=_params("parallel"),
        name="router_topk",
    )(xb3, wrt)


def _gather_body(cap, x_ref, slot_t_ref, o_ref):
    e = pl.program_id(1)
    seq = x_ref.shape[0]
    row = slot_t_ref[pl.ds(e, 1), :]
    onehot = (lax.broadcasted_iota(jnp.int32, (cap, seq), 0) == row).astype(BF16)
    o_ref[...] = jnp.dot(onehot, x_ref[...], preferred_element_type=F32).astype(o_ref.dtype)


def _gather_tokens(xb3, slot_t, cap):
    bsz, seq, d = xb3.shape
    return pl.pallas_call(
        functools.partial(_gather_body, cap),
        grid=(bsz, N_EXPERTS),
        in_specs=[pl.BlockSpec((None, seq, d), lambda b, e: (b, 0, 0)),
                  pl.BlockSpec((None, N_EXPERTS, seq), lambda b, e: (b, 0, 0))],
        out_specs=pl.BlockSpec((None, cap, d), lambda b, e: (e, b, 0)),
        out_shape=jax.ShapeDtypeStruct((N_EXPERTS, bsz * cap, d), BF16),
        compiler_params=_params("parallel", "parallel"),
        name="expert_gather",
    )(xb3, slot_t)


def _expert_body(x_ref, wg_ref, wu_ref, wd_ref, o_ref):
    x = x_ref[...]
    g = jnp.dot(x, wg_ref[...], preferred_element_type=F32)
    u = jnp.dot(x, wu_ref[...], preferred_element_type=F32)
    h = (g / (1.0 + jnp.exp(-g))) * u
    o_ref[...] = jnp.dot(h.astype(BF16), wd_ref[...], preferred_element_type=F32).astype(o_ref.dtype)


def _expert_ffn(xin, wg, wu, wd):
    n_exp, rows, d = xin.shape
    f = wg.shape[2]
    tm = min(512, rows)
    return pl.pallas_call(
        _expert_body,
        grid=(n_exp, rows // tm),
        in_specs=[pl.BlockSpec((None, tm, d), lambda e, m: (e, m, 0)),
                  pl.BlockSpec((None, d, f), lambda e, m: (e, 0, 0)),
                  pl.BlockSpec((None, d, f), lambda e, m: (e, 0, 0)),
                  pl.BlockSpec((None, f, d), lambda e, m: (e, 0, 0))],
        out_specs=pl.BlockSpec((None, tm, d), lambda e, m: (e, m, 0)),
        out_shape=jax.ShapeDtypeStruct((n_exp, rows, d), BF16),
        compiler_params=_params("parallel", "parallel"),
        name="expert_swiglu",
    )(xin, wg, wu, wd)


def _combine_body(alpha, cap, y_ref, slot_ref, gate_ref, x_ref, gain_ref, bias_ref, xo_ref, xb_ref, acc_ref):
    e = pl.program_id(2)

    @pl.when(e == 0)
    def _():
        acc_ref[...] = jnp.zeros_like(acc_ref)

    tt = slot_ref.shape[0]
    lane = lax.broadcasted_iota(jnp.int32, (tt, EXPERT_PAD), 1)
    pick = lane == e
    slot_col = jnp.sum(jnp.where(pick, slot_ref[...], 0.0), axis=1, keepdims=True)
    gate_col = jnp.sum(jnp.where(pick, gate_ref[...], 0.0), axis=1, keepdims=True)
    onehot = (lax.broadcasted_iota(jnp.int32, (tt, cap), 1).astype(F32) == slot_col).astype(BF16)
    acc_ref[...] += gate_col * jnp.dot(onehot, y_ref[...], preferred_element_type=F32)

    @pl.when(e == N_EXPERTS - 1)
    def _():
        out = _layer_norm_rows(alpha * x_ref[...] + acc_ref[...], gain_ref[...], bias_ref[...])
        xo_ref[...] = out
        xb_ref[...] = out.astype(BF16)


def _combine(y, slot, gate, x3, gain, bias, alpha, cap):
    bsz, seq, d = x3.shape
    tt = min(512, seq)
    tok = pl.BlockSpec((None, tt, d), lambda b, t, e: (b, t, 0))
    sel = pl.BlockSpec((None, tt, EXPERT_PAD), lambda b, t, e: (b, t, 0))
    vec = pl.BlockSpec((1, d), lambda b, t, e: (0, 0))
    return pl.pallas_call(
        functools.partial(_combine_body, alpha, cap),
        grid=(bsz, seq // tt, N_EXPERTS),
        in_specs=[pl.BlockSpec((None, cap, d), lambda b, t, e: (e, b, 0)), sel, sel, tok, vec, vec],
        out_specs=[tok, tok],
        out_shape=[jax.ShapeDtypeStruct((bsz, seq, d), F32), jax.ShapeDtypeStruct((bsz, seq, d), BF16)],
        scratch_shapes=[pltpu.VMEM((tt, d), F32)],
        compiler_params=_params("parallel", "parallel", "arbitrary"),
        name="expert_combine_layernorm",
    )(y, slot, gate, x3, gain, bias)


def _offsets(widths):
    offs = [0]
    for w in widths:
        offs.append(offs[-1] + w)
    return offs


def kernel(x, w_in, b_gate, w_branch, w_out, diff_lambda, diff_subln, sink_logit, na_rpb,
           w_router, w_exp_gate, w_exp_up, w_exp_down, ln_gain, ln_bias):
    bsz, seq, d = x.shape
    depth = w_in.shape[0]
    m_tot = bsz * seq
    alpha = (2.0 * depth) ** 0.25
    cap = EC_CAPACITY_FACTOR * seq // N_EXPERTS
    in_widths = (A_W, A_W, A_W, B_W, B_W, B_W, C_Q_W, C_KV_W, C_KV_W, D_W, D_W, D_W, N_BRANCHES * d)
    io = _offsets(in_widths)
    bo = _offsets((A_OUT_W, B_W, C_Q_W, D_W))
    cos128, sin128 = _rope_tables(seq, HEAD_DIM // 2)
    cos64, sin64 = _rope_tables(seq, B_DIM // 2)

    xf = x.reshape(m_tot, d)
    xb = xf.astype(BF16)
    for l in range(depth):
        lambda_init = 0.8 - 0.6 * math.exp(-0.3 * l)
        wl = w_in[l]

        def cols(a, b):
            return wl[:, io[a]:io[b]].astype(BF16)

        qk_b = _proj(xb, cols(3, 5), "rope64", B_W, seq, (cos64, sin64))
        v_b = _proj(xb, cols(5, 6), "plain", B_W, seq)
        qk_c = _proj(xb, cols(6, 8), "rope128", C_Q_W + C_KV_W, seq, (cos128, sin128))
        v_c = _proj(xb, cols(8, 9), "plain", C_KV_W, seq)
        qkv_d = _proj(xb, cols(9, 12), "plain", D_W, seq)
        gates = _proj(xb, cols(12, 13), "gate", min(1024, d), seq, (b_gate[l].reshape(1, -1),))

        outs, lses = [], []
        for g, (w, r) in enumerate(DIL_PAIRS):
            w_g = jnp.concatenate([wl[:, io[t] + g * A_OUT_W:io[t] + (g + 1) * A_OUT_W] for t in range(3)],
                                  axis=1).astype(BF16)
            qkv_g = _proj_dilated(xb, w_g, r, bsz, seq, cos128, sin128)
            o_g, lse_g = _banded(qkv_g, qkv_g, qkv_g, 0, 1, 2, hq=A_HEADS_PER_GROUP, group=1,
                                 halo=(w // 2) // r, want_lse=True)
            outs.append(o_g)
            lses.append(lse_g)
        o_a = _group_mix(outs, lses, seq)

        o_b = _diff_attention(qk_b.reshape(bsz, seq, 2 * B_W), v_b.reshape(bsz, seq, B_W),
                              diff_lambda[l], diff_subln[l].reshape(1, -1), lambda_init)

        qk_c4 = qk_c.reshape(bsz, 1, seq, C_Q_W + C_KV_W)
        (o_c,) = _banded(qk_c4, qk_c4, v_c.reshape(bsz, 1, seq, C_KV_W), 0, C_Q_W // C_KV_W, 0,
                         hq=C_Q_HEADS, group=C_Q_HEADS // C_KV_HEADS, halo=C_HALF_WINDOW, sink=sink_logit[l])

        o_d = _neighborhood_attention(qkv_d.reshape(bsz, seq, 3 * D_W), na_rpb[l])

        wb = w_branch[l].astype(BF16)
        mixed = _merge([o_a, o_b.reshape(m_tot, B_W), o_c.reshape(m_tot, C_Q_W), o_d.reshape(m_tot, D_W)],
                       [wb[bo[i]:bo[i + 1]] for i in range(N_BRANCHES)], gates, seq)
        xf, xb = _out_proj(mixed, w_out[l].astype(BF16), xf, ln_gain[l, 0].reshape(1, -1),
                           ln_bias[l, 0].reshape(1, -1), alpha, seq)

        xb3 = xb.reshape(bsz, seq, d)
        slot_t, slot, gate = _router(xb3, w_router[l].T.astype(BF16), cap)
        xin = _gather_tokens(xb3, slot_t, cap)
        y = _expert_ffn(xin, w_exp_gate[l].astype(BF16), w_exp_up[l].astype(BF16), w_exp_down[l].astype(BF16))
        xf3, xb3 = _combine(y, slot, gate, xf.reshape(bsz, seq, d), ln_gain[l, 1].reshape(1, -1),
                            ln_bias[l, 1].reshape(1, -1), alpha, cap)
        xf, xb = xf3.reshape(m_tot, d), xb3.reshape(m_tot, d)
    return xf.reshape(bsz, seq, d)
```

```python
import functools
import math

import jax
import jax.numpy as jnp
from jax import lax
from jax.experimental import pallas as pl
from jax.experimental.pallas import tpu as pltpu

F32 = jnp.float32
BF16 = jnp.bfloat16

LANES = 128
HEAD_DIM = 128
ROPE_THETA = 10000.0
NEG_INF = -1e30
LN_EPS = 1e-5
SUBLN_EPS = 1e-5
DIL_PAIRS = ((128, 1), (512, 4), (2048, 16))
A_HEADS_PER_GROUP = 6
A_GROUPS = len(DIL_PAIRS)
A_HEADS = A_GROUPS * A_HEADS_PER_GROUP
B_HEADS = 8
B_DIM = 64
C_Q_HEADS = 8
C_KV_HEADS = 2
C_HALF_WINDOW = 128
D_HEADS = 8
GRID_W = 64
NA_WIN_H = 8
NA_WIN_W = 16
NA_Q_ROWS = 8
NA_K_ROWS = 16
NA_BLOCKS_PER_STEP = 2
DIFF_Q_ROWS = 512
DIFF_SUB_ROWS = 256
DIFF_Q_SCALE = (B_DIM ** -0.5) * math.log2(math.e)
N_BRANCHES = 4
N_EXPERTS = 16
EC_CAPACITY_FACTOR = 2
EXPERT_PAD = 128

A_W = A_HEADS * HEAD_DIM
B_W = B_HEADS * 2 * B_DIM
C_Q_W = C_Q_HEADS * HEAD_DIM
C_KV_W = C_KV_HEADS * HEAD_DIM
D_W = D_HEADS * HEAD_DIM
A_OUT_W = A_HEADS_PER_GROUP * HEAD_DIM

VMEM_LIMIT = 56 * 1024 * 1024


def _params(*sem):
    return pltpu.CompilerParams(dimension_semantics=sem, vmem_limit_bytes=VMEM_LIMIT)


def _dot_nt(a, b):
    return lax.dot_general(a, b, (((1,), (1,)), ((), ())), preferred_element_type=F32)


def _proj_body(mode, x_ref, w_ref, *rest):
    acc = jnp.dot(x_ref[...], w_ref[...], preferred_element_type=F32)
    if mode == "plain":
        (o_ref,) = rest
        o_ref[...] = acc.astype(o_ref.dtype)
    elif mode == "gate":
        b_ref, o_ref = rest
        z = acc + b_ref[...]
        o_ref[...] = (1.0 / (1.0 + jnp.exp(-z))).astype(o_ref.dtype)
    else:
        cos_ref, sin_ref, o_ref = rest
        cos = cos_ref[...]
        sin = sin_ref[...]
        if mode == "rope64":
            qs = jnp.where(pl.program_id(0) == 0, DIFF_Q_SCALE, 1.0).astype(F32)
            cos = cos * qs
            sin = sin * qs
        lane = lax.broadcasted_iota(jnp.int32, cos.shape, 1)
        for c in range(acc.shape[1] // LANES):
            a = acc[:, c * LANES:(c + 1) * LANES]
            if mode == "rope128":
                rot = pltpu.roll(a, 64, 1)
            else:
                rot = jnp.where((lane % 64) < 32, pltpu.roll(a, 96, 1), pltpu.roll(a, 32, 1))
            o_ref[:, c * LANES:(c + 1) * LANES] = (a * cos + rot * sin).astype(o_ref.dtype)


def _proj(xb, w, mode, tn, seq, extra=()):
    m_tot, k = xb.shape
    n_tot = w.shape[1]
    tm = min(1024, seq)
    pos_blocks = seq // tm
    in_specs = [pl.BlockSpec((tm, k), lambda n, m: (m, 0)),
                pl.BlockSpec((k, tn), lambda n, m: (0, n))]
    if mode == "gate":
        in_specs.append(pl.BlockSpec((1, tn), lambda n, m: (0, n)))
    elif mode != "plain":
        in_specs += [pl.BlockSpec((tm, LANES), lambda n, m: (m % pos_blocks, 0))] * 2
    return pl.pallas_call(
        functools.partial(_proj_body, mode),
        grid=(n_tot // tn, m_tot // tm),
        in_specs=in_specs,
        out_specs=pl.BlockSpec((tm, tn), lambda n, m: (m, n)),
        out_shape=jax.ShapeDtypeStruct((m_tot, n_tot), BF16),
        compiler_params=_params("parallel", "parallel"),
        name="proj_" + mode,
    )(xb, w, *extra)


def _proj_dilated_body(dil, x_ref, w_ref, cos_ref, sin_ref, o_ref, *scratch):
    n = pl.program_id(0)
    tm = x_ref.shape[0]
    rows = tm // dil
    acc = jnp.dot(x_ref[...], w_ref[...], preferred_element_type=F32)
    dst = scratch[0] if dil > 1 else None

    def emit(c, val):
        if dil > 1:
            dst[c] = val
        else:
            o_ref[0, :, c * LANES:(c + 1) * LANES] = val.astype(o_ref.dtype)

    @pl.when(n < 2)
    def _():
        cos = cos_ref[...]
        sin = sin_ref[...]
        for c in range(acc.shape[1] // LANES):
            a = acc[:, c * LANES:(c + 1) * LANES]
            emit(c, a * cos + pltpu.roll(a, 64, 1) * sin)

    @pl.when(n >= 2)
    def _():
        for c in range(acc.shape[1] // LANES):
            emit(c, acc[:, c * LANES:(c + 1) * LANES])

    if dil > 1:
        for c in range(acc.shape[1] // LANES):
            for j in range(dil):
                o_ref[j, :, c * LANES:(c + 1) * LANES] = (
                    dst.at[c][pl.ds(j, rows, stride=dil), :].astype(o_ref.dtype))


def _proj_dilated(xb, w, dil, bsz, seq, cos, sin):
    m_tot, k = xb.shape
    tm = min(1024, seq)
    tn = A_OUT_W
    pos_blocks = seq // tm
    scratch = [pltpu.VMEM((tn // LANES, tm, LANES), F32)] if dil > 1 else []
    return pl.pallas_call(
        functools.partial(_proj_dilated_body, dil),
        grid=(3, m_tot // tm),
        in_specs=[pl.BlockSpec((tm, k), lambda n, m: (m, 0)),
                  pl.BlockSpec((k, tn), lambda n, m: (0, n)),
                  pl.BlockSpec((tm, LANES), lambda n, m: (m % pos_blocks, 0)),
                  pl.BlockSpec((tm, LANES), lambda n, m: (m % pos_blocks, 0))],
        out_specs=pl.BlockSpec((None, dil, tm // dil, tn), lambda n, m: (m // pos_blocks, 0, m % pos_blocks, n)),
        out_shape=jax.ShapeDtypeStruct((bsz, dil, seq // dil, 3 * tn), BF16),
        scratch_shapes=scratch,
        compiler_params=_params("parallel", "parallel"),
        name="proj_dilated",
    )(xb, w, cos, sin)


def _rope_tables(seq, half):
    inv = ROPE_THETA ** (-jnp.arange(half, dtype=F32) / half)
    ang = jnp.arange(seq, dtype=jnp.int32).astype(F32)[:, None] * inv[None, :]
    cos, sin = jnp.cos(ang), jnp.sin(ang)
    reps = LANES // (2 * half)
    return (jnp.tile(jnp.concatenate([cos, cos], axis=1), (1, reps)),
            jnp.tile(jnp.concatenate([-sin, sin], axis=1), (1, reps)))


def _banded_body(hq, group, halo, tq, win, length, has_sink, want_lse, *refs):
    refs = list(refs)
    q_ref, k_ref, v_ref = refs[:3]
    pos = 3
    sink_ref = None
    if has_sink:
        sink_ref = refs[pos]
        pos += 1
    o_ref = refs[pos]
    lse_ref = refs[pos + 1] if want_lse else None

    i = pl.program_id(2)
    start = pl.multiple_of(jnp.clip(i * tq - halo, 0, length - win), 64)
    qpos = i * tq + lax.broadcasted_iota(jnp.int32, (tq, win), 0)
    kpos = start + lax.broadcasted_iota(jnp.int32, (tq, win), 1)
    valid = jnp.abs(qpos - kpos) <= halo
    scale = HEAD_DIM ** -0.5
    lane = lax.broadcasted_iota(jnp.int32, (tq, LANES), 1)
    lse_acc = jnp.zeros((tq, LANES), F32)
    for h in range(hq):
        kh = h // group
        q = q_ref[:, h * HEAD_DIM:(h + 1) * HEAD_DIM]
        k = k_ref[pl.ds(start, win), kh * HEAD_DIM:(kh + 1) * HEAD_DIM]
        v = v_ref[pl.ds(start, win), kh * HEAD_DIM:(kh + 1) * HEAD_DIM]
        s = _dot_nt(q, k) * scale
        s = jnp.where(valid, s, NEG_INF)
        m = jnp.max(s, axis=-1, keepdims=True)
        if has_sink:
            sk = sink_ref[h]
            m = jnp.maximum(m, sk)
        p = jnp.exp(s - m)
        d = jnp.sum(p, axis=-1, keepdims=True)
        if has_sink:
            d = d + jnp.exp(sk - m)
        o = jnp.dot(p.astype(BF16), v, preferred_element_type=F32) / d
        o_ref[:, h * HEAD_DIM:(h + 1) * HEAD_DIM] = o.astype(o_ref.dtype)
        if want_lse:
            lse_acc = jnp.where(lane == h, m + jnp.log(d), lse_acc)
    if want_lse:
        lse_ref[...] = lse_acc


def _banded(q4, k4, v4, qcol, kcol, vcol, *, hq, group, halo, sink=None, want_lse=False):
    bsz, dil, length, _ = q4.shape
    hkv = hq // group
    tq = min(256, length)
    win = min(length, tq + 2 * halo)
    qw, kw = hq * HEAD_DIM, hkv * HEAD_DIM
    in_specs = [pl.BlockSpec((None, None, tq, qw), lambda b, j, i: (b, j, i, qcol)),
                pl.BlockSpec((None, None, length, kw), lambda b, j, i: (b, j, 0, kcol)),
                pl.BlockSpec((None, None, length, kw), lambda b, j, i: (b, j, 0, vcol))]
    args = [q4, k4, v4]
    if sink is not None:
        in_specs.append(pl.BlockSpec(memory_space=pltpu.SMEM))
        args.append(sink)
    out_specs = [pl.BlockSpec((None, None, tq, qw), lambda b, j, i: (b, j, i, 0))]
    out_shape = [jax.ShapeDtypeStruct((bsz, dil, length, qw), BF16)]
    if want_lse:
        out_specs.append(pl.BlockSpec((None, None, tq, LANES), lambda b, j, i: (b, j, i, 0)))
        out_shape.append(jax.ShapeDtypeStruct((bsz, dil, length, LANES), F32))
    return pl.pallas_call(
        functools.partial(_banded_body, hq, group, halo, tq, win, length, sink is not None, want_lse),
        grid=(bsz, dil, length // tq),
        in_specs=in_specs,
        out_specs=out_specs,
        out_shape=out_shape,
        compiler_params=_params("parallel", "parallel", "parallel"),
        name="banded_attention",
    )(*args)


def _group_mix_body(dils, *refs):
    n = len(dils)
    o_refs, l_refs, out_ref = refs[:n], refs[n:2 * n], refs[2 * n]
    scratch = list(refs[2 * n + 1:])
    tm = out_ref.shape[0]
    heads, lses = [], []
    for g, dil in enumerate(dils):
        if dil == 1:
            heads.append(functools.partial(
                lambda h, ref: ref[0, :, h * HEAD_DIM:(h + 1) * HEAD_DIM].astype(F32), ref=o_refs[g]))
            lses.append(l_refs[g][0])
            continue
        o_scr, l_scr = scratch.pop(0), scratch.pop(0)
        rows = tm // dil
        for j in range(dil):
            l_scr[pl.ds(j, rows, stride=dil), :] = l_refs[g][j]
            for h in range(A_HEADS_PER_GROUP):
                o_scr.at[h][pl.ds(j, rows, stride=dil), :] = (
                    o_refs[g][j, :, h * HEAD_DIM:(h + 1) * HEAD_DIM].astype(F32))
        heads.append(functools.partial(lambda h, ref: ref[h], ref=o_scr))
        lses.append(l_scr[...])
    m = functools.reduce(jnp.maximum, lses)
    es = [jnp.exp(l - m) for l in lses]
    tot = functools.reduce(lambda a, b: a + b, es)
    ws = [e / tot for e in es]
    for h in range(A_HEADS_PER_GROUP):
        acc = ws[0][:, h:h + 1] * heads[0](h)
        for g in range(1, n):
            acc = acc + ws[g][:, h:h + 1] * heads[g](h)
        out_ref[:, h * HEAD_DIM:(h + 1) * HEAD_DIM] = acc.astype(out_ref.dtype)


def _group_mix(outs, lses, seq):
    bsz = outs[0].shape[0]
    dils = tuple(o.shape[1] for o in outs)
    tm = min(512, seq)
    pos_blocks = seq // tm

    def spec(dil, width):
        return pl.BlockSpec((None, dil, tm // dil, width), lambda m: (m // pos_blocks, 0, m % pos_blocks, 0))

    scratch = []
    for dil in dils:
        if dil > 1:
            scratch += [pltpu.VMEM((A_HEADS_PER_GROUP, tm, HEAD_DIM), F32), pltpu.VMEM((tm, LANES), F32)]
    return pl.pallas_call(
        functools.partial(_group_mix_body, dils),
        grid=(bsz * pos_blocks,),
        in_specs=[spec(dil, A_OUT_W) for dil in dils] + [spec(dil, LANES) for dil in dils],
        out_specs=pl.BlockSpec((tm, A_OUT_W), lambda m: (m, 0)),
        out_shape=jax.ShapeDtypeStruct((bsz * seq, A_OUT_W), BF16),
        scratch_shapes=scratch,
        compiler_params=_params("parallel"),
        name="dilation_group_mix",
    )(*outs, *lses)


def _diff_body(lambda_init, lam_ref, subln_ref, q_ref, k_ref, v_ref, o_ref):
    lam = lam_ref[...]
    dot1 = jnp.sum(lam[0:1, :] * lam[1:2, :], axis=1, keepdims=True)
    dot2 = jnp.sum(lam[2:3, :] * lam[3:4, :], axis=1, keepdims=True)
    lmbda = jnp.exp(dot1) - jnp.exp(dot2) + lambda_init
    k = k_ref[...]
    v = v_ref[...]
    sub = min(DIFF_SUB_ROWS, q_ref.shape[0])
    lane = lax.broadcasted_iota(jnp.int32, (sub, 2 * B_DIM), 1)
    zero = jnp.zeros((sub, 2 * B_DIM), BF16)

    def softmax_map_times_v(qm):
        s = _dot_nt(qm, k)
        p = jnp.exp2(s - jnp.max(s, axis=-1, keepdims=True))
        return jnp.dot(p.astype(BF16), v, preferred_element_type=F32) / jnp.sum(p, axis=-1, keepdims=True)

    for r0 in range(0, q_ref.shape[0], sub):
        q = q_ref[r0:r0 + sub, :]
        o = (softmax_map_times_v(jnp.where(lane < B_DIM, q, zero))
             - lmbda * softmax_map_times_v(jnp.where(lane >= B_DIM, q, zero)))
        o = o * lax.rsqrt(jnp.mean(o * o, axis=-1, keepdims=True) + SUBLN_EPS) * subln_ref[...]
        o_ref[r0:r0 + sub, :] = (o * (1.0 - lambda_init)).astype(o_ref.dtype)


def _diff_attention(qk3, v3, lam, subln, lambda_init):
    bsz, seq, _ = qk3.shape
    tq = min(DIFF_Q_ROWS, seq)
    hw = 2 * B_DIM
    return pl.pallas_call(
        functools.partial(_diff_body, lambda_init),
        grid=(bsz, B_HEADS, seq // tq),
        in_specs=[pl.BlockSpec((4, B_DIM), lambda b, h, i: (0, 0)),
                  pl.BlockSpec((1, hw), lambda b, h, i: (0, 0)),
                  pl.BlockSpec((None, tq, hw), lambda b, h, i: (b, i, h)),
                  pl.BlockSpec((None, seq, hw), lambda b, h, i: (b, 0, B_HEADS + h)),
                  pl.BlockSpec((None, seq, hw), lambda b, h, i: (b, 0, h))],
        out_specs=pl.BlockSpec((None, tq, hw), lambda b, h, i: (b, i, h)),
        out_shape=jax.ShapeDtypeStruct((bsz, seq, B_W), BF16),
        compiler_params=_params("parallel", "parallel", "parallel"),
        name="diff_attention",
    )(lam, subln, qk3, qk3, v3)


def _na_body(rows, k_rows, q_ref, k_ref, v_ref, bias_ref, o_ref):
    n_rb = rows // NA_Q_ROWS
    tq = NA_Q_ROWS * GRID_W
    for u in range(NA_BLOCKS_PER_STEP):
        rb = pl.program_id(2) * NA_BLOCKS_PER_STEP + u
        w0 = jnp.clip(rb * NA_Q_ROWS - NA_WIN_H // 2, 0, rows - k_rows) * GRID_W
        w0 = pl.multiple_of(w0, 256)
        kind = jnp.where(rb == 0, 0, jnp.where(rb == n_rb - 1, 2, 1))
        k = k_ref[pl.ds(w0, k_rows * GRID_W), :]
        v = v_ref[pl.ds(w0, k_rows * GRID_W), :]
        s = _dot_nt(q_ref[u * tq:(u + 1) * tq, :], k) * (HEAD_DIM ** -0.5) + bias_ref[kind]
        p = jnp.exp(s - jnp.max(s, axis=-1, keepdims=True))
        d = jnp.sum(p, axis=-1, keepdims=True)
        o = jnp.dot(p.astype(BF16), v, preferred_element_type=F32) / d
        o_ref[u * tq:(u + 1) * tq, :] = o.astype(o_ref.dtype)


def _na_bias_tables(rpb, rows):
    kh = min(NA_WIN_H, rows)
    k_rows = min(NA_K_ROWS, rows)
    n_rb = rows // NA_Q_ROWS
    n_heads = rpb.shape[0]
    edge = GRID_W - NA_WIN_W
    ext = jnp.pad(rpb.astype(F32), ((0, 0), (0, 0), (edge, edge)), mode="edge")
    toeplitz = jnp.stack([ext[:, :, GRID_W - 1 - qc:2 * GRID_W - 1 - qc] for qc in range(GRID_W)], axis=2)
    c = jnp.arange(GRID_W)
    col_start = jnp.clip(c - NA_WIN_W // 2, 0, GRID_W - NA_WIN_W)
    col_ok = (c[None, :] >= col_start[:, None]) & (c[None, :] < col_start[:, None] + NA_WIN_W)
    toeplitz = jnp.where(col_ok[None, None], toeplitz, NEG_INF)
    tabs = []
    for rb in (0, min(1, n_rb - 1), n_rb - 1):
        r0 = rb * NA_Q_ROWS
        w0 = min(max(r0 - NA_WIN_H // 2, 0), rows - k_rows)
        strips = []
        for a in range(NA_Q_ROWS):
            qrow = r0 + a
            start = min(max(qrow - kh // 2, 0), rows - kh)
            first = start - qrow + (NA_WIN_H - 1)
            blk = toeplitz[:, first:first + kh].transpose(0, 2, 1, 3)
            blk = jnp.pad(blk, ((0, 0), (0, 0), (start - w0, k_rows - kh - (start - w0)), (0, 0)),
                          constant_values=NEG_INF)
            strips.append(blk.reshape(n_heads, GRID_W, k_rows * GRID_W))
        tabs.append(jnp.concatenate(strips, axis=1))
    return jnp.stack(tabs, axis=1)


def _neighborhood_attention(qkv3, rpb):
    bsz, seq, _ = qkv3.shape
    rows = seq // GRID_W
    k_rows = min(NA_K_ROWS, rows)
    tq = NA_Q_ROWS * GRID_W
    step_rows = NA_BLOCKS_PER_STEP * tq
    bias = _na_bias_tables(rpb, rows)
    return pl.pallas_call(
        functools.partial(_na_body, rows, k_rows),
        grid=(bsz, D_HEADS, seq // step_rows),
        in_specs=[pl.BlockSpec((None, step_rows, HEAD_DIM), lambda b, h, r: (b, r, h)),
                  pl.BlockSpec((None, seq, HEAD_DIM), lambda b, h, r: (b, 0, D_HEADS + h)),
                  pl.BlockSpec((None, seq, HEAD_DIM), lambda b, h, r: (b, 0, 2 * D_HEADS + h)),
                  pl.BlockSpec((None, 3, tq, k_rows * GRID_W), lambda b, h, r: (h, 0, 0, 0))],
        out_specs=pl.BlockSpec((None, step_rows, HEAD_DIM), lambda b, h, r: (b, r, h)),
        out_shape=jax.ShapeDtypeStruct((bsz, seq, D_W), BF16),
        compiler_params=_params("parallel", "parallel", "parallel"),
        name="neighborhood_attention",
    )(qkv3, qkv3, qkv3, bias)


def _merge_body(oa_ref, ob_ref, oc_ref, od_ref, wa_ref, wb_ref, wc_ref, wd_ref,
                ga_ref, gb_ref, gc_ref, gd_ref, out_ref):
    acc = ga_ref[...].astype(F32) * jnp.dot(oa_ref[...], wa_ref[...], preferred_element_type=F32)
    acc = acc + gb_ref[...].astype(F32) * jnp.dot(ob_ref[...], wb_ref[...], preferred_element_type=F32)
    acc = acc + gc_ref[...].astype(F32) * jnp.dot(oc_ref[...], wc_ref[...], preferred_element_type=F32)
    acc = acc + gd_ref[...].astype(F32) * jnp.dot(od_ref[...], wd_ref[...], preferred_element_type=F32)
    out_ref[...] = acc.astype(out_ref.dtype)


def _merge(branches, weights, gates, seq):
    m_tot = branches[0].shape[0]
    d = weights[0].shape[1]
    tm = min(1024, seq)
    tn = min(512, d)
    n_blocks = d // tn
    in_specs = [pl.BlockSpec((tm, o.shape[1]), lambda m, n: (m, 0)) for o in branches]
    in_specs += [pl.BlockSpec((w.shape[0], tn), lambda m, n: (0, n)) for w in weights]
    in_specs += [pl.BlockSpec((tm, tn), functools.partial(lambda m, n, i: (m, i * n_blocks + n), i=i))
                 for i in range(N_BRANCHES)]
    return pl.pallas_call(
        _merge_body,
        grid=(m_tot // tm, n_blocks),
        in_specs=in_specs,
        out_specs=pl.BlockSpec((tm, tn), lambda m, n: (m, n)),
        out_shape=jax.ShapeDtypeStruct((m_tot, d), BF16),
        compiler_params=_params("parallel", "parallel"),
        name="branch_merge",
    )(*branches, *weights, *([gates] * N_BRANCHES))


def _layer_norm_rows(y, gain, bias):
    mu = jnp.mean(y, axis=-1, keepdims=True)
    var = jnp.mean(jnp.square(y - mu), axis=-1, keepdims=True)
    return (y - mu) * lax.rsqrt(var + LN_EPS) * gain + bias


def _out_proj_body(alpha, mixed_ref, w_ref, x_ref, gain_ref, bias_ref, xo_ref, xb_ref, acc_ref):
    y = alpha * x_ref[...] + jnp.dot(mixed_ref[...], w_ref[...], preferred_element_type=F32)
    out = _layer_norm_rows(y, gain_ref[...], bias_ref[...])
    xo_ref[...] = out
    xb_ref[...] = out.astype(BF16)
    acc_ref[...] = alpha * out


def _out_proj(mixed, w, x, gain, bias, alpha, seq):
    m_tot, d = x.shape
    tm = min(256, seq)
    row = pl.BlockSpec((tm, d), lambda m: (m, 0))
    vec = pl.BlockSpec((1, d), lambda m: (0, 0))
    return pl.pallas_call(
        functools.partial(_out_proj_body, alpha),
        grid=(m_tot // tm,),
        in_specs=[row, pl.BlockSpec((d, d), lambda m: (0, 0)), row, vec, vec],
        out_specs=[row, row, row],
        out_shape=[jax.ShapeDtypeStruct((m_tot, d), F32), jax.ShapeDtypeStruct((m_tot, d), BF16),
                   jax.ShapeDtypeStruct((m_tot, d), F32)],
        compiler_params=_params("parallel"),
        name="out_proj_layernorm",
    )(mixed, w, x, gain, bias)


def _router_body(cap, x_ref, wrt_ref, tok_ref, idx_ref, gate_ref):
    seq = x_ref.shape[0]
    logits = _dot_nt(wrt_ref[...], x_ref[...])
    e = jnp.exp(logits - jnp.max(logits, axis=0, keepdims=True))
    aff = e / jnp.sum(e, axis=0, keepdims=True)
    bits = lax.bitcast_convert_type(aff, jnp.int32)
    thr = jnp.zeros((N_EXPERTS, 1), jnp.int32)
    for bit in range(30, -1, -1):
        cand = thr | (1 << bit)
        cnt = jnp.sum((bits >= cand).astype(F32), axis=1, keepdims=True)
        thr = jnp.where(cnt >= cap, cand, thr)
    above = bits > thr
    tied = bits == thr
    need = cap - jnp.sum(above.astype(F32), axis=1, keepdims=True)

    chunk = min(512, seq)
    upper = (lax.broadcasted_iota(jnp.int32, (chunk, chunk), 0)
             <= lax.broadcasted_iota(jnp.int32, (chunk, chunk), 1)).astype(BF16)

    def prefix_count(mask):
        parts, carry = [], jnp.zeros((N_EXPERTS, 1), F32)
        for c in range(seq // chunk):
            part = jnp.dot(mask[:, c * chunk:(c + 1) * chunk].astype(BF16), upper,
                           preferred_element_type=F32) + carry
            parts.append(part)
            carry = part[:, chunk - 1:chunk]
        return jnp.concatenate(parts, axis=1)

    sel = above | (tied & (prefix_count(tied) <= need))
    slot = jnp.where(sel, prefix_count(sel) - 1.0, -1.0)
    gate = jnp.where(sel, aff, 0.0)
    pad = EXPERT_PAD - N_EXPERTS
    gate_ref[...] = jnp.concatenate([gate, jnp.zeros((pad, seq), F32)], axis=0).T

    slot_i = slot.astype(jnp.int32)
    ck = min(1024, seq)
    slot_iota = lax.broadcasted_iota(jnp.int32, (cap, ck), 0)
    lane = lax.broadcasted_iota(jnp.int32, (cap, EXPERT_PAD), 1)
    idx_cols = jnp.zeros((cap, EXPERT_PAD), F32)
    for e in range(N_EXPERTS):
        r = jnp.zeros((cap, LANES), F32)
        for c in range(seq // ck):
            onehot = (slot_iota == slot_i[e:e + 1, c * ck:(c + 1) * ck]).astype(BF16)
            r = r + jnp.dot(onehot, tok_ref[c * ck:(c + 1) * ck, :], preferred_element_type=F32)
        idx_cols = jnp.where(lane == e, r[:, 0:1] * 64.0 + r[:, 1:2], idx_cols)
    idx_ref[...] = idx_cols.T[0:N_EXPERTS, :].astype(jnp.int32) + pl.program_id(0) * seq


def _router(xb3, wrt, cap):
    bsz, seq, d = xb3.shape
    tok = jnp.arange(seq, dtype=jnp.int32)
    tok_tab = jnp.zeros((seq, LANES), F32).at[:, 0].set((tok // 64).astype(F32)).at[:, 1].set((tok % 64).astype(F32))
    idx, gate = pl.pallas_call(
        functools.partial(_router_body, cap),
        grid=(bsz,),
        in_specs=[pl.BlockSpec((None, seq, d), lambda b: (b, 0, 0)),
                  pl.BlockSpec((N_EXPERTS, d), lambda b: (0, 0)),
                  pl.BlockSpec((seq, LANES), lambda b: (0, 0))],
        out_specs=[pl.BlockSpec((None, N_EXPERTS, cap), lambda b: (b, 0, 0)),
                   pl.BlockSpec((None, seq, EXPERT_PAD), lambda b: (b, 0, 0))],
        out_shape=[jax.ShapeDtypeStruct((bsz, N_EXPERTS, cap), jnp.int32),
                   jax.ShapeDtypeStruct((bsz, seq, EXPERT_PAD), F32)],
        compiler_params=_params("parallel"),
        name="router_topk",
    )(xb3, wrt, tok_tab.astype(BF16))
    return idx, gate.reshape(bsz * seq, EXPERT_PAD)


def _row_copy(src, src_row, dst, dst_row, sem):
    return pltpu.make_async_copy(src.at[pl.ds(src_row, 1), :], dst.at[pl.ds(dst_row, 1), :], sem)


def _expert_body(cap, idx_ref, x_hbm, gate_hbm, acc_in_hbm, wg_ref, wu_ref, wd_ref, acc_hbm,
                 x_rows, acc_rows, gate_rows, sems):
    del acc_in_hbm
    e = pl.program_id(0)
    unroll = 8

    def for_each_slot(fn):
        def body(p, carry):
            fn(p)
            return carry
        lax.fori_loop(0, cap, body, 0, unroll=unroll)

    def start_gathers(p):
        row = idx_ref[0, p]
        _row_copy(x_hbm, row, x_rows, p, sems.at[0]).start()
        _row_copy(gate_hbm, row, gate_rows, p, sems.at[1]).start()
        _row_copy(acc_hbm, row, acc_rows, p, sems.at[2]).start()

    def wait_gathers(p):
        row = idx_ref[0, p]
        _row_copy(x_hbm, row, x_rows, p, sems.at[0]).wait()
        _row_copy(gate_hbm, row, gate_rows, p, sems.at[1]).wait()
        _row_copy(acc_hbm, row, acc_rows, p, sems.at[2]).wait()

    for_each_slot(start_gathers)
    for_each_slot(wait_gathers)

    x = x_rows[...].astype(BF16)
    g = jnp.dot(x, wg_ref[...], preferred_element_type=F32)
    u = jnp.dot(x, wu_ref[...], preferred_element_type=F32)
    h = (g / (1.0 + jnp.exp(-g))) * u
    y = jnp.dot(h.astype(BF16), wd_ref[...], preferred_element_type=F32)
    lane = lax.broadcasted_iota(jnp.int32, gate_rows.shape, 1)
    gate_col = jnp.sum(jnp.where(lane == e, gate_rows[...], 0.0), axis=1, keepdims=True)
    acc_rows[...] = acc_rows[...] + gate_col * y

    for_each_slot(lambda p: _row_copy(acc_rows, p, acc_hbm, idx_ref[0, p], sems.at[3]).start())
    for_each_slot(lambda p: _row_copy(acc_rows, p, acc_hbm, idx_ref[0, p], sems.at[3]).wait())


def _expert_ffn(idx, xf, gate, acc, wg, wu, wd):
    bsz, n_exp, cap = idx.shape
    m_tot, d = xf.shape
    f = wg.shape[2]
    any_spec = pl.BlockSpec(memory_space=pl.ANY)
    return pl.pallas_call(
        functools.partial(_expert_body, cap),
        grid=(n_exp, bsz),
        in_specs=[pl.BlockSpec((None, 1, cap), lambda e, b: (b * n_exp + e, 0, 0), memory_space=pltpu.SMEM),
                  any_spec, any_spec, any_spec,
                  pl.BlockSpec((None, d, f), lambda e, b: (e, 0, 0)),
                  pl.BlockSpec((None, d, f), lambda e, b: (e, 0, 0)),
                  pl.BlockSpec((None, f, d), lambda e, b: (e, 0, 0))],
        out_specs=any_spec,
        out_shape=jax.ShapeDtypeStruct((m_tot, d), F32),
        scratch_shapes=[pltpu.VMEM((cap, d), F32), pltpu.VMEM((cap, d), F32),
                        pltpu.VMEM((cap, EXPERT_PAD), F32), pltpu.SemaphoreType.DMA((4,))],
        input_output_aliases={3: 0},
        compiler_params=_params("arbitrary", "arbitrary"),
        name="expert_swiglu_scatter",
    )(idx.reshape(bsz * n_exp, 1, cap), xf, gate, acc, wg, wu, wd)


def _final_norm_body(acc_ref, gain_ref, bias_ref, xo_ref, xb_ref):
    out = _layer_norm_rows(acc_ref[...], gain_ref[...], bias_ref[...])
    xo_ref[...] = out
    xb_ref[...] = out.astype(BF16)


def _final_norm(acc, gain, bias, seq):
    m_tot, d = acc.shape
    tm = min(512, seq)
    row = pl.BlockSpec((tm, d), lambda m: (m, 0))
    vec = pl.BlockSpec((1, d), lambda m: (0, 0))
    return pl.pallas_call(
        _final_norm_body,
        grid=(m_tot // tm,),
        in_specs=[row, vec, vec],
        out_specs=[row, row],
        out_shape=[jax.ShapeDtypeStruct((m_tot, d), F32), jax.ShapeDtypeStruct((m_tot, d), BF16)],
        compiler_params=_params("parallel"),
        name="channel_mixer_layernorm",
    )(acc, gain, bias)


def _offsets(widths):
    offs = [0]
    for w in widths:
        offs.append(offs[-1] + w)
    return offs


def kernel(x, w_in, b_gate, w_branch, w_out, diff_lambda, diff_subln, sink_logit, na_rpb,
           w_router, w_exp_gate, w_exp_up, w_exp_down, ln_gain, ln_bias):
    bsz, seq, d = x.shape
    depth = w_in.shape[0]
    m_tot = bsz * seq
    alpha = (2.0 * depth) ** 0.25
    cap = EC_CAPACITY_FACTOR * seq // N_EXPERTS
    in_widths = (A_W, A_W, A_W, B_W, B_W, B_W, C_Q_W, C_KV_W, C_KV_W, D_W, D_W, D_W, N_BRANCHES * d)
    io = _offsets(in_widths)
    bo = _offsets((A_OUT_W, B_W, C_Q_W, D_W))
    cos128, sin128 = _rope_tables(seq, HEAD_DIM // 2)
    cos64, sin64 = _rope_tables(seq, B_DIM // 2)

    xf = x.reshape(m_tot, d)
    xb = xf.astype(BF16)
    for l in range(depth):
        lambda_init = 0.8 - 0.6 * math.exp(-0.3 * l)
        wl = w_in[l]

        def cols(a, b):
            return wl[:, io[a]:io[b]].astype(BF16)

        qk_b = _proj(xb, cols(3, 5), "rope64", B_W, seq, (cos64, sin64))
        v_b = _proj(xb, cols(5, 6), "plain", B_W, seq)
        qk_c = _proj(xb, cols(6, 8), "rope128", C_Q_W + C_KV_W, seq, (cos128, sin128))
        v_c = _proj(xb, cols(8, 9), "plain", C_KV_W, seq)
        qkv_d = _proj(xb, cols(9, 12), "plain", D_W, seq)
        gates = _proj(xb, cols(12, 13), "gate", min(1024, d), seq, (b_gate[l].reshape(1, -1),))

        outs, lses = [], []
        for g, (w, r) in enumerate(DIL_PAIRS):
            w_g = jnp.concatenate([wl[:, io[t] + g * A_OUT_W:io[t] + (g + 1) * A_OUT_W] for t in range(3)],
                                  axis=1).astype(BF16)
            qkv_g = _proj_dilated(xb, w_g, r, bsz, seq, cos128, sin128)
            o_g, lse_g = _banded(qkv_g, qkv_g, qkv_g, 0, 1, 2, hq=A_HEADS_PER_GROUP, group=1,
                                 halo=(w // 2) // r, want_lse=True)
            outs.append(o_g)
            lses.append(lse_g)
        o_a = _group_mix(outs, lses, seq)

        o_b = _diff_attention(qk_b.reshape(bsz, seq, 2 * B_W), v_b.reshape(bsz, seq, B_W),
                              diff_lambda[l], diff_subln[l].reshape(1, -1), lambda_init)

        qk_c4 = qk_c.reshape(bsz, 1, seq, C_Q_W + C_KV_W)
        (o_c,) = _banded(qk_c4, qk_c4, v_c.reshape(bsz, 1, seq, C_KV_W), 0, C_Q_W // C_KV_W, 0,
                         hq=C_Q_HEADS, group=C_Q_HEADS // C_KV_HEADS, halo=C_HALF_WINDOW, sink=sink_logit[l])

        o_d = _neighborhood_attention(qkv_d.reshape(bsz, seq, 3 * D_W), na_rpb[l])

        wb = w_branch[l].astype(BF16)
        mixed = _merge([o_a, o_b.reshape(m_tot, B_W), o_c.reshape(m_tot, C_Q_W), o_d.reshape(m_tot, D_W)],
                       [wb[bo[i]:bo[i + 1]] for i in range(N_BRANCHES)], gates, seq)
        xf, xb, acc = _out_proj(mixed, w_out[l].astype(BF16), xf, ln_gain[l, 0].reshape(1, -1),
                                ln_bias[l, 0].reshape(1, -1), alpha, seq)

        idx, gate = _router(xb.reshape(bsz, seq, d), w_router[l].T.astype(BF16), cap)
        acc = _expert_ffn(idx, xf, gate, acc, w_exp_gate[l].astype(BF16), w_exp_up[l].astype(BF16),
                          w_exp_down[l].astype(BF16))
        xf, xb = _final_norm(acc, ln_gain[l, 1].reshape(1, -1), ln_bias[l, 1].reshape(1, -1), seq)
    return xf.reshape(bsz, seq, d)
```

```python
import functools
import math

import jax
import jax.numpy as jnp
from jax import lax
from jax.experimental import pallas as pl
from jax.experimental.pallas import tpu as pltpu

F32 = jnp.float32
BF16 = jnp.bfloat16

LANES = 128
HEAD_DIM = 128
ROPE_THETA = 10000.0
NEG_INF = -1e30
LN_EPS = 1e-5
SUBLN_EPS = 1e-5
DIL_PAIRS = ((128, 1), (512, 4), (2048, 16))
A_HEADS_PER_GROUP = 6
A_GROUPS = len(DIL_PAIRS)
A_HEADS = A_GROUPS * A_HEADS_PER_GROUP
B_HEADS = 8
B_DIM = 64
C_Q_HEADS = 8
C_KV_HEADS = 2
C_HALF_WINDOW = 128
D_HEADS = 8
GRID_W = 64
NA_WIN_H = 8
NA_WIN_W = 16
NA_Q_ROWS = 8
NA_K_ROWS = 16
NA_BLOCKS_PER_STEP = 2
DIFF_Q_ROWS = 512
DIFF_SUB_ROWS = 256
DIFF_Q_SCALE = (B_DIM ** -0.5) * math.log2(math.e)
N_BRANCHES = 4
N_EXPERTS = 16
EC_CAPACITY_FACTOR = 2
EXPERT_PAD = 128
EXPERT_SLOT_CHUNKS = 2

A_W = A_HEADS * HEAD_DIM
B_W = B_HEADS * 2 * B_DIM
C_Q_W = C_Q_HEADS * HEAD_DIM
C_KV_W = C_KV_HEADS * HEAD_DIM
D_W = D_HEADS * HEAD_DIM
A_OUT_W = A_HEADS_PER_GROUP * HEAD_DIM

VMEM_LIMIT = 56 * 1024 * 1024


def _params(*sem):
    return pltpu.CompilerParams(dimension_semantics=sem, vmem_limit_bytes=VMEM_LIMIT)


def _dot_nt(a, b):
    return lax.dot_general(a, b, (((1,), (1,)), ((), ())), preferred_element_type=F32)


def _cast_weight_tile(w_ref, wb_ref):
    @pl.when(pl.program_id(1) == 0)
    def _():
        wb_ref[...] = w_ref[...].astype(BF16)
    return wb_ref[...]


def _proj_body(mode, x_ref, w_ref, *rest):
    rest, wb_ref = rest[:-1], rest[-1]
    acc = jnp.dot(x_ref[...], _cast_weight_tile(w_ref, wb_ref), preferred_element_type=F32)
    if mode == "plain":
        (o_ref,) = rest
        o_ref[...] = acc.astype(o_ref.dtype)
    elif mode == "gate":
        b_ref, o_ref = rest
        z = acc + b_ref[...]
        o_ref[...] = (1.0 / (1.0 + jnp.exp(-z))).astype(o_ref.dtype)
    else:
        cos_ref, sin_ref, o_ref = rest
        cos = cos_ref[...]
        sin = sin_ref[...]
        if mode == "rope64":
            qs = jnp.where(pl.program_id(0) == 0, DIFF_Q_SCALE, 1.0).astype(F32)
            cos = cos * qs
            sin = sin * qs
        lane = lax.broadcasted_iota(jnp.int32, cos.shape, 1)
        for c in range(acc.shape[1] // LANES):
            a = acc[:, c * LANES:(c + 1) * LANES]
            if mode == "rope128":
                rot = pltpu.roll(a, 64, 1)
            else:
                rot = jnp.where((lane % 64) < 32, pltpu.roll(a, 96, 1), pltpu.roll(a, 32, 1))
            o_ref[:, c * LANES:(c + 1) * LANES] = (a * cos + rot * sin).astype(o_ref.dtype)


def _weight_tile_spec(k, tn, layer, lane_tile_of):
    return pl.BlockSpec((pl.Squeezed(), pl.Element(k), pl.Element(tn)),
                        lambda n, m: (layer, 0, lane_tile_of(n) * LANES))


def _proj(xb, w_stack, layer, col0, n_tot, mode, tn, seq, extra=()):
    m_tot, k = xb.shape
    tm = min(1024, seq)
    pos_blocks = seq // tm
    in_specs = [pl.BlockSpec((tm, k), lambda n, m: (m, 0)),
                _weight_tile_spec(k, tn, layer, lambda n: col0 // LANES + n * (tn // LANES))]
    if mode == "gate":
        in_specs.append(pl.BlockSpec((1, tn), lambda n, m: (0, n)))
    elif mode != "plain":
        in_specs += [pl.BlockSpec((tm, LANES), lambda n, m: (m % pos_blocks, 0))] * 2
    return pl.pallas_call(
        functools.partial(_proj_body, mode),
        grid=(n_tot // tn, m_tot // tm),
        in_specs=in_specs,
        out_specs=pl.BlockSpec((tm, tn), lambda n, m: (m, n)),
        out_shape=jax.ShapeDtypeStruct((m_tot, n_tot), BF16),
        scratch_shapes=[pltpu.VMEM((k, tn), BF16)],
        compiler_params=_params("parallel", "arbitrary"),
        name="proj_" + mode,
    )(xb, w_stack, *extra)


def _proj_dilated_body(dil, x_ref, w_ref, cos_ref, sin_ref, o_ref, wb_ref, *scratch):
    n = pl.program_id(0)
    tm = x_ref.shape[0]
    rows = tm // dil
    acc = jnp.dot(x_ref[...], _cast_weight_tile(w_ref, wb_ref), preferred_element_type=F32)
    dst = scratch[0] if dil > 1 else None

    def emit(c, val):
        if dil > 1:
            dst[c] = val
        else:
            o_ref[0, :, c * LANES:(c + 1) * LANES] = val.astype(o_ref.dtype)

    @pl.when(n < 2)
    def _():
        cos = cos_ref[...]
        sin = sin_ref[...]
        for c in range(acc.shape[1] // LANES):
            a = acc[:, c * LANES:(c + 1) * LANES]
            emit(c, a * cos + pltpu.roll(a, 64, 1) * sin)

    @pl.when(n >= 2)
    def _():
        for c in range(acc.shape[1] // LANES):
            emit(c, acc[:, c * LANES:(c + 1) * LANES])

    if dil > 1:
        for c in range(acc.shape[1] // LANES):
            for j in range(dil):
                o_ref[j, :, c * LANES:(c + 1) * LANES] = (
                    dst.at[c][pl.ds(j, rows, stride=dil), :].astype(o_ref.dtype))


def _proj_dilated(xb, w_stack, layer, group, dil, bsz, seq, cos, sin):
    m_tot, k = xb.shape
    tm = min(1024, seq)
    tn = A_OUT_W
    pos_blocks = seq // tm
    scratch = [pltpu.VMEM((k, tn), BF16)]
    if dil > 1:
        scratch.append(pltpu.VMEM((tn // LANES, tm, LANES), F32))
    return pl.pallas_call(
        functools.partial(_proj_dilated_body, dil),
        grid=(3, m_tot // tm),
        in_specs=[pl.BlockSpec((tm, k), lambda n, m: (m, 0)),
                  _weight_tile_spec(k, tn, layer, lambda n: n * (A_W // LANES) + group * (tn // LANES)),
                  pl.BlockSpec((tm, LANES), lambda n, m: (m % pos_blocks, 0)),
                  pl.BlockSpec((tm, LANES), lambda n, m: (m % pos_blocks, 0))],
        out_specs=pl.BlockSpec((None, dil, tm // dil, tn), lambda n, m: (m // pos_blocks, 0, m % pos_blocks, n)),
        out_shape=jax.ShapeDtypeStruct((bsz, dil, seq // dil, 3 * tn), BF16),
        scratch_shapes=scratch,
        compiler_params=_params("parallel", "arbitrary"),
        name="proj_dilated",
    )(xb, w_stack, cos, sin)


def _rope_tables(seq, half):
    inv = ROPE_THETA ** (-jnp.arange(half, dtype=F32) / half)
    ang = jnp.arange(seq, dtype=jnp.int32).astype(F32)[:, None] * inv[None, :]
    cos, sin = jnp.cos(ang), jnp.sin(ang)
    reps = LANES // (2 * half)
    return (jnp.tile(jnp.concatenate([cos, cos], axis=1), (1, reps)),
            jnp.tile(jnp.concatenate([-sin, sin], axis=1), (1, reps)))


def _banded_body(hq, group, halo, tq, win, length, has_sink, want_lse, *refs):
    refs = list(refs)
    q_ref, k_ref, v_ref = refs[:3]
    pos = 3
    sink_ref = None
    if has_sink:
        sink_ref = refs[pos]
        pos += 1
    o_ref = refs[pos]
    lse_ref = refs[pos + 1] if want_lse else None

    i = pl.program_id(2)
    start = pl.multiple_of(jnp.clip(i * tq - halo, 0, length - win), 64)
    qpos = i * tq + lax.broadcasted_iota(jnp.int32, (tq, win), 0)
    kpos = start + lax.broadcasted_iota(jnp.int32, (tq, win), 1)
    valid = jnp.abs(qpos - kpos) <= halo
    scale = HEAD_DIM ** -0.5
    lane = lax.broadcasted_iota(jnp.int32, (tq, LANES), 1)
    lse_acc = jnp.zeros((tq, LANES), F32)
    for h in range(hq):
        kh = h // group
        q = q_ref[:, h * HEAD_DIM:(h + 1) * HEAD_DIM]
        k = k_ref[pl.ds(start, win), kh * HEAD_DIM:(kh + 1) * HEAD_DIM]
        v = v_ref[pl.ds(start, win), kh * HEAD_DIM:(kh + 1) * HEAD_DIM]
        s = _dot_nt(q, k) * scale
        s = jnp.where(valid, s, NEG_INF)
        m = jnp.max(s, axis=-1, keepdims=True)
        if has_sink:
            sk = sink_ref[h]
            m = jnp.maximum(m, sk)
        p = jnp.exp(s - m)
        d = jnp.sum(p, axis=-1, keepdims=True)
        if has_sink:
            d = d + jnp.exp(sk - m)
        o = jnp.dot(p.astype(BF16), v, preferred_element_type=F32) / d
        o_ref[:, h * HEAD_DIM:(h + 1) * HEAD_DIM] = o.astype(o_ref.dtype)
        if want_lse:
            lse_acc = jnp.where(lane == h, m + jnp.log(d), lse_acc)
    if want_lse:
        lse_ref[...] = lse_acc


def _banded(q4, k4, v4, qcol, kcol, vcol, *, hq, group, halo, sink=None, want_lse=False):
    bsz, dil, length, _ = q4.shape
    hkv = hq // group
    tq = min(256, length)
    win = min(length, tq + 2 * halo)
    qw, kw = hq * HEAD_DIM, hkv * HEAD_DIM
    in_specs = [pl.BlockSpec((None, None, tq, qw), lambda b, j, i: (b, j, i, qcol)),
                pl.BlockSpec((None, None, length, kw), lambda b, j, i: (b, j, 0, kcol)),
                pl.BlockSpec((None, None, length, kw), lambda b, j, i: (b, j, 0, vcol))]
    args = [q4, k4, v4]
    if sink is not None:
        in_specs.append(pl.BlockSpec(memory_space=pltpu.SMEM))
        args.append(sink)
    out_specs = [pl.BlockSpec((None, None, tq, qw), lambda b, j, i: (b, j, i, 0))]
    out_shape = [jax.ShapeDtypeStruct((bsz, dil, length, qw), BF16)]
    if want_lse:
        out_specs.append(pl.BlockSpec((None, None, tq, LANES), lambda b, j, i: (b, j, i, 0)))
        out_shape.append(jax.ShapeDtypeStruct((bsz, dil, length, LANES), F32))
    return pl.pallas_call(
        functools.partial(_banded_body, hq, group, halo, tq, win, length, sink is not None, want_lse),
        grid=(bsz, dil, length // tq),
        in_specs=in_specs,
        out_specs=out_specs,
        out_shape=out_shape,
        compiler_params=_params("parallel", "parallel", "parallel"),
        name="banded_attention",
    )(*args)


def _group_mix_body(dils, *refs):
    n = len(dils)
    o_refs, l_refs, out_ref = refs[:n], refs[n:2 * n], refs[2 * n]
    scratch = list(refs[2 * n + 1:])
    tm = out_ref.shape[0]
    heads, lses = [], []
    for g, dil in enumerate(dils):
        if dil == 1:
            heads.append(functools.partial(
                lambda h, ref: ref[0, :, h * HEAD_DIM:(h + 1) * HEAD_DIM].astype(F32), ref=o_refs[g]))
            lses.append(l_refs[g][0])
            continue
        o_scr, l_scr = scratch.pop(0), scratch.pop(0)
        rows = tm // dil
        for j in range(dil):
            l_scr[pl.ds(j, rows, stride=dil), :] = l_refs[g][j]
            for h in range(A_HEADS_PER_GROUP):
                o_scr.at[h][pl.ds(j, rows, stride=dil), :] = (
                    o_refs[g][j, :, h * HEAD_DIM:(h + 1) * HEAD_DIM].astype(F32))
        heads.append(functools.partial(lambda h, ref: ref[h], ref=o_scr))
        lses.append(l_scr[...])
    m = functools.reduce(jnp.maximum, lses)
    es = [jnp.exp(l - m) for l in lses]
    tot = functools.reduce(lambda a, b: a + b, es)
    ws = [e / tot for e in es]
    for h in range(A_HEADS_PER_GROUP):
        acc = ws[0][:, h:h + 1] * heads[0](h)
        for g in range(1, n):
            acc = acc + ws[g][:, h:h + 1] * heads[g](h)
        out_ref[:, h * HEAD_DIM:(h + 1) * HEAD_DIM] = acc.astype(out_ref.dtype)


def _group_mix(outs, lses, seq):
    bsz = outs[0].shape[0]
    dils = tuple(o.shape[1] for o in outs)
    tm = min(512, seq)
    pos_blocks = seq // tm

    def spec(dil, width):
        return pl.BlockSpec((None, dil, tm // dil, width), lambda m: (m // pos_blocks, 0, m % pos_blocks, 0))

    scratch = []
    for dil in dils:
        if dil > 1:
            scratch += [pltpu.VMEM((A_HEADS_PER_GROUP, tm, HEAD_DIM), F32), pltpu.VMEM((tm, LANES), F32)]
    return pl.pallas_call(
        functools.partial(_group_mix_body, dils),
        grid=(bsz * pos_blocks,),
        in_specs=[spec(dil, A_OUT_W) for dil in dils] + [spec(dil, LANES) for dil in dils],
        out_specs=pl.BlockSpec((tm, A_OUT_W), lambda m: (m, 0)),
        out_shape=jax.ShapeDtypeStruct((bsz * seq, A_OUT_W), BF16),
        scratch_shapes=scratch,
        compiler_params=_params("parallel"),
        name="dilation_group_mix",
    )(*outs, *lses)


def _diff_body(lambda_init, lam_ref, subln_ref, q_ref, k_ref, v_ref, o_ref):
    lam = lam_ref[...]
    dot1 = jnp.sum(lam[0:1, :] * lam[1:2, :], axis=1, keepdims=True)
    dot2 = jnp.sum(lam[2:3, :] * lam[3:4, :], axis=1, keepdims=True)
    lmbda = jnp.exp(dot1) - jnp.exp(dot2) + lambda_init
    k = k_ref[...]
    v = v_ref[...]
    sub = min(DIFF_SUB_ROWS, q_ref.shape[0])
    lane = lax.broadcasted_iota(jnp.int32, (sub, 2 * B_DIM), 1)
    zero = jnp.zeros((sub, 2 * B_DIM), BF16)

    def softmax_map_times_v(qm):
        s = _dot_nt(qm, k)
        p = jnp.exp2(s - jnp.max(s, axis=-1, keepdims=True))
        return jnp.dot(p.astype(BF16), v, preferred_element_type=F32) / jnp.sum(p, axis=-1, keepdims=True)

    for r0 in range(0, q_ref.shape[0], sub):
        q = q_ref[r0:r0 + sub, :]
        o = (softmax_map_times_v(jnp.where(lane < B_DIM, q, zero))
             - lmbda * softmax_map_times_v(jnp.where(lane >= B_DIM, q, zero)))
        o = o * lax.rsqrt(jnp.mean(o * o, axis=-1, keepdims=True) + SUBLN_EPS) * subln_ref[...]
        o_ref[r0:r0 + sub, :] = (o * (1.0 - lambda_init)).astype(o_ref.dtype)


def _diff_attention(qk3, v3, lam, subln, lambda_init):
    bsz, seq, _ = qk3.shape
    tq = min(DIFF_Q_ROWS, seq)
    hw = 2 * B_DIM
    return pl.pallas_call(
        functools.partial(_diff_body, lambda_init),
        grid=(bsz, B_HEADS, seq // tq),
        in_specs=[pl.BlockSpec((4, B_DIM), lambda b, h, i: (0, 0)),
                  pl.BlockSpec((1, hw), lambda b, h, i: (0, 0)),
                  pl.BlockSpec((None, tq, hw), lambda b, h, i: (b, i, h)),
                  pl.BlockSpec((None, seq, hw), lambda b, h, i: (b, 0, B_HEADS + h)),
                  pl.BlockSpec((None, seq, hw), lambda b, h, i: (b, 0, h))],
        out_specs=pl.BlockSpec((None, tq, hw), lambda b, h, i: (b, i, h)),
        out_shape=jax.ShapeDtypeStruct((bsz, seq, B_W), BF16),
        compiler_params=_params("parallel", "parallel", "parallel"),
        name="diff_attention",
    )(lam, subln, qk3, qk3, v3)


def _na_body(rows, k_rows, q_ref, k_ref, v_ref, bias_ref, o_ref):
    n_rb = rows // NA_Q_ROWS
    tq = NA_Q_ROWS * GRID_W
    for u in range(NA_BLOCKS_PER_STEP):
        rb = pl.program_id(2) * NA_BLOCKS_PER_STEP + u
        w0 = jnp.clip(rb * NA_Q_ROWS - NA_WIN_H // 2, 0, rows - k_rows) * GRID_W
        w0 = pl.multiple_of(w0, 256)
        kind = jnp.where(rb == 0, 0, jnp.where(rb == n_rb - 1, 2, 1))
        k = k_ref[pl.ds(w0, k_rows * GRID_W), :]
        v = v_ref[pl.ds(w0, k_rows * GRID_W), :]
        s = _dot_nt(q_ref[u * tq:(u + 1) * tq, :], k) * (HEAD_DIM ** -0.5) + bias_ref[kind]
        p = jnp.exp(s - jnp.max(s, axis=-1, keepdims=True))
        d = jnp.sum(p, axis=-1, keepdims=True)
        o = jnp.dot(p.astype(BF16), v, preferred_element_type=F32) / d
        o_ref[u * tq:(u + 1) * tq, :] = o.astype(o_ref.dtype)


def _na_bias_tables(rpb, rows):
    kh = min(NA_WIN_H, rows)
    k_rows = min(NA_K_ROWS, rows)
    n_rb = rows // NA_Q_ROWS
    n_heads = rpb.shape[0]
    edge = GRID_W - NA_WIN_W
    ext = jnp.pad(rpb.astype(F32), ((0, 0), (0, 0), (edge, edge)), mode="edge")
    toeplitz = jnp.stack([ext[:, :, GRID_W - 1 - qc:2 * GRID_W - 1 - qc] for qc in range(GRID_W)], axis=2)
    c = jnp.arange(GRID_W)
    col_start = jnp.clip(c - NA_WIN_W // 2, 0, GRID_W - NA_WIN_W)
    col_ok = (c[None, :] >= col_start[:, None]) & (c[None, :] < col_start[:, None] + NA_WIN_W)
    toeplitz = jnp.where(col_ok[None, None], toeplitz, NEG_INF)
    tabs = []
    for rb in (0, min(1, n_rb - 1), n_rb - 1):
        r0 = rb * NA_Q_ROWS
        w0 = min(max(r0 - NA_WIN_H // 2, 0), rows - k_rows)
        strips = []
        for a in range(NA_Q_ROWS):
            qrow = r0 + a
            start = min(max(qrow - kh // 2, 0), rows - kh)
            first = start - qrow + (NA_WIN_H - 1)
            blk = toeplitz[:, first:first + kh].transpose(0, 2, 1, 3)
            blk = jnp.pad(blk, ((0, 0), (0, 0), (start - w0, k_rows - kh - (start - w0)), (0, 0)),
                          constant_values=NEG_INF)
            strips.append(blk.reshape(n_heads, GRID_W, k_rows * GRID_W))
        tabs.append(jnp.concatenate(strips, axis=1))
    return jnp.stack(tabs, axis=1)


def _neighborhood_attention(qkv3, rpb):
    bsz, seq, _ = qkv3.shape
    rows = seq // GRID_W
    k_rows = min(NA_K_ROWS, rows)
    tq = NA_Q_ROWS * GRID_W
    step_rows = NA_BLOCKS_PER_STEP * tq
    bias = _na_bias_tables(rpb, rows)
    return pl.pallas_call(
        functools.partial(_na_body, rows, k_rows),
        grid=(bsz, D_HEADS, seq // step_rows),
        in_specs=[pl.BlockSpec((None, step_rows, HEAD_DIM), lambda b, h, r: (b, r, h)),
                  pl.BlockSpec((None, seq, HEAD_DIM), lambda b, h, r: (b, 0, D_HEADS + h)),
                  pl.BlockSpec((None, seq, HEAD_DIM), lambda b, h, r: (b, 0, 2 * D_HEADS + h)),
                  pl.BlockSpec((None, 3, tq, k_rows * GRID_W), lambda b, h, r: (h, 0, 0, 0))],
        out_specs=pl.BlockSpec((None, step_rows, HEAD_DIM), lambda b, h, r: (b, r, h)),
        out_shape=jax.ShapeDtypeStruct((bsz, seq, D_W), BF16),
        compiler_params=_params("parallel", "parallel", "parallel"),
        name="neighborhood_attention",
    )(qkv3, qkv3, qkv3, bias)


def _merge_body(oa_ref, ob_ref, oc_ref, od_ref, wa_ref, wb_ref, wc_ref, wd_ref,
                ga_ref, gb_ref, gc_ref, gd_ref, out_ref):
    acc = ga_ref[...].astype(F32) * jnp.dot(oa_ref[...], wa_ref[...], preferred_element_type=F32)
    acc = acc + gb_ref[...].astype(F32) * jnp.dot(ob_ref[...], wb_ref[...], preferred_element_type=F32)
    acc = acc + gc_ref[...].astype(F32) * jnp.dot(oc_ref[...], wc_ref[...], preferred_element_type=F32)
    acc = acc + gd_ref[...].astype(F32) * jnp.dot(od_ref[...], wd_ref[...], preferred_element_type=F32)
    out_ref[...] = acc.astype(out_ref.dtype)


def _merge(branches, weights, gates, seq):
    m_tot = branches[0].shape[0]
    d = weights[0].shape[1]
    tm = min(1024, seq)
    tn = min(512, d)
    n_blocks = d // tn
    in_specs = [pl.BlockSpec((tm, o.shape[1]), lambda m, n: (m, 0)) for o in branches]
    in_specs += [pl.BlockSpec((w.shape[0], tn), lambda m, n: (0, n)) for w in weights]
    in_specs += [pl.BlockSpec((tm, tn), functools.partial(lambda m, n, i: (m, i * n_blocks + n), i=i))
                 for i in range(N_BRANCHES)]
    return pl.pallas_call(
        _merge_body,
        grid=(m_tot // tm, n_blocks),
        in_specs=in_specs,
        out_specs=pl.BlockSpec((tm, tn), lambda m, n: (m, n)),
        out_shape=jax.ShapeDtypeStruct((m_tot, d), BF16),
        compiler_params=_params("parallel", "parallel"),
        name="branch_merge",
    )(*branches, *weights, *([gates] * N_BRANCHES))


def _layer_norm_rows(y, gain, bias):
    mu = jnp.mean(y, axis=-1, keepdims=True)
    var = jnp.mean(jnp.square(y - mu), axis=-1, keepdims=True)
    return (y - mu) * lax.rsqrt(var + LN_EPS) * gain + bias


def _out_proj_body(alpha, mixed_ref, w_ref, x_ref, gain_ref, bias_ref, xo_ref, xb_ref, acc_ref):
    y = alpha * x_ref[...] + jnp.dot(mixed_ref[...], w_ref[...], preferred_element_type=F32)
    out = _layer_norm_rows(y, gain_ref[...], bias_ref[...])
    xo_ref[...] = out
    xb_ref[...] = out.astype(BF16)
    acc_ref[...] = alpha * out


def _out_proj(mixed, w, x, gain, bias, alpha, seq):
    m_tot, d = x.shape
    tm = min(256, seq)
    row = pl.BlockSpec((tm, d), lambda m: (m, 0))
    vec = pl.BlockSpec((1, d), lambda m: (0, 0))
    return pl.pallas_call(
        functools.partial(_out_proj_body, alpha),
        grid=(m_tot // tm,),
        in_specs=[row, pl.BlockSpec((d, d), lambda m: (0, 0)), row, vec, vec],
        out_specs=[row, row, row],
        out_shape=[jax.ShapeDtypeStruct((m_tot, d), F32), jax.ShapeDtypeStruct((m_tot, d), BF16),
                   jax.ShapeDtypeStruct((m_tot, d), F32)],
        compiler_params=_params("parallel"),
        name="out_proj_layernorm",
    )(mixed, w, x, gain, bias)


def _router_body(cap, x_ref, wrt_ref, tok_ref, idx_ref, gate_ref):
    seq = x_ref.shape[0]
    logits = _dot_nt(wrt_ref[...], x_ref[...])
    e = jnp.exp(logits - jnp.max(logits, axis=0, keepdims=True))
    aff = e / jnp.sum(e, axis=0, keepdims=True)
    bits = lax.bitcast_convert_type(aff, jnp.int32)
    thr = jnp.zeros((N_EXPERTS, 1), jnp.int32)
    for bit in range(30, -1, -1):
        cand = thr | (1 << bit)
        cnt = jnp.sum((bits >= cand).astype(F32), axis=1, keepdims=True)
        thr = jnp.where(cnt >= cap, cand, thr)
    above = bits > thr
    tied = bits == thr
    need = cap - jnp.sum(above.astype(F32), axis=1, keepdims=True)

    chunk = min(512, seq)
    upper = (lax.broadcasted_iota(jnp.int32, (chunk, chunk), 0)
             <= lax.broadcasted_iota(jnp.int32, (chunk, chunk), 1)).astype(BF16)

    def prefix_count(mask):
        parts, carry = [], jnp.zeros((N_EXPERTS, 1), F32)
        for c in range(seq // chunk):
            part = jnp.dot(mask[:, c * chunk:(c + 1) * chunk].astype(BF16), upper,
                           preferred_element_type=F32) + carry
            parts.append(part)
            carry = part[:, chunk - 1:chunk]
        return jnp.concatenate(parts, axis=1)

    sel = above | (tied & (prefix_count(tied) <= need))
    slot = jnp.where(sel, prefix_count(sel) - 1.0, -1.0)
    gate = jnp.where(sel, aff, 0.0)
    pad = EXPERT_PAD - N_EXPERTS
    gate_ref[...] = jnp.concatenate([gate, jnp.zeros((pad, seq), F32)], axis=0).T

    slot_i = slot.astype(jnp.int32)
    ck = min(1024, seq)
    slot_iota = lax.broadcasted_iota(jnp.int32, (cap, ck), 0)
    lane = lax.broadcasted_iota(jnp.int32, (cap, EXPERT_PAD), 1)
    idx_cols = jnp.zeros((cap, EXPERT_PAD), F32)
    for e in range(N_EXPERTS):
        r = jnp.zeros((cap, LANES), F32)
        for c in range(seq // ck):
            onehot = (slot_iota == slot_i[e:e + 1, c * ck:(c + 1) * ck]).astype(BF16)
            r = r + jnp.dot(onehot, tok_ref[c * ck:(c + 1) * ck, :], preferred_element_type=F32)
        idx_cols = jnp.where(lane == e, r[:, 0:1] * 64.0 + r[:, 1:2], idx_cols)
    idx_ref[...] = idx_cols.T[0:N_EXPERTS, :].astype(jnp.int32) + pl.program_id(0) * seq


def _router(xb3, wrt, cap):
    bsz, seq, d = xb3.shape
    tok = jnp.arange(seq, dtype=jnp.int32)
    tok_tab = jnp.zeros((seq, LANES), F32).at[:, 0].set((tok // 64).astype(F32)).at[:, 1].set((tok % 64).astype(F32))
    idx, gate = pl.pallas_call(
        functools.partial(_router_body, cap),
        grid=(bsz,),
        in_specs=[pl.BlockSpec((None, seq, d), lambda b: (b, 0, 0)),
                  pl.BlockSpec((N_EXPERTS, d), lambda b: (0, 0)),
                  pl.BlockSpec((seq, LANES), lambda b: (0, 0))],
        out_specs=[pl.BlockSpec((None, N_EXPERTS, cap), lambda b: (b, 0, 0)),
                   pl.BlockSpec((None, seq, EXPERT_PAD), lambda b: (b, 0, 0))],
        out_shape=[jax.ShapeDtypeStruct((bsz, N_EXPERTS, cap), jnp.int32),
                   jax.ShapeDtypeStruct((bsz, seq, EXPERT_PAD), F32)],
        compiler_params=_params("parallel"),
        name="router_topk",
    )(xb3, wrt, tok_tab.astype(BF16))
    return idx, gate.reshape(bsz * seq, EXPERT_PAD)


def _row_copy(src, src_row, dst, dst_row, sem):
    return pltpu.make_async_copy(src.at[pl.ds(src_row, 1), :], dst.at[pl.ds(dst_row, 1), :], sem)


def _expert_body(cap, idx_ref, x_hbm, gate_hbm, acc_in_hbm, wg_ref, wu_ref, wd_ref, acc_hbm,
                 x_rows, acc_rows, gate_rows, sems):
    del acc_in_hbm
    e = pl.program_id(0)
    n_chunks, rows, _ = x_rows.shape

    def for_each_slot(fn):
        def body(p, carry):
            fn(p)
            return carry
        lax.fori_loop(0, rows, body, 0, unroll=8)

    def gathers(c, p):
        row = idx_ref[0, c * rows + p]
        return (_row_copy(x_hbm, row, x_rows.at[c], p, sems.at[c, 0]),
                _row_copy(gate_hbm, row, gate_rows.at[c], p, sems.at[c, 1]),
                _row_copy(acc_hbm, row, acc_rows.at[c], p, sems.at[c, 2]))

    def scatter(c, p):
        return _row_copy(acc_rows.at[c], p, acc_hbm, idx_ref[0, c * rows + p], sems.at[c, 3])

    for c in range(n_chunks):
        for_each_slot(lambda p, c=c: [cp.start() for cp in gathers(c, p)])
    for c in range(n_chunks):
        for_each_slot(lambda p, c=c: [cp.wait() for cp in gathers(c, p)])
        x = x_rows[c].astype(BF16)
        g = jnp.dot(x, wg_ref[...], preferred_element_type=F32)
        u = jnp.dot(x, wu_ref[...], preferred_element_type=F32)
        h = (g / (1.0 + jnp.exp(-g))) * u
        y = jnp.dot(h.astype(BF16), wd_ref[...], preferred_element_type=F32)
        lane = lax.broadcasted_iota(jnp.int32, (rows, EXPERT_PAD), 1)
        gate_col = jnp.sum(jnp.where(lane == e, gate_rows[c], 0.0), axis=1, keepdims=True)
        acc_rows[c] = acc_rows[c] + gate_col * y
        for_each_slot(lambda p, c=c: scatter(c, p).start())
    for c in range(n_chunks):
        for_each_slot(lambda p, c=c: scatter(c, p).wait())


def _expert_ffn(idx, xf, gate, acc, wg, wu, wd):
    bsz, n_exp, cap = idx.shape
    m_tot, d = xf.shape
    f = wg.shape[2]
    n_chunks = EXPERT_SLOT_CHUNKS
    rows = cap // n_chunks
    any_spec = pl.BlockSpec(memory_space=pl.ANY)
    return pl.pallas_call(
        functools.partial(_expert_body, cap),
        grid=(n_exp, bsz),
        in_specs=[pl.BlockSpec((None, 1, cap), lambda e, b: (b * n_exp + e, 0, 0), memory_space=pltpu.SMEM),
                  any_spec, any_spec, any_spec,
                  pl.BlockSpec((None, d, f), lambda e, b: (e, 0, 0)),
                  pl.BlockSpec((None, d, f), lambda e, b: (e, 0, 0)),
                  pl.BlockSpec((None, f, d), lambda e, b: (e, 0, 0))],
        out_specs=any_spec,
        out_shape=jax.ShapeDtypeStruct((m_tot, d), F32),
        scratch_shapes=[pltpu.VMEM((n_chunks, rows, d), F32), pltpu.VMEM((n_chunks, rows, d), F32),
                        pltpu.VMEM((n_chunks, rows, EXPERT_PAD), F32), pltpu.SemaphoreType.DMA((n_chunks, 4))],
        input_output_aliases={3: 0},
        compiler_params=_params("arbitrary", "arbitrary"),
        name="expert_swiglu_scatter",
    )(idx.reshape(bsz * n_exp, 1, cap), xf, gate, acc, wg, wu, wd)


def _final_norm_body(acc_ref, gain_ref, bias_ref, xo_ref, xb_ref):
    out = _layer_norm_rows(acc_ref[...], gain_ref[...], bias_ref[...])
    xo_ref[...] = out
    xb_ref[...] = out.astype(BF16)


def _final_norm(acc, gain, bias, seq):
    m_tot, d = acc.shape
    tm = min(512, seq)
    row = pl.BlockSpec((tm, d), lambda m: (m, 0))
    vec = pl.BlockSpec((1, d), lambda m: (0, 0))
    return pl.pallas_call(
        _final_norm_body,
        grid=(m_tot // tm,),
        in_specs=[row, vec, vec],
        out_specs=[row, row],
        out_shape=[jax.ShapeDtypeStruct((m_tot, d), F32), jax.ShapeDtypeStruct((m_tot, d), BF16)],
        compiler_params=_params("parallel"),
        name="channel_mixer_layernorm",
    )(acc, gain, bias)


def _offsets(widths):
    offs = [0]
    for w in widths:
        offs.append(offs[-1] + w)
    return offs


def kernel(x, w_in, b_gate, w_branch, w_out, diff_lambda, diff_subln, sink_logit, na_rpb,
           w_router, w_exp_gate, w_exp_up, w_exp_down, ln_gain, ln_bias):
    bsz, seq, d = x.shape
    depth = w_in.shape[0]
    m_tot = bsz * seq
    alpha = (2.0 * depth) ** 0.25
    cap = EC_CAPACITY_FACTOR * seq // N_EXPERTS
    in_widths = (A_W, A_W, A_W, B_W, B_W, B_W, C_Q_W, C_KV_W, C_KV_W, D_W, D_W, D_W, N_BRANCHES * d)
    io = _offsets(in_widths)
    bo = _offsets((A_OUT_W, B_W, C_Q_W, D_W))
    cos128, sin128 = _rope_tables(seq, HEAD_DIM // 2)
    cos64, sin64 = _rope_tables(seq, B_DIM // 2)

    xf = x.reshape(m_tot, d)
    xb = xf.astype(BF16)
    for l in range(depth):
        lambda_init = 0.8 - 0.6 * math.exp(-0.3 * l)
        def proj(a, b, mode, tn, extra=()):
            return _proj(xb, w_in, l, io[a], io[b] - io[a], mode, tn, seq, extra)

        qk_b = proj(3, 5, "rope64", B_W, (cos64, sin64))
        v_b = proj(5, 6, "plain", B_W)
        qk_c = proj(6, 8, "rope128", C_Q_W + C_KV_W, (cos128, sin128))
        v_c = proj(8, 9, "plain", C_KV_W)
        qkv_d = proj(9, 12, "plain", D_W)
        gates = proj(12, 13, "gate", min(1024, d), (b_gate[l].reshape(1, -1),))

        outs, lses = [], []
        for g, (w, r) in enumerate(DIL_PAIRS):
            qkv_g = _proj_dilated(xb, w_in, l, g, r, bsz, seq, cos128, sin128)
            o_g, lse_g = _banded(qkv_g, qkv_g, qkv_g, 0, 1, 2, hq=A_HEADS_PER_GROUP, group=1,
                                 halo=(w // 2) // r, want_lse=True)
            outs.append(o_g)
            lses.append(lse_g)
        o_a = _group_mix(outs, lses, seq)

        o_b = _diff_attention(qk_b.reshape(bsz, seq, 2 * B_W), v_b.reshape(bsz, seq, B_W),
                              diff_lambda[l], diff_subln[l].reshape(1, -1), lambda_init)

        qk_c4 = qk_c.reshape(bsz, 1, seq, C_Q_W + C_KV_W)
        (o_c,) = _banded(qk_c4, qk_c4, v_c.reshape(bsz, 1, seq, C_KV_W), 0, C_Q_W // C_KV_W, 0,
                         hq=C_Q_HEADS, group=C_Q_HEADS // C_KV_HEADS, halo=C_HALF_WINDOW, sink=sink_logit[l])

        o_d = _neighborhood_attention(qkv_d.reshape(bsz, seq, 3 * D_W), na_rpb[l])

        wb = w_branch[l].astype(BF16)
        mixed = _merge([o_a, o_b.reshape(m_tot, B_W), o_c.reshape(m_tot, C_Q_W), o_d.reshape(m_tot, D_W)],
                       [wb[bo[i]:bo[i + 1]] for i in range(N_BRANCHES)], gates, seq)
        xf, xb, acc = _out_proj(mixed, w_out[l].astype(BF16), xf, ln_gain[l, 0].reshape(1, -1),
                                ln_bias[l, 0].reshape(1, -1), alpha, seq)

        idx, gate = _router(xb.reshape(bsz, seq, d), w_router[l].T.astype(BF16), cap)
        acc = _expert_ffn(idx, xf, gate, acc, w_exp_gate[l].astype(BF16), w_exp_up[l].astype(BF16),
                          w_exp_down[l].astype(BF16))
        xf, xb = _final_norm(acc, ln_gain[l, 1].reshape(1, -1), ln_bias[l, 1].reshape(1, -1), seq)
    return xf.reshape(bsz, seq, d)
```

```python
import functools
import math

import jax
import jax.numpy as jnp
from jax import lax
from jax.experimental import pallas as pl
from jax.experimental.pallas import tpu as pltpu

F32 = jnp.float32
BF16 = jnp.bfloat16

LANES = 128
HEAD_DIM = 128
ROPE_THETA = 10000.0
NEG_INF = -1e30
LN_EPS = 1e-5
SUBLN_EPS = 1e-5
DIL_PAIRS = ((128, 1), (512, 4), (2048, 16))
A_HEADS_PER_GROUP = 6
A_GROUPS = len(DIL_PAIRS)
A_HEADS = A_GROUPS * A_HEADS_PER_GROUP
B_HEADS = 8
B_DIM = 64
C_Q_HEADS = 8
C_KV_HEADS = 2
C_HALF_WINDOW = 128
D_HEADS = 8
GRID_W = 64
NA_WIN_H = 8
NA_WIN_W = 16
NA_Q_ROWS = 8
NA_K_ROWS = 16
NA_BLOCKS_PER_STEP = 4
DIFF_Q_ROWS = 1024
DIFF_SUB_ROWS = 256
DIFF_Q_SCALE = (B_DIM ** -0.5) * math.log2(math.e)
N_BRANCHES = 4
N_EXPERTS = 16
EC_CAPACITY_FACTOR = 2
EXPERT_PAD = 128
EXPERT_SLOT_CHUNKS = 2

A_W = A_HEADS * HEAD_DIM
B_W = B_HEADS * 2 * B_DIM
C_Q_W = C_Q_HEADS * HEAD_DIM
C_KV_W = C_KV_HEADS * HEAD_DIM
D_W = D_HEADS * HEAD_DIM
A_OUT_W = A_HEADS_PER_GROUP * HEAD_DIM

VMEM_LIMIT = 56 * 1024 * 1024


def _params(*sem):
    return pltpu.CompilerParams(dimension_semantics=sem, vmem_limit_bytes=VMEM_LIMIT)


def _dot_nt(a, b):
    return lax.dot_general(a, b, (((1,), (1,)), ((), ())), preferred_element_type=F32)


def _cast_weight_tile(w_ref, wb_ref):
    @pl.when(pl.program_id(1) == 0)
    def _():
        wb_ref[...] = w_ref[...].astype(BF16)
    return wb_ref[...]


def _proj_body(mode, x_ref, w_ref, *rest):
    rest, wb_ref = rest[:-1], rest[-1]
    acc = jnp.dot(x_ref[...], _cast_weight_tile(w_ref, wb_ref), preferred_element_type=F32)
    if mode == "plain":
        (o_ref,) = rest
        o_ref[...] = acc.astype(o_ref.dtype)
    elif mode == "gate":
        b_ref, o_ref = rest
        z = acc + b_ref[...]
        o_ref[...] = (1.0 / (1.0 + jnp.exp(-z))).astype(o_ref.dtype)
    else:
        cos_ref, sin_ref, o_ref = rest
        cos = cos_ref[...]
        sin = sin_ref[...]
        if mode == "rope64":
            qs = jnp.where(pl.program_id(0) == 0, DIFF_Q_SCALE, 1.0).astype(F32)
            cos = cos * qs
            sin = sin * qs
        lane = lax.broadcasted_iota(jnp.int32, cos.shape, 1)
        for c in range(acc.shape[1] // LANES):
            a = acc[:, c * LANES:(c + 1) * LANES]
            if mode == "rope128":
                rot = pltpu.roll(a, 64, 1)
            else:
                rot = jnp.where((lane % 64) < 32, pltpu.roll(a, 96, 1), pltpu.roll(a, 32, 1))
            o_ref[:, c * LANES:(c + 1) * LANES] = (a * cos + rot * sin).astype(o_ref.dtype)


def _weight_tile_spec(k, tn, layer, lane_tile_of):
    return pl.BlockSpec((pl.Squeezed(), pl.Element(k), pl.Element(tn)),
                        lambda n, m: (layer, 0, lane_tile_of(n) * LANES))


def _proj(xb, w_stack, layer, col0, n_tot, mode, tn, seq, extra=()):
    m_tot, k = xb.shape
    tm = min(1024, seq)
    pos_blocks = seq // tm
    in_specs = [pl.BlockSpec((tm, k), lambda n, m: (m, 0)),
                _weight_tile_spec(k, tn, layer, lambda n: col0 // LANES + n * (tn // LANES))]
    if mode == "gate":
        in_specs.append(pl.BlockSpec((1, tn), lambda n, m: (0, n)))
    elif mode != "plain":
        in_specs += [pl.BlockSpec((tm, LANES), lambda n, m: (m % pos_blocks, 0))] * 2
    return pl.pallas_call(
        functools.partial(_proj_body, mode),
        grid=(n_tot // tn, m_tot // tm),
        in_specs=in_specs,
        out_specs=pl.BlockSpec((tm, tn), lambda n, m: (m, n)),
        out_shape=jax.ShapeDtypeStruct((m_tot, n_tot), BF16),
        scratch_shapes=[pltpu.VMEM((k, tn), BF16)],
        compiler_params=_params("parallel", "arbitrary"),
        name="proj_" + mode,
    )(xb, w_stack, *extra)


def _proj_dilated_body(dil, x_ref, w_ref, cos_ref, sin_ref, o_ref, wb_ref, *scratch):
    n = pl.program_id(0)
    tm = x_ref.shape[0]
    rows = tm // dil
    acc = jnp.dot(x_ref[...], _cast_weight_tile(w_ref, wb_ref), preferred_element_type=F32)
    dst = scratch[0] if dil > 1 else None

    def emit(c, val):
        if dil > 1:
            dst[c] = val
        else:
            o_ref[0, :, c * LANES:(c + 1) * LANES] = val.astype(o_ref.dtype)

    @pl.when(n < 2)
    def _():
        cos = cos_ref[...]
        sin = sin_ref[...]
        for c in range(acc.shape[1] // LANES):
            a = acc[:, c * LANES:(c + 1) * LANES]
            emit(c, a * cos + pltpu.roll(a, 64, 1) * sin)

    @pl.when(n >= 2)
    def _():
        for c in range(acc.shape[1] // LANES):
            emit(c, acc[:, c * LANES:(c + 1) * LANES])

    if dil > 1:
        for c in range(acc.shape[1] // LANES):
            for j in range(dil):
                o_ref[j, :, c * LANES:(c + 1) * LANES] = (
                    dst.at[c][pl.ds(j, rows, stride=dil), :].astype(o_ref.dtype))


def _proj_dilated(xb, w_stack, layer, group, dil, bsz, seq, cos, sin):
    m_tot, k = xb.shape
    tm = min(1024, seq)
    tn = A_OUT_W
    pos_blocks = seq // tm
    scratch = [pltpu.VMEM((k, tn), BF16)]
    if dil > 1:
        scratch.append(pltpu.VMEM((tn // LANES, tm, LANES), F32))
    return pl.pallas_call(
        functools.partial(_proj_dilated_body, dil),
        grid=(3, m_tot // tm),
        in_specs=[pl.BlockSpec((tm, k), lambda n, m: (m, 0)),
                  _weight_tile_spec(k, tn, layer, lambda n: n * (A_W // LANES) + group * (tn // LANES)),
                  pl.BlockSpec((tm, LANES), lambda n, m: (m % pos_blocks, 0)),
                  pl.BlockSpec((tm, LANES), lambda n, m: (m % pos_blocks, 0))],
        out_specs=pl.BlockSpec((None, dil, tm // dil, tn), lambda n, m: (m // pos_blocks, 0, m % pos_blocks, n)),
        out_shape=jax.ShapeDtypeStruct((bsz, dil, seq // dil, 3 * tn), BF16),
        scratch_shapes=scratch,
        compiler_params=_params("parallel", "arbitrary"),
        name="proj_dilated",
    )(xb, w_stack, cos, sin)


def _rope_tables(seq, half):
    inv = ROPE_THETA ** (-jnp.arange(half, dtype=F32) / half)
    ang = jnp.arange(seq, dtype=jnp.int32).astype(F32)[:, None] * inv[None, :]
    cos, sin = jnp.cos(ang), jnp.sin(ang)
    reps = LANES // (2 * half)
    return (jnp.tile(jnp.concatenate([cos, cos], axis=1), (1, reps)),
            jnp.tile(jnp.concatenate([-sin, sin], axis=1), (1, reps)))


def _banded_body(hq, group, halo, tq, win, length, has_sink, want_lse, *refs):
    refs = list(refs)
    q_ref, k_ref, v_ref = refs[:3]
    pos = 3
    sink_ref = None
    if has_sink:
        sink_ref = refs[pos]
        pos += 1
    o_ref = refs[pos]
    lse_ref = refs[pos + 1] if want_lse else None

    i = pl.program_id(2)
    start = pl.multiple_of(jnp.clip(i * tq - halo, 0, length - win), 64)
    qpos = i * tq + lax.broadcasted_iota(jnp.int32, (tq, win), 0)
    kpos = start + lax.broadcasted_iota(jnp.int32, (tq, win), 1)
    valid = jnp.abs(qpos - kpos) <= halo
    scale = HEAD_DIM ** -0.5
    lane = lax.broadcasted_iota(jnp.int32, (tq, LANES), 1)
    lse_acc = jnp.zeros((tq, LANES), F32)
    for h in range(hq):
        kh = h // group
        q = q_ref[:, h * HEAD_DIM:(h + 1) * HEAD_DIM]
        k = k_ref[pl.ds(start, win), kh * HEAD_DIM:(kh + 1) * HEAD_DIM]
        v = v_ref[pl.ds(start, win), kh * HEAD_DIM:(kh + 1) * HEAD_DIM]
        s = _dot_nt(q, k) * scale
        s = jnp.where(valid, s, NEG_INF)
        m = jnp.max(s, axis=-1, keepdims=True)
        if has_sink:
            sk = sink_ref[h]
            m = jnp.maximum(m, sk)
        p = jnp.exp(s - m)
        d = jnp.sum(p, axis=-1, keepdims=True)
        if has_sink:
            d = d + jnp.exp(sk - m)
        o = jnp.dot(p.astype(BF16), v, preferred_element_type=F32) / d
        o_ref[:, h * HEAD_DIM:(h + 1) * HEAD_DIM] = o.astype(o_ref.dtype)
        if want_lse:
            lse_acc = jnp.where(lane == h, m + jnp.log(d), lse_acc)
    if want_lse:
        lse_ref[...] = lse_acc


def _banded(q4, k4, v4, qcol, kcol, vcol, *, hq, group, halo, sink=None, want_lse=False):
    bsz, dil, length, _ = q4.shape
    hkv = hq // group
    tq = min(256, length)
    win = min(length, tq + 2 * halo)
    qw, kw = hq * HEAD_DIM, hkv * HEAD_DIM
    in_specs = [pl.BlockSpec((None, None, tq, qw), lambda b, j, i: (b, j, i, qcol)),
                pl.BlockSpec((None, None, length, kw), lambda b, j, i: (b, j, 0, kcol)),
                pl.BlockSpec((None, None, length, kw), lambda b, j, i: (b, j, 0, vcol))]
    args = [q4, k4, v4]
    if sink is not None:
        in_specs.append(pl.BlockSpec(memory_space=pltpu.SMEM))
        args.append(sink)
    out_specs = [pl.BlockSpec((None, None, tq, qw), lambda b, j, i: (b, j, i, 0))]
    out_shape = [jax.ShapeDtypeStruct((bsz, dil, length, qw), BF16)]
    if want_lse:
        out_specs.append(pl.BlockSpec((None, None, tq, LANES), lambda b, j, i: (b, j, i, 0)))
        out_shape.append(jax.ShapeDtypeStruct((bsz, dil, length, LANES), F32))
    return pl.pallas_call(
        functools.partial(_banded_body, hq, group, halo, tq, win, length, sink is not None, want_lse),
        grid=(bsz, dil, length // tq),
        in_specs=in_specs,
        out_specs=out_specs,
        out_shape=out_shape,
        compiler_params=_params("parallel", "parallel", "parallel"),
        name="banded_attention",
    )(*args)


def _group_mix_body(dils, *refs):
    n = len(dils)
    o_refs, l_refs, out_ref = refs[:n], refs[n:2 * n], refs[2 * n]
    scratch = list(refs[2 * n + 1:])
    tm = out_ref.shape[0]
    heads, lses = [], []
    for g, dil in enumerate(dils):
        if dil == 1:
            heads.append(functools.partial(
                lambda h, ref: ref[0, :, h * HEAD_DIM:(h + 1) * HEAD_DIM].astype(F32), ref=o_refs[g]))
            lses.append(l_refs[g][0])
            continue
        o_scr, l_scr = scratch.pop(0), scratch.pop(0)
        rows = tm // dil
        for j in range(dil):
            l_scr[pl.ds(j, rows, stride=dil), :] = l_refs[g][j]
            for h in range(A_HEADS_PER_GROUP):
                o_scr.at[h][pl.ds(j, rows, stride=dil), :] = (
                    o_refs[g][j, :, h * HEAD_DIM:(h + 1) * HEAD_DIM].astype(F32))
        heads.append(functools.partial(lambda h, ref: ref[h], ref=o_scr))
        lses.append(l_scr[...])
    m = functools.reduce(jnp.maximum, lses)
    es = [jnp.exp(l - m) for l in lses]
    tot = functools.reduce(lambda a, b: a + b, es)
    ws = [e / tot for e in es]
    for h in range(A_HEADS_PER_GROUP):
        acc = ws[0][:, h:h + 1] * heads[0](h)
        for g in range(1, n):
            acc = acc + ws[g][:, h:h + 1] * heads[g](h)
        out_ref[:, h * HEAD_DIM:(h + 1) * HEAD_DIM] = acc.astype(out_ref.dtype)


def _group_mix(outs, lses, seq):
    bsz = outs[0].shape[0]
    dils = tuple(o.shape[1] for o in outs)
    tm = min(512, seq)
    pos_blocks = seq // tm

    def spec(dil, width):
        return pl.BlockSpec((None, dil, tm // dil, width), lambda m: (m // pos_blocks, 0, m % pos_blocks, 0))

    scratch = []
    for dil in dils:
        if dil > 1:
            scratch += [pltpu.VMEM((A_HEADS_PER_GROUP, tm, HEAD_DIM), F32), pltpu.VMEM((tm, LANES), F32)]
    return pl.pallas_call(
        functools.partial(_group_mix_body, dils),
        grid=(bsz * pos_blocks,),
        in_specs=[spec(dil, A_OUT_W) for dil in dils] + [spec(dil, LANES) for dil in dils],
        out_specs=pl.BlockSpec((tm, A_OUT_W), lambda m: (m, 0)),
        out_shape=jax.ShapeDtypeStruct((bsz * seq, A_OUT_W), BF16),
        scratch_shapes=scratch,
        compiler_params=_params("parallel"),
        name="dilation_group_mix",
    )(*outs, *lses)


def _diff_body(lambda_init, lam_ref, subln_ref, q_ref, k_ref, v_ref, o_ref):
    lam = lam_ref[...]
    dot1 = jnp.sum(lam[0:1, :] * lam[1:2, :], axis=1, keepdims=True)
    dot2 = jnp.sum(lam[2:3, :] * lam[3:4, :], axis=1, keepdims=True)
    lmbda = jnp.exp(dot1) - jnp.exp(dot2) + lambda_init
    k = k_ref[...]
    v = v_ref[...]
    sub = min(DIFF_SUB_ROWS, q_ref.shape[0])
    lane = lax.broadcasted_iota(jnp.int32, (sub, 2 * B_DIM), 1)
    zero = jnp.zeros((sub, 2 * B_DIM), BF16)

    def softmax_map_times_v(qm):
        s = _dot_nt(qm, k)
        p = jnp.exp2(s - jnp.max(s, axis=-1, keepdims=True))
        return jnp.dot(p.astype(BF16), v, preferred_element_type=F32) / jnp.sum(p, axis=-1, keepdims=True)

    for r0 in range(0, q_ref.shape[0], sub):
        q = q_ref[r0:r0 + sub, :]
        o = (softmax_map_times_v(jnp.where(lane < B_DIM, q, zero))
             - lmbda * softmax_map_times_v(jnp.where(lane >= B_DIM, q, zero)))
        o = o * lax.rsqrt(jnp.mean(o * o, axis=-1, keepdims=True) + SUBLN_EPS) * subln_ref[...]
        o_ref[r0:r0 + sub, :] = (o * (1.0 - lambda_init)).astype(o_ref.dtype)


def _diff_attention(qk3, v3, lam, subln, lambda_init):
    bsz, seq, _ = qk3.shape
    tq = min(DIFF_Q_ROWS, seq)
    hw = 2 * B_DIM
    return pl.pallas_call(
        functools.partial(_diff_body, lambda_init),
        grid=(bsz, B_HEADS, seq // tq),
        in_specs=[pl.BlockSpec((4, B_DIM), lambda b, h, i: (0, 0)),
                  pl.BlockSpec((1, hw), lambda b, h, i: (0, 0)),
                  pl.BlockSpec((None, tq, hw), lambda b, h, i: (b, i, h)),
                  pl.BlockSpec((None, seq, hw), lambda b, h, i: (b, 0, B_HEADS + h)),
                  pl.BlockSpec((None, seq, hw), lambda b, h, i: (b, 0, h))],
        out_specs=pl.BlockSpec((None, tq, hw), lambda b, h, i: (b, i, h)),
        out_shape=jax.ShapeDtypeStruct((bsz, seq, B_W), BF16),
        compiler_params=_params("parallel", "parallel", "parallel"),
        name="diff_attention",
    )(lam, subln, qk3, qk3, v3)


def _na_body(rows, k_rows, blocks, q_ref, k_ref, v_ref, bias_ref, o_ref):
    n_rb = rows // NA_Q_ROWS
    tq = NA_Q_ROWS * GRID_W
    for u in range(blocks):
        rb = pl.program_id(2) * blocks + u
        w0 = jnp.clip(rb * NA_Q_ROWS - NA_WIN_H // 2, 0, rows - k_rows) * GRID_W
        w0 = pl.multiple_of(w0, 256)
        kind = jnp.where(rb == 0, 0, jnp.where(rb == n_rb - 1, 2, 1))
        k = k_ref[pl.ds(w0, k_rows * GRID_W), :]
        v = v_ref[pl.ds(w0, k_rows * GRID_W), :]
        s = _dot_nt(q_ref[u * tq:(u + 1) * tq, :], k) * (HEAD_DIM ** -0.5) + bias_ref[kind]
        p = jnp.exp(s - jnp.max(s, axis=-1, keepdims=True))
        d = jnp.sum(p, axis=-1, keepdims=True)
        o = jnp.dot(p.astype(BF16), v, preferred_element_type=F32) / d
        o_ref[u * tq:(u + 1) * tq, :] = o.astype(o_ref.dtype)


def _na_bias_tables(rpb, rows):
    kh = min(NA_WIN_H, rows)
    k_rows = min(NA_K_ROWS, rows)
    n_rb = rows // NA_Q_ROWS
    n_heads = rpb.shape[0]
    edge = GRID_W - NA_WIN_W
    ext = jnp.pad(rpb.astype(F32), ((0, 0), (0, 0), (edge, edge)), mode="edge")
    toeplitz = jnp.stack([ext[:, :, GRID_W - 1 - qc:2 * GRID_W - 1 - qc] for qc in range(GRID_W)], axis=2)
    c = jnp.arange(GRID_W)
    col_start = jnp.clip(c - NA_WIN_W // 2, 0, GRID_W - NA_WIN_W)
    col_ok = (c[None, :] >= col_start[:, None]) & (c[None, :] < col_start[:, None] + NA_WIN_W)
    toeplitz = jnp.where(col_ok[None, None], toeplitz, NEG_INF)
    tabs = []
    for rb in (0, min(1, n_rb - 1), n_rb - 1):
        r0 = rb * NA_Q_ROWS
        w0 = min(max(r0 - NA_WIN_H // 2, 0), rows - k_rows)
        strips = []
        for a in range(NA_Q_ROWS):
            qrow = r0 + a
            start = min(max(qrow - kh // 2, 0), rows - kh)
            first = start - qrow + (NA_WIN_H - 1)
            blk = toeplitz[:, first:first + kh].transpose(0, 2, 1, 3)
            blk = jnp.pad(blk, ((0, 0), (0, 0), (start - w0, k_rows - kh - (start - w0)), (0, 0)),
                          constant_values=NEG_INF)
            strips.append(blk.reshape(n_heads, GRID_W, k_rows * GRID_W))
        tabs.append(jnp.concatenate(strips, axis=1))
    return jnp.stack(tabs, axis=1)


def _neighborhood_attention(qkv3, rpb):
    bsz, seq, _ = qkv3.shape
    rows = seq // GRID_W
    k_rows = min(NA_K_ROWS, rows)
    tq = NA_Q_ROWS * GRID_W
    blocks = min(NA_BLOCKS_PER_STEP, rows // NA_Q_ROWS)
    step_rows = blocks * tq
    bias = _na_bias_tables(rpb, rows)
    return pl.pallas_call(
        functools.partial(_na_body, rows, k_rows, blocks),
        grid=(bsz, D_HEADS, seq // step_rows),
        in_specs=[pl.BlockSpec((None, step_rows, HEAD_DIM), lambda b, h, r: (b, r, h)),
                  pl.BlockSpec((None, seq, HEAD_DIM), lambda b, h, r: (b, 0, D_HEADS + h)),
                  pl.BlockSpec((None, seq, HEAD_DIM), lambda b, h, r: (b, 0, 2 * D_HEADS + h)),
                  pl.BlockSpec((None, 3, tq, k_rows * GRID_W), lambda b, h, r: (h, 0, 0, 0))],
        out_specs=pl.BlockSpec((None, step_rows, HEAD_DIM), lambda b, h, r: (b, r, h)),
        out_shape=jax.ShapeDtypeStruct((bsz, seq, D_W), BF16),
        compiler_params=_params("parallel", "parallel", "parallel"),
        name="neighborhood_attention",
    )(qkv3, qkv3, qkv3, bias)


def _merge_body(oa_ref, ob_ref, oc_ref, od_ref, wa_ref, wb_ref, wc_ref, wd_ref,
                ga_ref, gb_ref, gc_ref, gd_ref, out_ref):
    acc = ga_ref[...].astype(F32) * jnp.dot(oa_ref[...], wa_ref[...], preferred_element_type=F32)
    acc = acc + gb_ref[...].astype(F32) * jnp.dot(ob_ref[...], wb_ref[...], preferred_element_type=F32)
    acc = acc + gc_ref[...].astype(F32) * jnp.dot(oc_ref[...], wc_ref[...], preferred_element_type=F32)
    acc = acc + gd_ref[...].astype(F32) * jnp.dot(od_ref[...], wd_ref[...], preferred_element_type=F32)
    out_ref[...] = acc.astype(out_ref.dtype)


def _merge(branches, weights, gates, seq):
    m_tot = branches[0].shape[0]
    d = weights[0].shape[1]
    tm = min(1024, seq)
    tn = min(512, d)
    n_blocks = d // tn
    in_specs = [pl.BlockSpec((tm, o.shape[1]), lambda m, n: (m, 0)) for o in branches]
    in_specs += [pl.BlockSpec((w.shape[0], tn), lambda m, n: (0, n)) for w in weights]
    in_specs += [pl.BlockSpec((tm, tn), functools.partial(lambda m, n, i: (m, i * n_blocks + n), i=i))
                 for i in range(N_BRANCHES)]
    return pl.pallas_call(
        _merge_body,
        grid=(m_tot // tm, n_blocks),
        in_specs=in_specs,
        out_specs=pl.BlockSpec((tm, tn), lambda m, n: (m, n)),
        out_shape=jax.ShapeDtypeStruct((m_tot, d), BF16),
        compiler_params=_params("parallel", "parallel"),
        name="branch_merge",
    )(*branches, *weights, *([gates] * N_BRANCHES))


def _layer_norm_rows(y, gain, bias):
    mu = jnp.mean(y, axis=-1, keepdims=True)
    var = jnp.mean(jnp.square(y - mu), axis=-1, keepdims=True)
    return (y - mu) * lax.rsqrt(var + LN_EPS) * gain + bias


def _out_proj_body(alpha, mixed_ref, w_ref, x_ref, gain_ref, bias_ref, xb_ref, state_ref):
    d = x_ref.shape[1]
    y = alpha * x_ref[...] + jnp.dot(mixed_ref[...], w_ref[...], preferred_element_type=F32)
    out = _layer_norm_rows(y, gain_ref[...], bias_ref[...])
    xb_ref[...] = out.astype(BF16)
    state_ref[:, 0:d] = out
    state_ref[:, d:2 * d] = alpha * out
    state_ref[:, 2 * d:] = jnp.zeros((out.shape[0], EXPERT_PAD), F32)


def _out_proj(mixed, w, x, gain, bias, alpha, seq):
    m_tot, d = x.shape
    tm = min(256, seq)
    row = pl.BlockSpec((tm, d), lambda m: (m, 0))
    vec = pl.BlockSpec((1, d), lambda m: (0, 0))
    return pl.pallas_call(
        functools.partial(_out_proj_body, alpha),
        grid=(m_tot // tm,),
        in_specs=[row, pl.BlockSpec((d, d), lambda m: (0, 0)), row, vec, vec],
        out_specs=[row, pl.BlockSpec((tm, 2 * d + EXPERT_PAD), lambda m: (m, 0))],
        out_shape=[jax.ShapeDtypeStruct((m_tot, d), BF16),
                   jax.ShapeDtypeStruct((m_tot, 2 * d + EXPERT_PAD), F32)],
        compiler_params=_params("parallel"),
        name="out_proj_layernorm",
    )(mixed, w, x, gain, bias)


def _router_body(cap, x_ref, wrt_ref, tok_ref, state_in_hbm, idx_ref, gate_ref):
    del state_in_hbm
    seq = x_ref.shape[0]
    logits = _dot_nt(wrt_ref[...], x_ref[...])
    e = jnp.exp(logits - jnp.max(logits, axis=0, keepdims=True))
    aff = e / jnp.sum(e, axis=0, keepdims=True)
    bits = lax.bitcast_convert_type(aff, jnp.int32)
    thr = jnp.zeros((N_EXPERTS, 1), jnp.int32)
    for bit in range(30, -1, -1):
        cand = thr | (1 << bit)
        cnt = jnp.sum((bits >= cand).astype(F32), axis=1, keepdims=True)
        thr = jnp.where(cnt >= cap, cand, thr)
    above = bits > thr
    tied = bits == thr
    need = cap - jnp.sum(above.astype(F32), axis=1, keepdims=True)

    chunk = min(512, seq)
    upper = (lax.broadcasted_iota(jnp.int32, (chunk, chunk), 0)
             <= lax.broadcasted_iota(jnp.int32, (chunk, chunk), 1)).astype(BF16)

    def prefix_count(mask):
        parts, carry = [], jnp.zeros((N_EXPERTS, 1), F32)
        for c in range(seq // chunk):
            part = jnp.dot(mask[:, c * chunk:(c + 1) * chunk].astype(BF16), upper,
                           preferred_element_type=F32) + carry
            parts.append(part)
            carry = part[:, chunk - 1:chunk]
        return jnp.concatenate(parts, axis=1)

    sel = above | (tied & (prefix_count(tied) <= need))
    slot = jnp.where(sel, prefix_count(sel) - 1.0, -1.0)
    gate = jnp.where(sel, aff, 0.0)
    pad = EXPERT_PAD - N_EXPERTS
    gate_ref[...] = jnp.concatenate([gate, jnp.zeros((pad, seq), F32)], axis=0).T

    slot_i = slot.astype(jnp.int32)
    ck = min(1024, seq)
    slot_iota = lax.broadcasted_iota(jnp.int32, (cap, ck), 0)
    lane = lax.broadcasted_iota(jnp.int32, (cap, EXPERT_PAD), 1)
    idx_cols = jnp.zeros((cap, EXPERT_PAD), F32)
    for e in range(N_EXPERTS):
        r = jnp.zeros((cap, LANES), F32)
        for c in range(seq // ck):
            onehot = (slot_iota == slot_i[e:e + 1, c * ck:(c + 1) * ck]).astype(BF16)
            r = r + jnp.dot(onehot, tok_ref[c * ck:(c + 1) * ck, :], preferred_element_type=F32)
        idx_cols = jnp.where(lane == e, r[:, 0:1] * 64.0 + r[:, 1:2], idx_cols)
    idx_ref[...] = idx_cols.T[0:N_EXPERTS, :].astype(jnp.int32) + pl.program_id(0) * seq


def _router(xb3, wrt, state, cap):
    bsz, seq, d = xb3.shape
    tok = jnp.arange(seq, dtype=jnp.int32)
    tok_tab = jnp.zeros((seq, LANES), F32).at[:, 0].set((tok // 64).astype(F32)).at[:, 1].set((tok % 64).astype(F32))
    return pl.pallas_call(
        functools.partial(_router_body, cap),
        grid=(bsz,),
        in_specs=[pl.BlockSpec((None, seq, d), lambda b: (b, 0, 0)),
                  pl.BlockSpec((N_EXPERTS, d), lambda b: (0, 0)),
                  pl.BlockSpec((seq, LANES), lambda b: (0, 0)),
                  pl.BlockSpec(memory_space=pl.ANY)],
        out_specs=[pl.BlockSpec((None, N_EXPERTS, cap), lambda b: (b, 0, 0)),
                   pl.BlockSpec((seq, EXPERT_PAD), lambda b: (b, 2 * d // EXPERT_PAD))],
        out_shape=[jax.ShapeDtypeStruct((bsz, N_EXPERTS, cap), jnp.int32),
                   jax.ShapeDtypeStruct(state.shape, F32)],
        input_output_aliases={3: 1},
        compiler_params=_params("parallel"),
        name="router_topk",
    )(xb3, wrt, tok_tab.astype(BF16), state)


def _expert_body(d, idx_ref, state_in_hbm, wg_ref, wu_ref, wd_ref, state_hbm, rows_buf, sems):
    del state_in_hbm
    e = pl.program_id(0)
    n_chunks, rows, _ = rows_buf.shape

    def for_each_slot(fn):
        def body(p, carry):
            fn(p)
            return carry
        lax.fori_loop(0, rows, body, 0, unroll=8)

    def gather(c, p):
        row = idx_ref[0, c * rows + p]
        return pltpu.make_async_copy(state_hbm.at[pl.ds(row, 1), :], rows_buf.at[c].at[pl.ds(p, 1), :],
                                     sems.at[c, 0])

    def scatter(c, p):
        row = idx_ref[0, c * rows + p]
        return pltpu.make_async_copy(rows_buf.at[c].at[pl.ds(p, 1), pl.ds(d, d)],
                                     state_hbm.at[pl.ds(row, 1), pl.ds(d, d)], sems.at[c, 1])

    for c in range(n_chunks):
        for_each_slot(lambda p, c=c: gather(c, p).start())
    for c in range(n_chunks):
        for_each_slot(lambda p, c=c: gather(c, p).wait())
        x = rows_buf[c, :, 0:d].astype(BF16)
        g = jnp.dot(x, wg_ref[...], preferred_element_type=F32)
        u = jnp.dot(x, wu_ref[...], preferred_element_type=F32)
        h = (g / (1.0 + jnp.exp(-g))) * u
        y = jnp.dot(h.astype(BF16), wd_ref[...], preferred_element_type=F32)
        lane = lax.broadcasted_iota(jnp.int32, (rows, EXPERT_PAD), 1)
        gates = rows_buf[c, :, 2 * d:2 * d + EXPERT_PAD]
        gate_col = jnp.sum(jnp.where(lane == e, gates, 0.0), axis=1, keepdims=True)
        rows_buf[c, :, d:2 * d] = rows_buf[c, :, d:2 * d] + gate_col * y
        for_each_slot(lambda p, c=c: scatter(c, p).start())
    for c in range(n_chunks):
        for_each_slot(lambda p, c=c: scatter(c, p).wait())


def _expert_ffn(idx, state, layer, wg, wu, wd):
    bsz, n_exp, cap = idx.shape
    m_tot, width = state.shape
    d = (width - EXPERT_PAD) // 2
    f = wg.shape[3]
    n_chunks = EXPERT_SLOT_CHUNKS
    any_spec = pl.BlockSpec(memory_space=pl.ANY)
    return pl.pallas_call(
        functools.partial(_expert_body, d),
        grid=(n_exp, bsz),
        in_specs=[pl.BlockSpec((None, 1, cap), lambda e, b: (b * n_exp + e, 0, 0), memory_space=pltpu.SMEM),
                  any_spec,
                  pl.BlockSpec((None, None, d, f), lambda e, b: (layer, e, 0, 0)),
                  pl.BlockSpec((None, None, d, f), lambda e, b: (layer, e, 0, 0)),
                  pl.BlockSpec((None, None, f, d), lambda e, b: (layer, e, 0, 0))],
        out_specs=any_spec,
        out_shape=jax.ShapeDtypeStruct((m_tot, width), F32),
        scratch_shapes=[pltpu.VMEM((n_chunks, cap // n_chunks, width), F32),
                        pltpu.SemaphoreType.DMA((n_chunks, 2))],
        input_output_aliases={1: 0},
        compiler_params=_params("arbitrary", "arbitrary"),
        name="expert_swiglu_scatter",
    )(idx.reshape(bsz * n_exp, 1, cap), state, wg, wu, wd)


def _final_norm_body(acc_ref, gain_ref, bias_ref, xo_ref, xb_ref):
    out = _layer_norm_rows(acc_ref[...], gain_ref[...], bias_ref[...])
    xo_ref[...] = out
    xb_ref[...] = out.astype(BF16)


def _final_norm(state, gain, bias, seq):
    m_tot = state.shape[0]
    d = gain.shape[1]
    tm = min(512, seq)
    row = pl.BlockSpec((tm, d), lambda m: (m, 0))
    vec = pl.BlockSpec((1, d), lambda m: (0, 0))
    return pl.pallas_call(
        _final_norm_body,
        grid=(m_tot // tm,),
        in_specs=[pl.BlockSpec((tm, d), lambda m: (m, 1)), vec, vec],
        out_specs=[row, row],
        out_shape=[jax.ShapeDtypeStruct((m_tot, d), F32), jax.ShapeDtypeStruct((m_tot, d), BF16)],
        compiler_params=_params("parallel"),
        name="channel_mixer_layernorm",
    )(state, gain, bias)


def _offsets(widths):
    offs = [0]
    for w in widths:
        offs.append(offs[-1] + w)
    return offs


def kernel(x, w_in, b_gate, w_branch, w_out, diff_lambda, diff_subln, sink_logit, na_rpb,
           w_router, w_exp_gate, w_exp_up, w_exp_down, ln_gain, ln_bias):
    bsz, seq, d = x.shape
    depth = w_in.shape[0]
    m_tot = bsz * seq
    alpha = (2.0 * depth) ** 0.25
    cap = EC_CAPACITY_FACTOR * seq // N_EXPERTS
    in_widths = (A_W, A_W, A_W, B_W, B_W, B_W, C_Q_W, C_KV_W, C_KV_W, D_W, D_W, D_W, N_BRANCHES * d)
    io = _offsets(in_widths)
    bo = _offsets((A_OUT_W, B_W, C_Q_W, D_W))
    cos128, sin128 = _rope_tables(seq, HEAD_DIM // 2)
    cos64, sin64 = _rope_tables(seq, B_DIM // 2)

    wg_stack, wu_stack, wd_stack = (w.astype(BF16) for w in (w_exp_gate, w_exp_up, w_exp_down))
    xf = x.reshape(m_tot, d)
    xb = xf.astype(BF16)
    for l in range(depth):
        lambda_init = 0.8 - 0.6 * math.exp(-0.3 * l)
        def proj(a, b, mode, tn, extra=()):
            return _proj(xb, w_in, l, io[a], io[b] - io[a], mode, tn, seq, extra)

        qk_b = proj(3, 5, "rope64", B_W, (cos64, sin64))
        v_b = proj(5, 6, "plain", B_W)
        qk_c = proj(6, 8, "rope128", C_Q_W + C_KV_W, (cos128, sin128))
        v_c = proj(8, 9, "plain", C_KV_W)
        qkv_d = proj(9, 12, "plain", D_W)
        gates = proj(12, 13, "gate", min(1024, d), (b_gate[l].reshape(1, -1),))

        outs, lses = [], []
        for g, (w, r) in enumerate(DIL_PAIRS):
            qkv_g = _proj_dilated(xb, w_in, l, g, r, bsz, seq, cos128, sin128)
            o_g, lse_g = _banded(qkv_g, qkv_g, qkv_g, 0, 1, 2, hq=A_HEADS_PER_GROUP, group=1,
                                 halo=(w // 2) // r, want_lse=True)
            outs.append(o_g)
            lses.append(lse_g)
        o_a = _group_mix(outs, lses, seq)

        o_b = _diff_attention(qk_b.reshape(bsz, seq, 2 * B_W), v_b.reshape(bsz, seq, B_W),
                              diff_lambda[l], diff_subln[l].reshape(1, -1), lambda_init)

        qk_c4 = qk_c.reshape(bsz, 1, seq, C_Q_W + C_KV_W)
        (o_c,) = _banded(qk_c4, qk_c4, v_c.reshape(bsz, 1, seq, C_KV_W), 0, C_Q_W // C_KV_W, 0,
                         hq=C_Q_HEADS, group=C_Q_HEADS // C_KV_HEADS, halo=C_HALF_WINDOW, sink=sink_logit[l])

        o_d = _neighborhood_attention(qkv_d.reshape(bsz, seq, 3 * D_W), na_rpb[l])

        wb = w_branch[l].astype(BF16)
        mixed = _merge([o_a, o_b.reshape(m_tot, B_W), o_c.reshape(m_tot, C_Q_W), o_d.reshape(m_tot, D_W)],
                       [wb[bo[i]:bo[i + 1]] for i in range(N_BRANCHES)], gates, seq)
        xb, state = _out_proj(mixed, w_out[l].astype(BF16), xf, ln_gain[l, 0].reshape(1, -1),
                              ln_bias[l, 0].reshape(1, -1), alpha, seq)

        idx, state = _router(xb.reshape(bsz, seq, d), w_router[l].T.astype(BF16), state, cap)
        state = _expert_ffn(idx, state, l, wg_stack, wu_stack, wd_stack)
        xf, xb = _final_norm(state, ln_gain[l, 1].reshape(1, -1), ln_bias[l, 1].reshape(1, -1), seq)
    return xf.reshape(bsz, seq, d)
```

```python
import functools
import math

import jax
import jax.numpy as jnp
from jax import lax
from jax.experimental import pallas as pl
from jax.experimental.pallas import tpu as pltpu

F32 = jnp.float32
BF16 = jnp.bfloat16

LANES = 128
HEAD_DIM = 128
ROPE_THETA = 10000.0
NEG_INF = -1e30
LN_EPS = 1e-5
SUBLN_EPS = 1e-5
DIL_PAIRS = ((128, 1), (512, 4), (2048, 16))
A_HEADS_PER_GROUP = 6
A_GROUPS = len(DIL_PAIRS)
A_HEADS = A_GROUPS * A_HEADS_PER_GROUP
B_HEADS = 8
B_DIM = 64
C_Q_HEADS = 8
C_KV_HEADS = 2
C_HALF_WINDOW = 128
D_HEADS = 8
GRID_W = 64
NA_WIN_H = 8
NA_WIN_W = 16
NA_Q_ROWS = 8
NA_K_ROWS = 16
NA_BLOCKS_PER_STEP = 8
DIFF_Q_ROWS = 1024
DIFF_SUB_ROWS = 256
DIFF_Q_SCALE = (B_DIM ** -0.5) * math.log2(math.e)
N_BRANCHES = 4
N_EXPERTS = 16
EC_CAPACITY_FACTOR = 2
EXPERT_PAD = 128
EXPERT_SLOT_CHUNKS = 2

A_W = A_HEADS * HEAD_DIM
B_W = B_HEADS * 2 * B_DIM
C_Q_W = C_Q_HEADS * HEAD_DIM
C_KV_W = C_KV_HEADS * HEAD_DIM
D_W = D_HEADS * HEAD_DIM
A_OUT_W = A_HEADS_PER_GROUP * HEAD_DIM

VMEM_LIMIT = 56 * 1024 * 1024


def _params(*sem):
    return pltpu.CompilerParams(dimension_semantics=sem, vmem_limit_bytes=VMEM_LIMIT)


def _dot_nt(a, b):
    return lax.dot_general(a, b, (((1,), (1,)), ((), ())), preferred_element_type=F32)


def _cast_weight_tile(w_ref, wb_ref):
    @pl.when(pl.program_id(1) == 0)
    def _():
        wb_ref[...] = w_ref[...].astype(BF16)
    return wb_ref[...]


def _proj_body(mode, x_ref, w_ref, *rest):
    rest, wb_ref = rest[:-1], rest[-1]
    acc = jnp.dot(x_ref[...], _cast_weight_tile(w_ref, wb_ref), preferred_element_type=F32)
    if mode == "plain":
        (o_ref,) = rest
        o_ref[...] = acc.astype(o_ref.dtype)
    elif mode == "gate":
        b_ref, o_ref = rest
        z = acc + b_ref[...]
        o_ref[...] = (1.0 / (1.0 + jnp.exp(-z))).astype(o_ref.dtype)
    else:
        cos_ref, sin_ref, o_ref = rest
        cos = cos_ref[...]
        sin = sin_ref[...]
        if mode == "rope64":
            qs = jnp.where(pl.program_id(0) == 0, DIFF_Q_SCALE, 1.0).astype(F32)
            cos = cos * qs
            sin = sin * qs
        lane = lax.broadcasted_iota(jnp.int32, cos.shape, 1)
        for c in range(acc.shape[1] // LANES):
            a = acc[:, c * LANES:(c + 1) * LANES]
            if mode == "rope128":
                rot = pltpu.roll(a, 64, 1)
            else:
                rot = jnp.where((lane % 64) < 32, pltpu.roll(a, 96, 1), pltpu.roll(a, 32, 1))
            o_ref[:, c * LANES:(c + 1) * LANES] = (a * cos + rot * sin).astype(o_ref.dtype)


def _weight_tile_spec(k, tn, layer, lane_tile_of):
    return pl.BlockSpec((pl.Squeezed(), pl.Element(k), pl.Element(tn)),
                        lambda n, m: (layer, 0, lane_tile_of(n) * LANES))


def _proj(xb, w_stack, layer, col0, n_tot, mode, tn, seq, extra=()):
    m_tot, k = xb.shape
    tm = min(1024, seq)
    pos_blocks = seq // tm
    in_specs = [pl.BlockSpec((tm, k), lambda n, m: (m, 0)),
                _weight_tile_spec(k, tn, layer, lambda n: col0 // LANES + n * (tn // LANES))]
    if mode == "gate":
        in_specs.append(pl.BlockSpec((1, tn), lambda n, m: (0, n)))
    elif mode != "plain":
        in_specs += [pl.BlockSpec((tm, LANES), lambda n, m: (m % pos_blocks, 0))] * 2
    return pl.pallas_call(
        functools.partial(_proj_body, mode),
        grid=(n_tot // tn, m_tot // tm),
        in_specs=in_specs,
        out_specs=pl.BlockSpec((tm, tn), lambda n, m: (m, n)),
        out_shape=jax.ShapeDtypeStruct((m_tot, n_tot), BF16),
        scratch_shapes=[pltpu.VMEM((k, tn), BF16)],
        compiler_params=_params("parallel", "arbitrary"),
        name="proj_" + mode,
    )(xb, w_stack, *extra)


def _proj_dilated_body(dil, x_ref, w_ref, cos_ref, sin_ref, o_ref, wb_ref, *scratch):
    n = pl.program_id(0)
    tm = x_ref.shape[0]
    rows = tm // dil
    acc = jnp.dot(x_ref[...], _cast_weight_tile(w_ref, wb_ref), preferred_element_type=F32)
    dst = scratch[0] if dil > 1 else None

    def emit(c, val):
        if dil > 1:
            dst[c] = val
        else:
            o_ref[0, :, c * LANES:(c + 1) * LANES] = val.astype(o_ref.dtype)

    @pl.when(n < 2)
    def _():
        cos = cos_ref[...]
        sin = sin_ref[...]
        for c in range(acc.shape[1] // LANES):
            a = acc[:, c * LANES:(c + 1) * LANES]
            emit(c, a * cos + pltpu.roll(a, 64, 1) * sin)

    @pl.when(n >= 2)
    def _():
        for c in range(acc.shape[1] // LANES):
            emit(c, acc[:, c * LANES:(c + 1) * LANES])

    if dil > 1:
        for c in range(acc.shape[1] // LANES):
            for j in range(dil):
                o_ref[j, :, c * LANES:(c + 1) * LANES] = (
                    dst.at[c][pl.ds(j, rows, stride=dil), :].astype(o_ref.dtype))


def _proj_dilated(xb, w_stack, layer, group, dil, bsz, seq, cos, sin):
    m_tot, k = xb.shape
    tm = min(1024, seq)
    tn = A_OUT_W
    pos_blocks = seq // tm
    scratch = [pltpu.VMEM((k, tn), BF16)]
    if dil > 1:
        scratch.append(pltpu.VMEM((tn // LANES, tm, LANES), F32))
    return pl.pallas_call(
        functools.partial(_proj_dilated_body, dil),
        grid=(3, m_tot // tm),
        in_specs=[pl.BlockSpec((tm, k), lambda n, m: (m, 0)),
                  _weight_tile_spec(k, tn, layer, lambda n: n * (A_W // LANES) + group * (tn // LANES)),
                  pl.BlockSpec((tm, LANES), lambda n, m: (m % pos_blocks, 0)),
                  pl.BlockSpec((tm, LANES), lambda n, m: (m % pos_blocks, 0))],
        out_specs=pl.BlockSpec((None, dil, tm // dil, tn), lambda n, m: (m // pos_blocks, 0, m % pos_blocks, n)),
        out_shape=jax.ShapeDtypeStruct((bsz, dil, seq // dil, 3 * tn), BF16),
        scratch_shapes=scratch,
        compiler_params=_params("parallel", "arbitrary"),
        name="proj_dilated",
    )(xb, w_stack, cos, sin)


def _rope_tables(seq, half):
    inv = ROPE_THETA ** (-jnp.arange(half, dtype=F32) / half)
    ang = jnp.arange(seq, dtype=jnp.int32).astype(F32)[:, None] * inv[None, :]
    cos, sin = jnp.cos(ang), jnp.sin(ang)
    reps = LANES // (2 * half)
    return (jnp.tile(jnp.concatenate([cos, cos], axis=1), (1, reps)),
            jnp.tile(jnp.concatenate([-sin, sin], axis=1), (1, reps)))


def _banded_body(hq, group, halo, tq, win, length, has_sink, want_lse, *refs):
    refs = list(refs)
    q_ref, k_ref, v_ref = refs[:3]
    pos = 3
    sink_ref = None
    if has_sink:
        sink_ref = refs[pos]
        pos += 1
    o_ref = refs[pos]
    lse_ref = refs[pos + 1] if want_lse else None

    i = pl.program_id(2)
    start = pl.multiple_of(jnp.clip(i * tq - halo, 0, length - win), 64)
    qpos = i * tq + lax.broadcasted_iota(jnp.int32, (tq, win), 0)
    kpos = start + lax.broadcasted_iota(jnp.int32, (tq, win), 1)
    valid = jnp.abs(qpos - kpos) <= halo
    scale = HEAD_DIM ** -0.5
    lane = lax.broadcasted_iota(jnp.int32, (tq, LANES), 1)
    lse_acc = jnp.zeros((tq, LANES), F32)
    for h in range(hq):
        kh = h // group
        q = q_ref[:, h * HEAD_DIM:(h + 1) * HEAD_DIM]
        k = k_ref[pl.ds(start, win), kh * HEAD_DIM:(kh + 1) * HEAD_DIM]
        v = v_ref[pl.ds(start, win), kh * HEAD_DIM:(kh + 1) * HEAD_DIM]
        s = _dot_nt(q, k) * scale
        s = jnp.where(valid, s, NEG_INF)
        m = jnp.max(s, axis=-1, keepdims=True)
        if has_sink:
            sk = sink_ref[h]
            m = jnp.maximum(m, sk)
        p = jnp.exp(s - m)
        d = jnp.sum(p, axis=-1, keepdims=True)
        if has_sink:
            d = d + jnp.exp(sk - m)
        o = jnp.dot(p.astype(BF16), v, preferred_element_type=F32) / d
        o_ref[:, h * HEAD_DIM:(h + 1) * HEAD_DIM] = o.astype(o_ref.dtype)
        if want_lse:
            lse_acc = jnp.where(lane == h, m + jnp.log(d), lse_acc)
    if want_lse:
        lse_ref[...] = lse_acc


def _banded(q4, k4, v4, qcol, kcol, vcol, *, hq, group, halo, sink=None, want_lse=False):
    bsz, dil, length, _ = q4.shape
    hkv = hq // group
    tq = min(256, length)
    win = min(length, tq + 2 * halo)
    qw, kw = hq * HEAD_DIM, hkv * HEAD_DIM
    in_specs = [pl.BlockSpec((None, None, tq, qw), lambda b, j, i: (b, j, i, qcol)),
                pl.BlockSpec((None, None, length, kw), lambda b, j, i: (b, j, 0, kcol)),
                pl.BlockSpec((None, None, length, kw), lambda b, j, i: (b, j, 0, vcol))]
    args = [q4, k4, v4]
    if sink is not None:
        in_specs.append(pl.BlockSpec(memory_space=pltpu.SMEM))
        args.append(sink)
    out_specs = [pl.BlockSpec((None, None, tq, qw), lambda b, j, i: (b, j, i, 0))]
    out_shape = [jax.ShapeDtypeStruct((bsz, dil, length, qw), BF16)]
    if want_lse:
        out_specs.append(pl.BlockSpec((None, None, tq, LANES), lambda b, j, i: (b, j, i, 0)))
        out_shape.append(jax.ShapeDtypeStruct((bsz, dil, length, LANES), F32))
    return pl.pallas_call(
        functools.partial(_banded_body, hq, group, halo, tq, win, length, sink is not None, want_lse),
        grid=(bsz, dil, length // tq),
        in_specs=in_specs,
        out_specs=out_specs,
        out_shape=out_shape,
        compiler_params=_params("parallel", "parallel", "parallel"),
        name="banded_attention",
    )(*args)


def _group_mix_body(dils, *refs):
    n = len(dils)
    o_refs, l_refs, out_ref = refs[:n], refs[n:2 * n], refs[2 * n]
    scratch = list(refs[2 * n + 1:])
    tm = out_ref.shape[0]
    heads, lses = [], []
    for g, dil in enumerate(dils):
        if dil == 1:
            heads.append(functools.partial(
                lambda h, ref: ref[0, :, h * HEAD_DIM:(h + 1) * HEAD_DIM].astype(F32), ref=o_refs[g]))
            lses.append(l_refs[g][0])
            continue
        o_scr, l_scr = scratch.pop(0), scratch.pop(0)
        rows = tm // dil
        for j in range(dil):
            l_scr[pl.ds(j, rows, stride=dil), :] = l_refs[g][j]
            for h in range(A_HEADS_PER_GROUP):
                o_scr.at[h][pl.ds(j, rows, stride=dil), :] = (
                    o_refs[g][j, :, h * HEAD_DIM:(h + 1) * HEAD_DIM].astype(F32))
        heads.append(functools.partial(lambda h, ref: ref[h], ref=o_scr))
        lses.append(l_scr[...])
    m = functools.reduce(jnp.maximum, lses)
    es = [jnp.exp(l - m) for l in lses]
    tot = functools.reduce(lambda a, b: a + b, es)
    ws = [e / tot for e in es]
    for h in range(A_HEADS_PER_GROUP):
        acc = ws[0][:, h:h + 1] * heads[0](h)
        for g in range(1, n):
            acc = acc + ws[g][:, h:h + 1] * heads[g](h)
        out_ref[:, h * HEAD_DIM:(h + 1) * HEAD_DIM] = acc.astype(out_ref.dtype)


def _group_mix(outs, lses, seq):
    bsz = outs[0].shape[0]
    dils = tuple(o.shape[1] for o in outs)
    tm = min(512, seq)
    pos_blocks = seq // tm

    def spec(dil, width):
        return pl.BlockSpec((None, dil, tm // dil, width), lambda m: (m // pos_blocks, 0, m % pos_blocks, 0))

    scratch = []
    for dil in dils:
        if dil > 1:
            scratch += [pltpu.VMEM((A_HEADS_PER_GROUP, tm, HEAD_DIM), F32), pltpu.VMEM((tm, LANES), F32)]
    return pl.pallas_call(
        functools.partial(_group_mix_body, dils),
        grid=(bsz * pos_blocks,),
        in_specs=[spec(dil, A_OUT_W) for dil in dils] + [spec(dil, LANES) for dil in dils],
        out_specs=pl.BlockSpec((tm, A_OUT_W), lambda m: (m, 0)),
        out_shape=jax.ShapeDtypeStruct((bsz * seq, A_OUT_W), BF16),
        scratch_shapes=scratch,
        compiler_params=_params("parallel"),
        name="dilation_group_mix",
    )(*outs, *lses)


def _diff_body(lambda_init, lam_ref, subln_ref, q_ref, k_ref, v_ref, o_ref):
    lam = lam_ref[...]
    dot1 = jnp.sum(lam[0:1, :] * lam[1:2, :], axis=1, keepdims=True)
    dot2 = jnp.sum(lam[2:3, :] * lam[3:4, :], axis=1, keepdims=True)
    lmbda = jnp.exp(dot1) - jnp.exp(dot2) + lambda_init
    k = k_ref[...]
    v = v_ref[...]
    sub = min(DIFF_SUB_ROWS, q_ref.shape[0])
    lane = lax.broadcasted_iota(jnp.int32, (sub, 2 * B_DIM), 1)
    zero = jnp.zeros((sub, 2 * B_DIM), BF16)

    def softmax_map_times_v(qm):
        s = _dot_nt(qm, k)
        p = jnp.exp2(s - jnp.max(s, axis=-1, keepdims=True))
        return jnp.dot(p.astype(BF16), v, preferred_element_type=F32) / jnp.sum(p, axis=-1, keepdims=True)

    for r0 in range(0, q_ref.shape[0], sub):
        q = q_ref[r0:r0 + sub, :]
        o = (softmax_map_times_v(jnp.where(lane < B_DIM, q, zero))
             - lmbda * softmax_map_times_v(jnp.where(lane >= B_DIM, q, zero)))
        o = o * lax.rsqrt(jnp.mean(o * o, axis=-1, keepdims=True) + SUBLN_EPS) * subln_ref[...]
        o_ref[r0:r0 + sub, :] = (o * (1.0 - lambda_init)).astype(o_ref.dtype)


def _diff_attention(qk3, v3, lam, subln, lambda_init):
    bsz, seq, _ = qk3.shape
    tq = min(DIFF_Q_ROWS, seq)
    hw = 2 * B_DIM
    return pl.pallas_call(
        functools.partial(_diff_body, lambda_init),
        grid=(bsz, B_HEADS, seq // tq),
        in_specs=[pl.BlockSpec((4, B_DIM), lambda b, h, i: (0, 0)),
                  pl.BlockSpec((1, hw), lambda b, h, i: (0, 0)),
                  pl.BlockSpec((None, tq, hw), lambda b, h, i: (b, i, h)),
                  pl.BlockSpec((None, seq, hw), lambda b, h, i: (b, 0, B_HEADS + h)),
                  pl.BlockSpec((None, seq, hw), lambda b, h, i: (b, 0, h))],
        out_specs=pl.BlockSpec((None, tq, hw), lambda b, h, i: (b, i, h)),
        out_shape=jax.ShapeDtypeStruct((bsz, seq, B_W), BF16),
        compiler_params=_params("parallel", "parallel", "parallel"),
        name="diff_attention",
    )(lam, subln, qk3, qk3, v3)


def _na_body(rows, k_rows, blocks, q_ref, k_ref, v_ref, bias_ref, o_ref):
    n_rb = rows // NA_Q_ROWS
    tq = NA_Q_ROWS * GRID_W
    for u in range(blocks):
        rb = pl.program_id(2) * blocks + u
        w0 = jnp.clip(rb * NA_Q_ROWS - NA_WIN_H // 2, 0, rows - k_rows) * GRID_W
        w0 = pl.multiple_of(w0, 256)
        kind = jnp.where(rb == 0, 0, jnp.where(rb == n_rb - 1, 2, 1))
        k = k_ref[pl.ds(w0, k_rows * GRID_W), :]
        v = v_ref[pl.ds(w0, k_rows * GRID_W), :]
        s = _dot_nt(q_ref[u * tq:(u + 1) * tq, :], k) * (HEAD_DIM ** -0.5) + bias_ref[kind]
        p = jnp.exp(s - jnp.max(s, axis=-1, keepdims=True))
        d = jnp.sum(p, axis=-1, keepdims=True)
        o = jnp.dot(p.astype(BF16), v, preferred_element_type=F32) / d
        o_ref[u * tq:(u + 1) * tq, :] = o.astype(o_ref.dtype)


def _na_bias_tables(rpb, rows):
    kh = min(NA_WIN_H, rows)
    k_rows = min(NA_K_ROWS, rows)
    n_rb = rows // NA_Q_ROWS
    n_heads = rpb.shape[0]
    edge = GRID_W - NA_WIN_W
    ext = jnp.pad(rpb.astype(F32), ((0, 0), (0, 0), (edge, edge)), mode="edge")
    toeplitz = jnp.stack([ext[:, :, GRID_W - 1 - qc:2 * GRID_W - 1 - qc] for qc in range(GRID_W)], axis=2)
    c = jnp.arange(GRID_W)
    col_start = jnp.clip(c - NA_WIN_W // 2, 0, GRID_W - NA_WIN_W)
    col_ok = (c[None, :] >= col_start[:, None]) & (c[None, :] < col_start[:, None] + NA_WIN_W)
    toeplitz = jnp.where(col_ok[None, None], toeplitz, NEG_INF)
    tabs = []
    for rb in (0, min(1, n_rb - 1), n_rb - 1):
        r0 = rb * NA_Q_ROWS
        w0 = min(max(r0 - NA_WIN_H // 2, 0), rows - k_rows)
        strips = []
        for a in range(NA_Q_ROWS):
            qrow = r0 + a
            start = min(max(qrow - kh // 2, 0), rows - kh)
            first = start - qrow + (NA_WIN_H - 1)
            blk = toeplitz[:, first:first + kh].transpose(0, 2, 1, 3)
            blk = jnp.pad(blk, ((0, 0), (0, 0), (start - w0, k_rows - kh - (start - w0)), (0, 0)),
                          constant_values=NEG_INF)
            strips.append(blk.reshape(n_heads, GRID_W, k_rows * GRID_W))
        tabs.append(jnp.concatenate(strips, axis=1))
    return jnp.stack(tabs, axis=1)


def _neighborhood_attention(qkv3, rpb):
    bsz, seq, _ = qkv3.shape
    rows = seq // GRID_W
    k_rows = min(NA_K_ROWS, rows)
    tq = NA_Q_ROWS * GRID_W
    blocks = min(NA_BLOCKS_PER_STEP, rows // NA_Q_ROWS)
    step_rows = blocks * tq
    bias = _na_bias_tables(rpb, rows)
    return pl.pallas_call(
        functools.partial(_na_body, rows, k_rows, blocks),
        grid=(bsz, D_HEADS, seq // step_rows),
        in_specs=[pl.BlockSpec((None, step_rows, HEAD_DIM), lambda b, h, r: (b, r, h)),
                  pl.BlockSpec((None, seq, HEAD_DIM), lambda b, h, r: (b, 0, D_HEADS + h)),
                  pl.BlockSpec((None, seq, HEAD_DIM), lambda b, h, r: (b, 0, 2 * D_HEADS + h)),
                  pl.BlockSpec((None, 3, tq, k_rows * GRID_W), lambda b, h, r: (h, 0, 0, 0))],
        out_specs=pl.BlockSpec((None, step_rows, HEAD_DIM), lambda b, h, r: (b, r, h)),
        out_shape=jax.ShapeDtypeStruct((bsz, seq, D_W), BF16),
        compiler_params=_params("parallel", "parallel", "parallel"),
        name="neighborhood_attention",
    )(qkv3, qkv3, qkv3, bias)


def _merge_body(oa_ref, ob_ref, oc_ref, od_ref, wa_ref, wb_ref, wc_ref, wd_ref,
                ga_ref, gb_ref, gc_ref, gd_ref, out_ref):
    acc = ga_ref[...].astype(F32) * jnp.dot(oa_ref[...], wa_ref[...], preferred_element_type=F32)
    acc = acc + gb_ref[...].astype(F32) * jnp.dot(ob_ref[...], wb_ref[...], preferred_element_type=F32)
    acc = acc + gc_ref[...].astype(F32) * jnp.dot(oc_ref[...], wc_ref[...], preferred_element_type=F32)
    acc = acc + gd_ref[...].astype(F32) * jnp.dot(od_ref[...], wd_ref[...], preferred_element_type=F32)
    out_ref[...] = acc.astype(out_ref.dtype)


def _merge(branches, weights, gates, seq):
    m_tot = branches[0].shape[0]
    d = weights[0].shape[1]
    tm = min(1024, seq)
    tn = min(512, d)
    n_blocks = d // tn
    in_specs = [pl.BlockSpec((tm, o.shape[1]), lambda m, n: (m, 0)) for o in branches]
    in_specs += [pl.BlockSpec((w.shape[0], tn), lambda m, n: (0, n)) for w in weights]
    in_specs += [pl.BlockSpec((tm, tn), functools.partial(lambda m, n, i: (m, i * n_blocks + n), i=i))
                 for i in range(N_BRANCHES)]
    return pl.pallas_call(
        _merge_body,
        grid=(m_tot // tm, n_blocks),
        in_specs=in_specs,
        out_specs=pl.BlockSpec((tm, tn), lambda m, n: (m, n)),
        out_shape=jax.ShapeDtypeStruct((m_tot, d), BF16),
        compiler_params=_params("parallel", "parallel"),
        name="branch_merge",
    )(*branches, *weights, *([gates] * N_BRANCHES))


def _layer_norm_rows(y, gain, bias):
    mu = jnp.mean(y, axis=-1, keepdims=True)
    var = jnp.mean(jnp.square(y - mu), axis=-1, keepdims=True)
    return (y - mu) * lax.rsqrt(var + LN_EPS) * gain + bias


def _out_proj_body(alpha, mixed_ref, w_ref, x_ref, gain_ref, bias_ref, xb_ref, state_ref):
    d = x_ref.shape[1]
    y = alpha * x_ref[...] + jnp.dot(mixed_ref[...], w_ref[...], preferred_element_type=F32)
    out = _layer_norm_rows(y, gain_ref[...], bias_ref[...])
    xb_ref[...] = out.astype(BF16)
    state_ref[:, 0:d] = out
    state_ref[:, d:2 * d] = alpha * out
    state_ref[:, 2 * d:] = jnp.zeros((out.shape[0], EXPERT_PAD), F32)


def _out_proj(mixed, w, x, gain, bias, alpha, seq):
    m_tot, d = x.shape
    tm = min(512, seq)
    row = pl.BlockSpec((tm, d), lambda m: (m, 0))
    vec = pl.BlockSpec((1, d), lambda m: (0, 0))
    w_spec = pl.BlockSpec((d, d), lambda m: (0, 0), pipeline_mode=pl.Buffered(1))
    return pl.pallas_call(
        functools.partial(_out_proj_body, alpha),
        grid=(m_tot // tm,),
        in_specs=[row, w_spec, row, vec, vec],
        out_specs=[row, pl.BlockSpec((tm, 2 * d + EXPERT_PAD), lambda m: (m, 0))],
        out_shape=[jax.ShapeDtypeStruct((m_tot, d), BF16),
                   jax.ShapeDtypeStruct((m_tot, 2 * d + EXPERT_PAD), F32)],
        compiler_params=_params("parallel"),
        name="out_proj_layernorm",
    )(mixed, w, x, gain, bias)


def _router_body(cap, x_ref, wrt_ref, tok_ref, state_in_hbm, idx_ref, gate_ref):
    del state_in_hbm
    seq = x_ref.shape[0]
    logits = _dot_nt(wrt_ref[...], x_ref[...])
    e = jnp.exp(logits - jnp.max(logits, axis=0, keepdims=True))
    aff = e / jnp.sum(e, axis=0, keepdims=True)
    bits = lax.bitcast_convert_type(aff, jnp.int32)
    thr = jnp.zeros((N_EXPERTS, 1), jnp.int32)
    for bit in range(30, -1, -1):
        cand = thr | (1 << bit)
        cnt = jnp.sum((bits >= cand).astype(F32), axis=1, keepdims=True)
        thr = jnp.where(cnt >= cap, cand, thr)
    above = bits > thr
    tied = bits == thr
    need = cap - jnp.sum(above.astype(F32), axis=1, keepdims=True)

    chunk = min(512, seq)
    upper = (lax.broadcasted_iota(jnp.int32, (chunk, chunk), 0)
             <= lax.broadcasted_iota(jnp.int32, (chunk, chunk), 1)).astype(BF16)

    def prefix_count(mask):
        parts, carry = [], jnp.zeros((N_EXPERTS, 1), F32)
        for c in range(seq // chunk):
            part = jnp.dot(mask[:, c * chunk:(c + 1) * chunk].astype(BF16), upper,
                           preferred_element_type=F32) + carry
            parts.append(part)
            carry = part[:, chunk - 1:chunk]
        return jnp.concatenate(parts, axis=1)

    sel = above | (tied & (prefix_count(tied) <= need))
    slot = jnp.where(sel, prefix_count(sel) - 1.0, -1.0)
    gate = jnp.where(sel, aff, 0.0)
    pad = EXPERT_PAD - N_EXPERTS
    gate_ref[...] = jnp.concatenate([gate, jnp.zeros((pad, seq), F32)], axis=0).T

    slot_i = slot.astype(jnp.int32)
    ck = min(1024, seq)
    slot_iota = lax.broadcasted_iota(jnp.int32, (cap, ck), 0)
    lane = lax.broadcasted_iota(jnp.int32, (cap, EXPERT_PAD), 1)
    idx_cols = jnp.zeros((cap, EXPERT_PAD), F32)
    for e in range(N_EXPERTS):
        r = jnp.zeros((cap, LANES), F32)
        for c in range(seq // ck):
            onehot = (slot_iota == slot_i[e:e + 1, c * ck:(c + 1) * ck]).astype(BF16)
            r = r + jnp.dot(onehot, tok_ref[c * ck:(c + 1) * ck, :], preferred_element_type=F32)
        idx_cols = jnp.where(lane == e, r[:, 0:1] * 64.0 + r[:, 1:2], idx_cols)
    idx_ref[...] = idx_cols.T[0:N_EXPERTS, :].astype(jnp.int32) + pl.program_id(0) * seq


def _router(xb3, wrt, state, cap):
    bsz, seq, d = xb3.shape
    tok = jnp.arange(seq, dtype=jnp.int32)
    tok_tab = jnp.zeros((seq, LANES), F32).at[:, 0].set((tok // 64).astype(F32)).at[:, 1].set((tok % 64).astype(F32))
    return pl.pallas_call(
        functools.partial(_router_body, cap),
        grid=(bsz,),
        in_specs=[pl.BlockSpec((None, seq, d), lambda b: (b, 0, 0)),
                  pl.BlockSpec((N_EXPERTS, d), lambda b: (0, 0)),
                  pl.BlockSpec((seq, LANES), lambda b: (0, 0)),
                  pl.BlockSpec(memory_space=pl.ANY)],
        out_specs=[pl.BlockSpec((None, N_EXPERTS, cap), lambda b: (b, 0, 0)),
                   pl.BlockSpec((seq, EXPERT_PAD), lambda b: (b, 2 * d // EXPERT_PAD))],
        out_shape=[jax.ShapeDtypeStruct((bsz, N_EXPERTS, cap), jnp.int32),
                   jax.ShapeDtypeStruct(state.shape, F32)],
        input_output_aliases={3: 1},
        compiler_params=_params("parallel"),
        name="router_topk",
    )(xb3, wrt, tok_tab.astype(BF16), state)


def _expert_body(d, idx_ref, state_in_hbm, wg_ref, wu_ref, wd_ref, state_hbm, rows_buf, sems):
    del state_in_hbm
    e = pl.program_id(0)
    n_chunks, rows, _ = rows_buf.shape

    def for_each_slot(fn):
        for p in range(rows):
            fn(p)

    def gather(c, p):
        row = idx_ref[0, c * rows + p]
        return pltpu.make_async_copy(state_hbm.at[pl.ds(row, 1), :], rows_buf.at[c].at[pl.ds(p, 1), :],
                                     sems.at[c, 0])

    def scatter(c, p):
        row = idx_ref[0, c * rows + p]
        return pltpu.make_async_copy(rows_buf.at[c].at[pl.ds(p, 1), pl.ds(d, d)],
                                     state_hbm.at[pl.ds(row, 1), pl.ds(d, d)], sems.at[c, 1])

    for c in range(n_chunks):
        for_each_slot(lambda p, c=c: gather(c, p).start())
    for c in range(n_chunks):
        for_each_slot(lambda p, c=c: gather(c, p).wait())
        x = rows_buf[c, :, 0:d].astype(BF16)
        g = jnp.dot(x, wg_ref[...], preferred_element_type=F32)
        u = jnp.dot(x, wu_ref[...], preferred_element_type=F32)
        h = (g / (1.0 + jnp.exp(-g))) * u
        y = jnp.dot(h.astype(BF16), wd_ref[...], preferred_element_type=F32)
        lane = lax.broadcasted_iota(jnp.int32, (rows, EXPERT_PAD), 1)
        gates = rows_buf[c, :, 2 * d:2 * d + EXPERT_PAD]
        gate_col = jnp.sum(jnp.where(lane == e, gates, 0.0), axis=1, keepdims=True)
        rows_buf[c, :, d:2 * d] = rows_buf[c, :, d:2 * d] + gate_col * y
        for_each_slot(lambda p, c=c: scatter(c, p).start())
    for c in range(n_chunks):
        for_each_slot(lambda p, c=c: scatter(c, p).wait())


def _expert_ffn(idx, state, layer, wg, wu, wd):
    bsz, n_exp, cap = idx.shape
    m_tot, width = state.shape
    d = (width - EXPERT_PAD) // 2
    f = wg.shape[3]
    n_chunks = EXPERT_SLOT_CHUNKS
    any_spec = pl.BlockSpec(memory_space=pl.ANY)
    return pl.pallas_call(
        functools.partial(_expert_body, d),
        grid=(n_exp, bsz),
        in_specs=[pl.BlockSpec((None, 1, cap), lambda e, b: (b * n_exp + e, 0, 0), memory_space=pltpu.SMEM),
                  any_spec,
                  pl.BlockSpec((None, None, d, f), lambda e, b: (layer, e, 0, 0)),
                  pl.BlockSpec((None, None, d, f), lambda e, b: (layer, e, 0, 0)),
                  pl.BlockSpec((None, None, f, d), lambda e, b: (layer, e, 0, 0))],
        out_specs=any_spec,
        out_shape=jax.ShapeDtypeStruct((m_tot, width), F32),
        scratch_shapes=[pltpu.VMEM((n_chunks, cap // n_chunks, width), F32),
                        pltpu.SemaphoreType.DMA((n_chunks, 2))],
        input_output_aliases={1: 0},
        compiler_params=_params("arbitrary", "arbitrary"),
        name="expert_swiglu_scatter",
    )(idx.reshape(bsz * n_exp, 1, cap), state, wg, wu, wd)


def _final_norm_body(acc_ref, gain_ref, bias_ref, xo_ref, xb_ref):
    out = _layer_norm_rows(acc_ref[...], gain_ref[...], bias_ref[...])
    xo_ref[...] = out
    xb_ref[...] = out.astype(BF16)


def _final_norm(state, gain, bias, seq):
    m_tot = state.shape[0]
    d = gain.shape[1]
    tm = min(512, seq)
    row = pl.BlockSpec((tm, d), lambda m: (m, 0))
    vec = pl.BlockSpec((1, d), lambda m: (0, 0))
    return pl.pallas_call(
        _final_norm_body,
        grid=(m_tot // tm,),
        in_specs=[pl.BlockSpec((tm, d), lambda m: (m, 1)), vec, vec],
        out_specs=[row, row],
        out_shape=[jax.ShapeDtypeStruct((m_tot, d), F32), jax.ShapeDtypeStruct((m_tot, d), BF16)],
        compiler_params=_params("parallel"),
        name="channel_mixer_layernorm",
    )(state, gain, bias)


def _offsets(widths):
    offs = [0]
    for w in widths:
        offs.append(offs[-1] + w)
    return offs


def kernel(x, w_in, b_gate, w_branch, w_out, diff_lambda, diff_subln, sink_logit, na_rpb,
           w_router, w_exp_gate, w_exp_up, w_exp_down, ln_gain, ln_bias):
    bsz, seq, d = x.shape
    depth = w_in.shape[0]
    m_tot = bsz * seq
    alpha = (2.0 * depth) ** 0.25
    cap = EC_CAPACITY_FACTOR * seq // N_EXPERTS
    in_widths = (A_W, A_W, A_W, B_W, B_W, B_W, C_Q_W, C_KV_W, C_KV_W, D_W, D_W, D_W, N_BRANCHES * d)
    io = _offsets(in_widths)
    bo = _offsets((A_OUT_W, B_W, C_Q_W, D_W))
    cos128, sin128 = _rope_tables(seq, HEAD_DIM // 2)
    cos64, sin64 = _rope_tables(seq, B_DIM // 2)

    wg_stack, wu_stack, wd_stack = (w.astype(BF16) for w in (w_exp_gate, w_exp_up, w_exp_down))
    xf = x.reshape(m_tot, d)
    xb = xf.astype(BF16)
    for l in range(depth):
        lambda_init = 0.8 - 0.6 * math.exp(-0.3 * l)
        def proj(a, b, mode, tn, extra=()):
            return _proj(xb, w_in, l, io[a], io[b] - io[a], mode, tn, seq, extra)

        qk_b = proj(3, 5, "rope64", B_W, (cos64, sin64))
        v_b = proj(5, 6, "plain", B_W)
        qk_c = proj(6, 8, "rope128", C_Q_W + C_KV_W, (cos128, sin128))
        v_c = proj(8, 9, "plain", C_KV_W)
        qkv_d = proj(9, 12, "plain", D_W)
        gates = proj(12, 13, "gate", min(1024, d), (b_gate[l].reshape(1, -1),))

        outs, lses = [], []
        for g, (w, r) in enumerate(DIL_PAIRS):
            qkv_g = _proj_dilated(xb, w_in, l, g, r, bsz, seq, cos128, sin128)
            o_g, lse_g = _banded(qkv_g, qkv_g, qkv_g, 0, 1, 2, hq=A_HEADS_PER_GROUP, group=1,
                                 halo=(w // 2) // r, want_lse=True)
            outs.append(o_g)
            lses.append(lse_g)
        o_a = _group_mix(outs, lses, seq)

        o_b = _diff_attention(qk_b.reshape(bsz, seq, 2 * B_W), v_b.reshape(bsz, seq, B_W),
                              diff_lambda[l], diff_subln[l].reshape(1, -1), lambda_init)

        qk_c4 = qk_c.reshape(bsz, 1, seq, C_Q_W + C_KV_W)
        (o_c,) = _banded(qk_c4, qk_c4, v_c.reshape(bsz, 1, seq, C_KV_W), 0, C_Q_W // C_KV_W, 0,
                         hq=C_Q_HEADS, group=C_Q_HEADS // C_KV_HEADS, halo=C_HALF_WINDOW, sink=sink_logit[l])

        o_d = _neighborhood_attention(qkv_d.reshape(bsz, seq, 3 * D_W), na_rpb[l])

        wb = w_branch[l].astype(BF16)
        mixed = _merge([o_a, o_b.reshape(m_tot, B_W), o_c.reshape(m_tot, C_Q_W), o_d.reshape(m_tot, D_W)],
                       [wb[bo[i]:bo[i + 1]] for i in range(N_BRANCHES)], gates, seq)
        xb, state = _out_proj(mixed, w_out[l].astype(BF16), xf, ln_gain[l, 0].reshape(1, -1),
                              ln_bias[l, 0].reshape(1, -1), alpha, seq)

        idx, state = _router(xb.reshape(bsz, seq, d), w_router[l].T.astype(BF16), state, cap)
        state = _expert_ffn(idx, state, l, wg_stack, wu_stack, wd_stack)
        xf, xb = _final_norm(state, ln_gain[l, 1].reshape(1, -1), ln_bias[l, 1].reshape(1, -1), seq)
    return xf.reshape(bsz, seq, d)
```

```python
import functools
import math

import jax
import jax.numpy as jnp
from jax import lax
from jax.experimental import pallas as pl
from jax.experimental.pallas import tpu as pltpu

F32 = jnp.float32
BF16 = jnp.bfloat16

LANES = 128
HEAD_DIM = 128
ROPE_THETA = 10000.0
NEG_INF = -1e30
LN_EPS = 1e-5
SUBLN_EPS = 1e-5
DIL_PAIRS = ((128, 1), (512, 4), (2048, 16))
A_HEADS_PER_GROUP = 6
A_GROUPS = len(DIL_PAIRS)
A_HEADS = A_GROUPS * A_HEADS_PER_GROUP
B_HEADS = 8
B_DIM = 64
C_Q_HEADS = 8
C_KV_HEADS = 2
C_HALF_WINDOW = 128
D_HEADS = 8
GRID_W = 64
NA_WIN_H = 8
NA_WIN_W = 16
BANDED_TILES_PER_STEP = 2
NA_Q_ROWS = 8
NA_K_ROWS = 16
NA_BLOCKS_PER_STEP = 8
DIFF_Q_ROWS = 1024
DIFF_SUB_ROWS = 256
LOG2_E = math.log2(math.e)
LN_2 = math.log(2.0)
DIFF_Q_SCALE = (B_DIM ** -0.5) * LOG2_E
ATTN_Q_SCALE = (HEAD_DIM ** -0.5) * LOG2_E
N_BRANCHES = 4
N_EXPERTS = 16
EC_CAPACITY_FACTOR = 2
EXPERT_PAD = 128
EXPERT_SLOT_CHUNKS = 2

A_W = A_HEADS * HEAD_DIM
B_W = B_HEADS * 2 * B_DIM
C_Q_W = C_Q_HEADS * HEAD_DIM
C_KV_W = C_KV_HEADS * HEAD_DIM
D_W = D_HEADS * HEAD_DIM
A_OUT_W = A_HEADS_PER_GROUP * HEAD_DIM

VMEM_LIMIT = 56 * 1024 * 1024


def _params(*sem):
    return pltpu.CompilerParams(dimension_semantics=sem, vmem_limit_bytes=VMEM_LIMIT)


def _dot_nt(a, b):
    return lax.dot_general(a, b, (((1,), (1,)), ((), ())), preferred_element_type=F32)


def _cast_weight_tile(w_ref, wb_ref):
    @pl.when(pl.program_id(1) == 0)
    def _():
        wb_ref[...] = w_ref[...].astype(BF16)
    return wb_ref[...]


def _proj_body(mode, q_cols, q_scale, x_ref, w_ref, *rest):
    rest, wb_ref = rest[:-1], rest[-1]
    acc = jnp.dot(x_ref[...], _cast_weight_tile(w_ref, wb_ref), preferred_element_type=F32)
    tn = acc.shape[1]

    def chunk_scale(c):
        return jnp.where(pl.program_id(0) * tn + c * LANES < q_cols, q_scale, 1.0).astype(F32)

    if mode == "plain":
        (o_ref,) = rest
        if q_cols:
            for c in range(tn // LANES):
                cs = slice(c * LANES, (c + 1) * LANES)
                o_ref[:, cs] = (acc[:, cs] * chunk_scale(c)).astype(o_ref.dtype)
        else:
            o_ref[...] = acc.astype(o_ref.dtype)
    elif mode == "gate":
        b_ref, o_ref = rest
        z = acc + b_ref[...]
        o_ref[...] = (1.0 / (1.0 + jnp.exp(-z))).astype(o_ref.dtype)
    else:
        cos_ref, sin_ref, o_ref = rest
        cos = cos_ref[...]
        sin = sin_ref[...]
        lane = lax.broadcasted_iota(jnp.int32, cos.shape, 1)
        for c in range(tn // LANES):
            a = acc[:, c * LANES:(c + 1) * LANES]
            if mode == "rope128":
                rot = pltpu.roll(a, 64, 1)
            else:
                rot = jnp.where((lane % 64) < 32, pltpu.roll(a, 96, 1), pltpu.roll(a, 32, 1))
            qs = chunk_scale(c)
            o_ref[:, c * LANES:(c + 1) * LANES] = (a * (cos * qs) + rot * (sin * qs)).astype(o_ref.dtype)


def _weight_tile_spec(k, tn, layer, lane_tile_of):
    return pl.BlockSpec((pl.Squeezed(), pl.Element(k), pl.Element(tn)),
                        lambda n, m: (layer, 0, lane_tile_of(n) * LANES))


def _proj(xb, w_stack, layer, col0, n_tot, mode, tn, seq, extra=(), q_cols=0, q_scale=1.0):
    m_tot, k = xb.shape
    tm = min(1024, seq)
    pos_blocks = seq // tm
    in_specs = [pl.BlockSpec((tm, k), lambda n, m: (m, 0)),
                _weight_tile_spec(k, tn, layer, lambda n: col0 // LANES + n * (tn // LANES))]
    if mode == "gate":
        in_specs.append(pl.BlockSpec((1, tn), lambda n, m: (0, n)))
    elif mode != "plain":
        in_specs += [pl.BlockSpec((tm, LANES), lambda n, m: (m % pos_blocks, 0))] * 2
    return pl.pallas_call(
        functools.partial(_proj_body, mode, q_cols, q_scale),
        grid=(n_tot // tn, m_tot // tm),
        in_specs=in_specs,
        out_specs=pl.BlockSpec((tm, tn), lambda n, m: (m, n)),
        out_shape=jax.ShapeDtypeStruct((m_tot, n_tot), BF16),
        scratch_shapes=[pltpu.VMEM((k, tn), BF16)],
        compiler_params=_params("parallel", "arbitrary"),
        name="proj_" + mode,
    )(xb, w_stack, *extra)


def _proj_dilated_body(dil, x_ref, w_ref, cos_ref, sin_ref, o_ref, wb_ref, *scratch):
    n = pl.program_id(0)
    tm = x_ref.shape[0]
    rows = tm // dil
    acc = jnp.dot(x_ref[...], _cast_weight_tile(w_ref, wb_ref), preferred_element_type=F32)
    dst = scratch[0] if dil > 1 else None

    def emit(c, val):
        if dil > 1:
            dst[c] = val
        else:
            o_ref[0, :, c * LANES:(c + 1) * LANES] = val.astype(o_ref.dtype)

    @pl.when(n < 2)
    def _():
        qs = jnp.where(n == 0, ATTN_Q_SCALE, 1.0).astype(F32)
        cos = cos_ref[...] * qs
        sin = sin_ref[...] * qs
        for c in range(acc.shape[1] // LANES):
            a = acc[:, c * LANES:(c + 1) * LANES]
            emit(c, a * cos + pltpu.roll(a, 64, 1) * sin)

    @pl.when(n >= 2)
    def _():
        for c in range(acc.shape[1] // LANES):
            emit(c, acc[:, c * LANES:(c + 1) * LANES])

    if dil > 1:
        for c in range(acc.shape[1] // LANES):
            for j in range(dil):
                o_ref[j, :, c * LANES:(c + 1) * LANES] = (
                    dst.at[c][pl.ds(j, rows, stride=dil), :].astype(o_ref.dtype))


def _proj_dilated(xb, w_stack, layer, group, dil, bsz, seq, cos, sin):
    m_tot, k = xb.shape
    tm = min(1024, seq)
    tn = A_OUT_W
    pos_blocks = seq // tm
    scratch = [pltpu.VMEM((k, tn), BF16)]
    if dil > 1:
        scratch.append(pltpu.VMEM((tn // LANES, tm, LANES), F32))
    return pl.pallas_call(
        functools.partial(_proj_dilated_body, dil),
        grid=(3, m_tot // tm),
        in_specs=[pl.BlockSpec((tm, k), lambda n, m: (m, 0)),
                  _weight_tile_spec(k, tn, layer, lambda n: n * (A_W // LANES) + group * (tn // LANES)),
                  pl.BlockSpec((tm, LANES), lambda n, m: (m % pos_blocks, 0)),
                  pl.BlockSpec((tm, LANES), lambda n, m: (m % pos_blocks, 0))],
        out_specs=pl.BlockSpec((None, dil, tm // dil, tn), lambda n, m: (m // pos_blocks, 0, m % pos_blocks, n)),
        out_shape=jax.ShapeDtypeStruct((bsz, dil, seq // dil, 3 * tn), BF16),
        scratch_shapes=scratch,
        compiler_params=_params("parallel", "arbitrary"),
        name="proj_dilated",
    )(xb, w_stack, cos, sin)


def _rope_tables(seq, half):
    inv = ROPE_THETA ** (-jnp.arange(half, dtype=F32) / half)
    ang = jnp.arange(seq, dtype=jnp.int32).astype(F32)[:, None] * inv[None, :]
    cos, sin = jnp.cos(ang), jnp.sin(ang)
    reps = LANES // (2 * half)
    return (jnp.tile(jnp.concatenate([cos, cos], axis=1), (1, reps)),
            jnp.tile(jnp.concatenate([-sin, sin], axis=1), (1, reps)))


def _banded_body(hq, group, halo, tq, win, length, has_sink, want_lse, *refs):
    refs = list(refs)
    q_ref, k_ref, v_ref = refs[:3]
    pos = 3
    sink_ref = None
    if has_sink:
        sink_ref = refs[pos]
        pos += 1
    o_ref = refs[pos]
    lse_ref = refs[pos + 1] if want_lse else None

    lane = lax.broadcasted_iota(jnp.int32, (tq, LANES), 1)
    for u in range(q_ref.shape[0] // tq):
        i = pl.program_id(2) * (q_ref.shape[0] // tq) + u
        rows = slice(u * tq, (u + 1) * tq)
        start = pl.multiple_of(jnp.clip(i * tq - halo, 0, length - win), 64)
        qpos = i * tq + lax.broadcasted_iota(jnp.int32, (tq, win), 0)
        kpos = start + lax.broadcasted_iota(jnp.int32, (tq, win), 1)
        valid = jnp.abs(qpos - kpos) <= halo
        lse_acc = jnp.zeros((tq, LANES), F32)
        for h in range(hq):
            kh = h // group
            q = q_ref[rows, h * HEAD_DIM:(h + 1) * HEAD_DIM]
            k = k_ref[pl.ds(start, win), kh * HEAD_DIM:(kh + 1) * HEAD_DIM]
            v = v_ref[pl.ds(start, win), kh * HEAD_DIM:(kh + 1) * HEAD_DIM]
            s = jnp.where(valid, _dot_nt(q, k), NEG_INF)
            m = jnp.max(s, axis=-1, keepdims=True)
            if has_sink:
                sk = sink_ref[h] * LOG2_E
                m = jnp.maximum(m, sk)
            p = jnp.exp2(s - m)
            d = jnp.sum(p, axis=-1, keepdims=True)
            if has_sink:
                d = d + jnp.exp2(sk - m)
            o = jnp.dot(p.astype(BF16), v, preferred_element_type=F32) / d
            o_ref[rows, h * HEAD_DIM:(h + 1) * HEAD_DIM] = o.astype(o_ref.dtype)
            if want_lse:
                lse_acc = jnp.where(lane == h, (m + jnp.log2(d)) * LN_2, lse_acc)
        if want_lse:
            lse_ref[rows, :] = lse_acc


def _banded(q4, k4, v4, qcol, kcol, vcol, *, hq, group, halo, sink=None, want_lse=False):
    bsz, dil, length, _ = q4.shape
    hkv = hq // group
    tq = min(256, length)
    win = min(length, tq + 2 * halo)
    qw, kw = hq * HEAD_DIM, hkv * HEAD_DIM
    step_rows = tq * min(BANDED_TILES_PER_STEP, length // tq)
    in_specs = [pl.BlockSpec((None, None, step_rows, qw), lambda b, j, i: (b, j, i, qcol)),
                pl.BlockSpec((None, None, length, kw), lambda b, j, i: (b, j, 0, kcol)),
                pl.BlockSpec((None, None, length, kw), lambda b, j, i: (b, j, 0, vcol))]
    args = [q4, k4, v4]
    if sink is not None:
        in_specs.append(pl.BlockSpec(memory_space=pltpu.SMEM))
        args.append(sink)
    out_specs = [pl.BlockSpec((None, None, step_rows, qw), lambda b, j, i: (b, j, i, 0))]
    out_shape = [jax.ShapeDtypeStruct((bsz, dil, length, qw), BF16)]
    if want_lse:
        out_specs.append(pl.BlockSpec((None, None, step_rows, LANES), lambda b, j, i: (b, j, i, 0)))
        out_shape.append(jax.ShapeDtypeStruct((bsz, dil, length, LANES), F32))
    return pl.pallas_call(
        functools.partial(_banded_body, hq, group, halo, tq, win, length, sink is not None, want_lse),
        grid=(bsz, dil, length // step_rows),
        in_specs=in_specs,
        out_specs=out_specs,
        out_shape=out_shape,
        compiler_params=_params("parallel", "parallel", "parallel"),
        name="banded_attention",
    )(*args)


def _group_mix_body(dils, *refs):
    n = len(dils)
    o_refs, l_refs, out_ref = refs[:n], refs[n:2 * n], refs[2 * n]
    scratch = list(refs[2 * n + 1:])
    tm = out_ref.shape[0]
    heads, lses = [], []
    for g, dil in enumerate(dils):
        if dil == 1:
            heads.append(functools.partial(
                lambda h, ref: ref[0, :, h * HEAD_DIM:(h + 1) * HEAD_DIM].astype(F32), ref=o_refs[g]))
            lses.append(l_refs[g][0])
            continue
        o_scr, l_scr = scratch.pop(0), scratch.pop(0)
        rows = tm // dil
        for j in range(dil):
            l_scr[pl.ds(j, rows, stride=dil), :] = l_refs[g][j]
            for h in range(A_HEADS_PER_GROUP):
                o_scr.at[h][pl.ds(j, rows, stride=dil), :] = (
                    o_refs[g][j, :, h * HEAD_DIM:(h + 1) * HEAD_DIM].astype(F32))
        heads.append(functools.partial(lambda h, ref: ref[h], ref=o_scr))
        lses.append(l_scr[...])
    m = functools.reduce(jnp.maximum, lses)
    es = [jnp.exp(l - m) for l in lses]
    tot = functools.reduce(lambda a, b: a + b, es)
    ws = [e / tot for e in es]
    for h in range(A_HEADS_PER_GROUP):
        acc = ws[0][:, h:h + 1] * heads[0](h)
        for g in range(1, n):
            acc = acc + ws[g][:, h:h + 1] * heads[g](h)
        out_ref[:, h * HEAD_DIM:(h + 1) * HEAD_DIM] = acc.astype(out_ref.dtype)


def _group_mix(outs, lses, seq):
    bsz = outs[0].shape[0]
    dils = tuple(o.shape[1] for o in outs)
    tm = min(512, seq)
    pos_blocks = seq // tm

    def spec(dil, width):
        return pl.BlockSpec((None, dil, tm // dil, width), lambda m: (m // pos_blocks, 0, m % pos_blocks, 0))

    scratch = []
    for dil in dils:
        if dil > 1:
            scratch += [pltpu.VMEM((A_HEADS_PER_GROUP, tm, HEAD_DIM), F32), pltpu.VMEM((tm, LANES), F32)]
    return pl.pallas_call(
        functools.partial(_group_mix_body, dils),
        grid=(bsz * pos_blocks,),
        in_specs=[spec(dil, A_OUT_W) for dil in dils] + [spec(dil, LANES) for dil in dils],
        out_specs=pl.BlockSpec((tm, A_OUT_W), lambda m: (m, 0)),
        out_shape=jax.ShapeDtypeStruct((bsz * seq, A_OUT_W), BF16),
        scratch_shapes=scratch,
        compiler_params=_params("parallel"),
        name="dilation_group_mix",
    )(*outs, *lses)


def _diff_body(lambda_init, lam_ref, subln_ref, q_ref, k_ref, v_ref, o_ref):
    lam = lam_ref[...]
    dot1 = jnp.sum(lam[0:1, :] * lam[1:2, :], axis=1, keepdims=True)
    dot2 = jnp.sum(lam[2:3, :] * lam[3:4, :], axis=1, keepdims=True)
    lmbda = jnp.exp(dot1) - jnp.exp(dot2) + lambda_init
    k = k_ref[...]
    v = v_ref[...]
    sub = min(DIFF_SUB_ROWS, q_ref.shape[0])
    lane = lax.broadcasted_iota(jnp.int32, (sub, 2 * B_DIM), 1)
    zero = jnp.zeros((sub, 2 * B_DIM), BF16)

    def softmax_map_times_v(qm):
        s = _dot_nt(qm, k)
        p = jnp.exp2(s - jnp.max(s, axis=-1, keepdims=True))
        return jnp.dot(p.astype(BF16), v, preferred_element_type=F32) / jnp.sum(p, axis=-1, keepdims=True)

    for r0 in range(0, q_ref.shape[0], sub):
        q = q_ref[r0:r0 + sub, :]
        o = (softmax_map_times_v(jnp.where(lane < B_DIM, q, zero))
             - lmbda * softmax_map_times_v(jnp.where(lane >= B_DIM, q, zero)))
        o = o * lax.rsqrt(jnp.mean(o * o, axis=-1, keepdims=True) + SUBLN_EPS) * subln_ref[...]
        o_ref[r0:r0 + sub, :] = (o * (1.0 - lambda_init)).astype(o_ref.dtype)


def _diff_attention(qk3, v3, lam, subln, lambda_init):
    bsz, seq, _ = qk3.shape
    tq = min(DIFF_Q_ROWS, seq)
    hw = 2 * B_DIM
    return pl.pallas_call(
        functools.partial(_diff_body, lambda_init),
        grid=(bsz, B_HEADS, seq // tq),
        in_specs=[pl.BlockSpec((4, B_DIM), lambda b, h, i: (0, 0)),
                  pl.BlockSpec((1, hw), lambda b, h, i: (0, 0)),
                  pl.BlockSpec((None, tq, hw), lambda b, h, i: (b, i, h)),
                  pl.BlockSpec((None, seq, hw), lambda b, h, i: (b, 0, B_HEADS + h)),
                  pl.BlockSpec((None, seq, hw), lambda b, h, i: (b, 0, h))],
        out_specs=pl.BlockSpec((None, tq, hw), lambda b, h, i: (b, i, h)),
        out_shape=jax.ShapeDtypeStruct((bsz, seq, B_W), BF16),
        compiler_params=_params("parallel", "parallel", "parallel"),
        name="diff_attention",
    )(lam, subln, qk3, qk3, v3)


def _na_body(rows, k_rows, blocks, q_ref, k_ref, v_ref, bias_ref, o_ref):
    n_rb = rows // NA_Q_ROWS
    tq = NA_Q_ROWS * GRID_W
    for u in range(blocks):
        rb = pl.program_id(2) * blocks + u
        w0 = jnp.clip(rb * NA_Q_ROWS - NA_WIN_H // 2, 0, rows - k_rows) * GRID_W
        w0 = pl.multiple_of(w0, 256)
        kind = jnp.where(rb == 0, 0, jnp.where(rb == n_rb - 1, 2, 1))
        k = k_ref[pl.ds(w0, k_rows * GRID_W), :]
        v = v_ref[pl.ds(w0, k_rows * GRID_W), :]
        s = _dot_nt(q_ref[u * tq:(u + 1) * tq, :], k) + bias_ref[kind]
        p = jnp.exp2(s - jnp.max(s, axis=-1, keepdims=True))
        d = jnp.sum(p, axis=-1, keepdims=True)
        o = jnp.dot(p.astype(BF16), v, preferred_element_type=F32) / d
        o_ref[u * tq:(u + 1) * tq, :] = o.astype(o_ref.dtype)


def _na_bias_tables(rpb, rows):
    kh = min(NA_WIN_H, rows)
    k_rows = min(NA_K_ROWS, rows)
    n_rb = rows // NA_Q_ROWS
    n_heads = rpb.shape[0]
    edge = GRID_W - NA_WIN_W
    ext = jnp.pad(rpb.astype(F32) * LOG2_E, ((0, 0), (0, 0), (edge, edge)), mode="edge")
    toeplitz = jnp.stack([ext[:, :, GRID_W - 1 - qc:2 * GRID_W - 1 - qc] for qc in range(GRID_W)], axis=2)
    c = jnp.arange(GRID_W)
    col_start = jnp.clip(c - NA_WIN_W // 2, 0, GRID_W - NA_WIN_W)
    col_ok = (c[None, :] >= col_start[:, None]) & (c[None, :] < col_start[:, None] + NA_WIN_W)
    toeplitz = jnp.where(col_ok[None, None], toeplitz, NEG_INF)
    tabs = []
    for rb in (0, min(1, n_rb - 1), n_rb - 1):
        r0 = rb * NA_Q_ROWS
        w0 = min(max(r0 - NA_WIN_H // 2, 0), rows - k_rows)
        strips = []
        for a in range(NA_Q_ROWS):
            qrow = r0 + a
            start = min(max(qrow - kh // 2, 0), rows - kh)
            first = start - qrow + (NA_WIN_H - 1)
            blk = toeplitz[:, first:first + kh].transpose(0, 2, 1, 3)
            blk = jnp.pad(blk, ((0, 0), (0, 0), (start - w0, k_rows - kh - (start - w0)), (0, 0)),
                          constant_values=NEG_INF)
            strips.append(blk.reshape(n_heads, GRID_W, k_rows * GRID_W))
        tabs.append(jnp.concatenate(strips, axis=1))
    return jnp.stack(tabs, axis=1)


def _neighborhood_attention(qkv3, rpb):
    bsz, seq, _ = qkv3.shape
    rows = seq // GRID_W
    k_rows = min(NA_K_ROWS, rows)
    tq = NA_Q_ROWS * GRID_W
    blocks = min(NA_BLOCKS_PER_STEP, rows // NA_Q_ROWS)
    step_rows = blocks * tq
    bias = _na_bias_tables(rpb, rows)
    return pl.pallas_call(
        functools.partial(_na_body, rows, k_rows, blocks),
        grid=(bsz, D_HEADS, seq // step_rows),
        in_specs=[pl.BlockSpec((None, step_rows, HEAD_DIM), lambda b, h, r: (b, r, h)),
                  pl.BlockSpec((None, seq, HEAD_DIM), lambda b, h, r: (b, 0, D_HEADS + h)),
                  pl.BlockSpec((None, seq, HEAD_DIM), lambda b, h, r: (b, 0, 2 * D_HEADS + h)),
                  pl.BlockSpec((None, 3, tq, k_rows * GRID_W), lambda b, h, r: (h, 0, 0, 0))],
        out_specs=pl.BlockSpec((None, step_rows, HEAD_DIM), lambda b, h, r: (b, r, h)),
        out_shape=jax.ShapeDtypeStruct((bsz, seq, D_W), BF16),
        compiler_params=_params("parallel", "parallel", "parallel"),
        name="neighborhood_attention",
    )(qkv3, qkv3, qkv3, bias)


def _merge_body(oa_ref, ob_ref, oc_ref, od_ref, wa_ref, wb_ref, wc_ref, wd_ref,
                ga_ref, gb_ref, gc_ref, gd_ref, out_ref):
    acc = ga_ref[...].astype(F32) * jnp.dot(oa_ref[...], wa_ref[...], preferred_element_type=F32)
    acc = acc + gb_ref[...].astype(F32) * jnp.dot(ob_ref[...], wb_ref[...], preferred_element_type=F32)
    acc = acc + gc_ref[...].astype(F32) * jnp.dot(oc_ref[...], wc_ref[...], preferred_element_type=F32)
    acc = acc + gd_ref[...].astype(F32) * jnp.dot(od_ref[...], wd_ref[...], preferred_element_type=F32)
    out_ref[...] = acc.astype(out_ref.dtype)


def _merge(branches, weights, gates, seq):
    m_tot = branches[0].shape[0]
    d = weights[0].shape[1]
    tm = min(1024, seq)
    tn = min(512, d)
    n_blocks = d // tn
    in_specs = [pl.BlockSpec((tm, o.shape[1]), lambda m, n: (m, 0)) for o in branches]
    in_specs += [pl.BlockSpec((w.shape[0], tn), lambda m, n: (0, n)) for w in weights]
    in_specs += [pl.BlockSpec((tm, tn), functools.partial(lambda m, n, i: (m, i * n_blocks + n), i=i))
                 for i in range(N_BRANCHES)]
    return pl.pallas_call(
        _merge_body,
        grid=(m_tot // tm, n_blocks),
        in_specs=in_specs,
        out_specs=pl.BlockSpec((tm, tn), lambda m, n: (m, n)),
        out_shape=jax.ShapeDtypeStruct((m_tot, d), BF16),
        compiler_params=_params("parallel", "parallel"),
        name="branch_merge",
    )(*branches, *weights, *([gates] * N_BRANCHES))


def _layer_norm_rows(y, gain, bias):
    mu = jnp.mean(y, axis=-1, keepdims=True)
    var = jnp.mean(jnp.square(y - mu), axis=-1, keepdims=True)
    return (y - mu) * lax.rsqrt(var + LN_EPS) * gain + bias


def _out_proj_body(alpha, mixed_ref, w_ref, x_ref, gain_ref, bias_ref, xb_ref, state_ref):
    d = x_ref.shape[1]
    y = alpha * x_ref[...] + jnp.dot(mixed_ref[...], w_ref[...], preferred_element_type=F32)
    out = _layer_norm_rows(y, gain_ref[...], bias_ref[...])
    xb_ref[...] = out.astype(BF16)
    state_ref[:, 0:d] = out
    state_ref[:, d:2 * d] = alpha * out
    state_ref[:, 2 * d:] = jnp.zeros((out.shape[0], EXPERT_PAD), F32)


def _out_proj(mixed, w, x, gain, bias, alpha, seq):
    m_tot, d = x.shape
    tm = min(512, seq)
    row = pl.BlockSpec((tm, d), lambda m: (m, 0))
    vec = pl.BlockSpec((1, d), lambda m: (0, 0))
    w_spec = pl.BlockSpec((d, d), lambda m: (0, 0), pipeline_mode=pl.Buffered(1))
    return pl.pallas_call(
        functools.partial(_out_proj_body, alpha),
        grid=(m_tot // tm,),
        in_specs=[row, w_spec, row, vec, vec],
        out_specs=[row, pl.BlockSpec((tm, 2 * d + EXPERT_PAD), lambda m: (m, 0))],
        out_shape=[jax.ShapeDtypeStruct((m_tot, d), BF16),
                   jax.ShapeDtypeStruct((m_tot, 2 * d + EXPERT_PAD), F32)],
        compiler_params=_params("parallel"),
        name="out_proj_layernorm",
    )(mixed, w, x, gain, bias)


def _router_body(cap, x_ref, wrt_ref, tok_ref, state_in_hbm, idx_ref, gate_ref):
    del state_in_hbm
    seq = x_ref.shape[0]
    logits = _dot_nt(wrt_ref[...], x_ref[...])
    e = jnp.exp(logits - jnp.max(logits, axis=0, keepdims=True))
    aff = e / jnp.sum(e, axis=0, keepdims=True)
    bits = lax.bitcast_convert_type(aff, jnp.int32)
    thr = jnp.zeros((N_EXPERTS, 1), jnp.int32)
    for bit in range(30, -1, -1):
        cand = thr | (1 << bit)
        cnt = jnp.sum((bits >= cand).astype(F32), axis=1, keepdims=True)
        thr = jnp.where(cnt >= cap, cand, thr)
    above = bits > thr
    tied = bits == thr
    need = cap - jnp.sum(above.astype(F32), axis=1, keepdims=True)

    chunk = min(512, seq)
    upper = (lax.broadcasted_iota(jnp.int32, (chunk, chunk), 0)
             <= lax.broadcasted_iota(jnp.int32, (chunk, chunk), 1)).astype(BF16)

    def prefix_count(mask):
        parts, carry = [], jnp.zeros((N_EXPERTS, 1), F32)
        for c in range(seq // chunk):
            part = jnp.dot(mask[:, c * chunk:(c + 1) * chunk].astype(BF16), upper,
                           preferred_element_type=F32) + carry
            parts.append(part)
            carry = part[:, chunk - 1:chunk]
        return jnp.concatenate(parts, axis=1)

    sel = above | (tied & (prefix_count(tied) <= need))
    slot = jnp.where(sel, prefix_count(sel) - 1.0, -1.0)
    gate = jnp.where(sel, aff, 0.0)
    pad = EXPERT_PAD - N_EXPERTS
    gate_ref[...] = jnp.concatenate([gate, jnp.zeros((pad, seq), F32)], axis=0).T

    slot_i = slot.astype(jnp.int32)
    ck = min(1024, seq)
    slot_iota = lax.broadcasted_iota(jnp.int32, (cap, ck), 0)
    lane = lax.broadcasted_iota(jnp.int32, (cap, EXPERT_PAD), 1)
    idx_cols = jnp.zeros((cap, EXPERT_PAD), F32)
    for e in range(N_EXPERTS):
        r = jnp.zeros((cap, LANES), F32)
        for c in range(seq // ck):
            onehot = (slot_iota == slot_i[e:e + 1, c * ck:(c + 1) * ck]).astype(BF16)
            r = r + jnp.dot(onehot, tok_ref[c * ck:(c + 1) * ck, :], preferred_element_type=F32)
        idx_cols = jnp.where(lane == e, r[:, 0:1] * 64.0 + r[:, 1:2], idx_cols)
    idx_ref[...] = idx_cols.T[0:N_EXPERTS, :].astype(jnp.int32) + pl.program_id(0) * seq


def _router(xb3, wrt, state, cap):
    bsz, seq, d = xb3.shape
    tok = jnp.arange(seq, dtype=jnp.int32)
    tok_tab = jnp.zeros((seq, LANES), F32).at[:, 0].set((tok // 64).astype(F32)).at[:, 1].set((tok % 64).astype(F32))
    return pl.pallas_call(
        functools.partial(_router_body, cap),
        grid=(bsz,),
        in_specs=[pl.BlockSpec((None, seq, d), lambda b: (b, 0, 0)),
                  pl.BlockSpec((N_EXPERTS, d), lambda b: (0, 0)),
                  pl.BlockSpec((seq, LANES), lambda b: (0, 0)),
                  pl.BlockSpec(memory_space=pl.ANY)],
        out_specs=[pl.BlockSpec((None, N_EXPERTS, cap), lambda b: (b, 0, 0)),
                   pl.BlockSpec((seq, EXPERT_PAD), lambda b: (b, 2 * d // EXPERT_PAD))],
        out_shape=[jax.ShapeDtypeStruct((bsz, N_EXPERTS, cap), jnp.int32),
                   jax.ShapeDtypeStruct(state.shape, F32)],
        input_output_aliases={3: 1},
        compiler_params=_params("parallel"),
        name="router_topk",
    )(xb3, wrt, tok_tab.astype(BF16), state)


def _expert_body(d, idx_ref, state_in_hbm, wg_ref, wu_ref, wd_ref, state_hbm, rows_buf, sems):
    del state_in_hbm
    e = pl.program_id(0)
    n_chunks, rows, _ = rows_buf.shape

    def for_each_slot(fn):
        for p in range(rows):
            fn(p)

    def gather(c, p):
        row = idx_ref[0, c * rows + p]
        return pltpu.make_async_copy(state_hbm.at[pl.ds(row, 1), :], rows_buf.at[c].at[pl.ds(p, 1), :],
                                     sems.at[c, 0])

    def scatter(c, p):
        row = idx_ref[0, c * rows + p]
        return pltpu.make_async_copy(rows_buf.at[c].at[pl.ds(p, 1), pl.ds(d, d)],
                                     state_hbm.at[pl.ds(row, 1), pl.ds(d, d)], sems.at[c, 1])

    for c in range(n_chunks):
        for_each_slot(lambda p, c=c: gather(c, p).start(priority=p % 2))
    for c in range(n_chunks):
        for_each_slot(lambda p, c=c: gather(c, p).wait())
        x = rows_buf[c, :, 0:d].astype(BF16)
        g = jnp.dot(x, wg_ref[...], preferred_element_type=F32)
        u = jnp.dot(x, wu_ref[...], preferred_element_type=F32)
        h = (g / (1.0 + jnp.exp(-g))) * u
        y = jnp.dot(h.astype(BF16), wd_ref[...], preferred_element_type=F32)
        lane = lax.broadcasted_iota(jnp.int32, (rows, EXPERT_PAD), 1)
        gates = rows_buf[c, :, 2 * d:2 * d + EXPERT_PAD]
        gate_col = jnp.sum(jnp.where(lane == e, gates, 0.0), axis=1, keepdims=True)
        rows_buf[c, :, d:2 * d] = rows_buf[c, :, d:2 * d] + gate_col * y
        for_each_slot(lambda p, c=c: scatter(c, p).start(priority=p % 2))
    for c in range(n_chunks):
        for_each_slot(lambda p, c=c: scatter(c, p).wait())


def _expert_ffn(idx, state, layer, wg, wu, wd):
    bsz, n_exp, cap = idx.shape
    m_tot, width = state.shape
    d = (width - EXPERT_PAD) // 2
    f = wg.shape[3]
    n_chunks = EXPERT_SLOT_CHUNKS
    any_spec = pl.BlockSpec(memory_space=pl.ANY)
    return pl.pallas_call(
        functools.partial(_expert_body, d),
        grid=(n_exp, bsz),
        in_specs=[pl.BlockSpec((None, 1, cap), lambda e, b: (b * n_exp + e, 0, 0), memory_space=pltpu.SMEM),
                  any_spec,
                  pl.BlockSpec((None, None, d, f), lambda e, b: (layer, e, 0, 0)),
                  pl.BlockSpec((None, None, d, f), lambda e, b: (layer, e, 0, 0)),
                  pl.BlockSpec((None, None, f, d), lambda e, b: (layer, e, 0, 0))],
        out_specs=any_spec,
        out_shape=jax.ShapeDtypeStruct((m_tot, width), F32),
        scratch_shapes=[pltpu.VMEM((n_chunks, cap // n_chunks, width), F32),
                        pltpu.SemaphoreType.DMA((n_chunks, 2))],
        input_output_aliases={1: 0},
        compiler_params=_params("arbitrary", "arbitrary"),
        name="expert_swiglu_scatter",
    )(idx.reshape(bsz * n_exp, 1, cap), state, wg, wu, wd)


def _final_norm_body(acc_ref, gain_ref, bias_ref, xo_ref, xb_ref):
    out = _layer_norm_rows(acc_ref[...], gain_ref[...], bias_ref[...])
    xo_ref[...] = out
    xb_ref[...] = out.astype(BF16)


def _final_norm(state, gain, bias, seq):
    m_tot = state.shape[0]
    d = gain.shape[1]
    tm = min(512, seq)
    row = pl.BlockSpec((tm, d), lambda m: (m, 0))
    vec = pl.BlockSpec((1, d), lambda m: (0, 0))
    return pl.pallas_call(
        _final_norm_body,
        grid=(m_tot // tm,),
        in_specs=[pl.BlockSpec((tm, d), lambda m: (m, 1)), vec, vec],
        out_specs=[row, row],
        out_shape=[jax.ShapeDtypeStruct((m_tot, d), F32), jax.ShapeDtypeStruct((m_tot, d), BF16)],
        compiler_params=_params("parallel"),
        name="channel_mixer_layernorm",
    )(state, gain, bias)


def _offsets(widths):
    offs = [0]
    for w in widths:
        offs.append(offs[-1] + w)
    return offs


def kernel(x, w_in, b_gate, w_branch, w_out, diff_lambda, diff_subln, sink_logit, na_rpb,
           w_router, w_exp_gate, w_exp_up, w_exp_down, ln_gain, ln_bias):
    bsz, seq, d = x.shape
    depth = w_in.shape[0]
    m_tot = bsz * seq
    alpha = (2.0 * depth) ** 0.25
    cap = EC_CAPACITY_FACTOR * seq // N_EXPERTS
    in_widths = (A_W, A_W, A_W, B_W, B_W, B_W, C_Q_W, C_KV_W, C_KV_W, D_W, D_W, D_W, N_BRANCHES * d)
    io = _offsets(in_widths)
    bo = _offsets((A_OUT_W, B_W, C_Q_W, D_W))
    cos128, sin128 = _rope_tables(seq, HEAD_DIM // 2)
    cos64, sin64 = _rope_tables(seq, B_DIM // 2)

    wg_stack, wu_stack, wd_stack = (w.astype(BF16) for w in (w_exp_gate, w_exp_up, w_exp_down))
    xf = x.reshape(m_tot, d)
    xb = xf.astype(BF16)
    for l in range(depth):
        lambda_init = 0.8 - 0.6 * math.exp(-0.3 * l)
        def proj(a, b, mode, tn, extra=(), **q_fold):
            return _proj(xb, w_in, l, io[a], io[b] - io[a], mode, tn, seq, extra, **q_fold)

        qk_b = proj(3, 5, "rope64", B_W, (cos64, sin64), q_cols=B_W, q_scale=DIFF_Q_SCALE)
        v_b = proj(5, 6, "plain", B_W)
        qk_c = proj(6, 8, "rope128", C_Q_W + C_KV_W, (cos128, sin128), q_cols=C_Q_W, q_scale=ATTN_Q_SCALE)
        v_c = proj(8, 9, "plain", C_KV_W)
        qkv_d = proj(9, 12, "plain", D_W, q_cols=D_W, q_scale=ATTN_Q_SCALE)
        gates = proj(12, 13, "gate", min(1024, d), (b_gate[l].reshape(1, -1),))

        outs, lses = [], []
        for g, (w, r) in enumerate(DIL_PAIRS):
            qkv_g = _proj_dilated(xb, w_in, l, g, r, bsz, seq, cos128, sin128)
            o_g, lse_g = _banded(qkv_g, qkv_g, qkv_g, 0, 1, 2, hq=A_HEADS_PER_GROUP, group=1,
                                 halo=(w // 2) // r, want_lse=True)
            outs.append(o_g)
            lses.append(lse_g)
        o_a = _group_mix(outs, lses, seq)

        o_b = _diff_attention(qk_b.reshape(bsz, seq, 2 * B_W), v_b.reshape(bsz, seq, B_W),
                              diff_lambda[l], diff_subln[l].reshape(1, -1), lambda_init)

        qk_c4 = qk_c.reshape(bsz, 1, seq, C_Q_W + C_KV_W)
        (o_c,) = _banded(qk_c4, qk_c4, v_c.reshape(bsz, 1, seq, C_KV_W), 0, C_Q_W // C_KV_W, 0,
                         hq=C_Q_HEADS, group=C_Q_HEADS // C_KV_HEADS, halo=C_HALF_WINDOW, sink=sink_logit[l])

        o_d = _neighborhood_attention(qkv_d.reshape(bsz, seq, 3 * D_W), na_rpb[l])

        wb = w_branch[l].astype(BF16)
        mixed = _merge([o_a, o_b.reshape(m_tot, B_W), o_c.reshape(m_tot, C_Q_W), o_d.reshape(m_tot, D_W)],
                       [wb[bo[i]:bo[i + 1]] for i in range(N_BRANCHES)], gates, seq)
        xb, state = _out_proj(mixed, w_out[l].astype(BF16), xf, ln_gain[l, 0].reshape(1, -1),
                              ln_bias[l, 0].reshape(1, -1), alpha, seq)

        idx, state = _router(xb.reshape(bsz, seq, d), w_router[l].T.astype(BF16), state, cap)
        state = _expert_ffn(idx, state, l, wg_stack, wu_stack, wd_stack)
        xf, xb = _final_norm(state, ln_gain[l, 1].reshape(1, -1), ln_bias[l, 1].reshape(1, -1), seq)
    return xf.reshape(bsz, seq, d)
```

```python
import functools
import math

import jax
import jax.numpy as jnp
from jax import lax
from jax.experimental import pallas as pl
from jax.experimental.pallas import tpu as pltpu

F32 = jnp.float32
BF16 = jnp.bfloat16

LANES = 128
HEAD_DIM = 128
ROPE_THETA = 10000.0
NEG_INF = -1e30
LN_EPS = 1e-5
SUBLN_EPS = 1e-5
DIL_PAIRS = ((128, 1), (512, 4), (2048, 16))
A_HEADS_PER_GROUP = 6
A_GROUPS = len(DIL_PAIRS)
A_HEADS = A_GROUPS * A_HEADS_PER_GROUP
B_HEADS = 8
B_DIM = 64
C_Q_HEADS = 8
C_KV_HEADS = 2
C_HALF_WINDOW = 128
D_HEADS = 8
GRID_W = 64
NA_WIN_H = 8
NA_WIN_W = 16
BANDED_TILES_PER_STEP = 2
NA_Q_ROWS = 8
NA_K_ROWS = 16
NA_BLOCKS_PER_STEP = 8
DIFF_Q_ROWS = 1024
DIFF_SUB_ROWS = 256
LOG2_E = math.log2(math.e)
LN_2 = math.log(2.0)
DIFF_Q_SCALE = (B_DIM ** -0.5) * LOG2_E
ATTN_Q_SCALE = (HEAD_DIM ** -0.5) * LOG2_E
N_BRANCHES = 4
N_EXPERTS = 16
EC_CAPACITY_FACTOR = 2
EXPERT_PAD = 128
EXPERT_SLOT_CHUNKS = 2

A_W = A_HEADS * HEAD_DIM
B_W = B_HEADS * 2 * B_DIM
C_Q_W = C_Q_HEADS * HEAD_DIM
C_KV_W = C_KV_HEADS * HEAD_DIM
D_W = D_HEADS * HEAD_DIM
A_OUT_W = A_HEADS_PER_GROUP * HEAD_DIM

VMEM_LIMIT = 56 * 1024 * 1024


def _params(*sem):
    return pltpu.CompilerParams(dimension_semantics=sem, vmem_limit_bytes=VMEM_LIMIT)


def _dot_nt(a, b):
    return lax.dot_general(a, b, (((1,), (1,)), ((), ())), preferred_element_type=F32)


def _cast_weight_tile(w_ref, wb_ref):
    @pl.when(pl.program_id(1) == 0)
    def _():
        wb_ref[...] = w_ref[...].astype(BF16)
    return wb_ref[...]


def _proj_body(mode, q_cols, q_scale, x_ref, w_ref, *rest):
    rest, wb_ref = rest[:-1], rest[-1]
    acc = jnp.dot(x_ref[...], _cast_weight_tile(w_ref, wb_ref), preferred_element_type=F32)
    tn = acc.shape[1]

    def chunk_scale(c):
        return jnp.where(pl.program_id(0) * tn + c * LANES < q_cols, q_scale, 1.0).astype(F32)

    if mode == "plain":
        (o_ref,) = rest
        if q_cols:
            for c in range(tn // LANES):
                cs = slice(c * LANES, (c + 1) * LANES)
                o_ref[:, cs] = (acc[:, cs] * chunk_scale(c)).astype(o_ref.dtype)
        else:
            o_ref[...] = acc.astype(o_ref.dtype)
    elif mode == "gate":
        b_ref, o_ref = rest
        z = acc + b_ref[...]
        o_ref[...] = (1.0 / (1.0 + jnp.exp(-z))).astype(o_ref.dtype)
    else:
        cos_ref, sin_ref, o_ref = rest
        cos = cos_ref[...]
        sin = sin_ref[...]
        lane = lax.broadcasted_iota(jnp.int32, cos.shape, 1)
        for c in range(tn // LANES):
            a = acc[:, c * LANES:(c + 1) * LANES]
            if mode == "rope128":
                rot = pltpu.roll(a, 64, 1)
            else:
                rot = jnp.where((lane % 64) < 32, pltpu.roll(a, 96, 1), pltpu.roll(a, 32, 1))
            qs = chunk_scale(c)
            o_ref[:, c * LANES:(c + 1) * LANES] = (a * (cos * qs) + rot * (sin * qs)).astype(o_ref.dtype)


def _weight_tile_spec(k, tn, layer, lane_tile_of):
    return pl.BlockSpec((pl.Squeezed(), pl.Element(k), pl.Element(tn)),
                        lambda n, m: (layer, 0, lane_tile_of(n) * LANES))


def _proj(xb, w_stack, layer, col0, n_tot, mode, tn, seq, extra=(), q_cols=0, q_scale=1.0):
    m_tot, k = xb.shape
    tm = min(1024, seq)
    pos_blocks = seq // tm
    in_specs = [pl.BlockSpec((tm, k), lambda n, m: (m, 0)),
                _weight_tile_spec(k, tn, layer, lambda n: col0 // LANES + n * (tn // LANES))]
    if mode == "gate":
        in_specs.append(pl.BlockSpec((1, tn), lambda n, m: (0, n)))
    elif mode != "plain":
        in_specs += [pl.BlockSpec((tm, LANES), lambda n, m: (m % pos_blocks, 0))] * 2
    return pl.pallas_call(
        functools.partial(_proj_body, mode, q_cols, q_scale),
        grid=(n_tot // tn, m_tot // tm),
        in_specs=in_specs,
        out_specs=pl.BlockSpec((tm, tn), lambda n, m: (m, n)),
        out_shape=jax.ShapeDtypeStruct((m_tot, n_tot), BF16),
        scratch_shapes=[pltpu.VMEM((k, tn), BF16)],
        compiler_params=_params("parallel", "arbitrary"),
        name="proj_" + mode,
    )(xb, w_stack, *extra)


def _proj_dilated_body(dil, x_ref, w_ref, cos_ref, sin_ref, o_ref, wb_ref, *scratch):
    n = pl.program_id(0)
    tm = x_ref.shape[0]
    rows = tm // dil
    acc = jnp.dot(x_ref[...], _cast_weight_tile(w_ref, wb_ref), preferred_element_type=F32)
    dst = scratch[0] if dil > 1 else None

    def emit(c, val):
        if dil > 1:
            dst[c] = val
        else:
            o_ref[0, :, c * LANES:(c + 1) * LANES] = val.astype(o_ref.dtype)

    @pl.when(n < 2)
    def _():
        qs = jnp.where(n == 0, ATTN_Q_SCALE, 1.0).astype(F32)
        cos = cos_ref[...] * qs
        sin = sin_ref[...] * qs
        for c in range(acc.shape[1] // LANES):
            a = acc[:, c * LANES:(c + 1) * LANES]
            emit(c, a * cos + pltpu.roll(a, 64, 1) * sin)

    @pl.when(n >= 2)
    def _():
        for c in range(acc.shape[1] // LANES):
            emit(c, acc[:, c * LANES:(c + 1) * LANES])

    if dil > 1:
        for c in range(acc.shape[1] // LANES):
            for j in range(dil):
                o_ref[j, :, c * LANES:(c + 1) * LANES] = (
                    dst.at[c][pl.ds(j, rows, stride=dil), :].astype(o_ref.dtype))


def _proj_dilated(xb, w_stack, layer, group, dil, bsz, seq, cos, sin):
    m_tot, k = xb.shape
    tm = min(1024, seq)
    tn = A_OUT_W
    pos_blocks = seq // tm
    scratch = [pltpu.VMEM((k, tn), BF16)]
    if dil > 1:
        scratch.append(pltpu.VMEM((tn // LANES, tm, LANES), F32))
    return pl.pallas_call(
        functools.partial(_proj_dilated_body, dil),
        grid=(3, m_tot // tm),
        in_specs=[pl.BlockSpec((tm, k), lambda n, m: (m, 0)),
                  _weight_tile_spec(k, tn, layer, lambda n: n * (A_W // LANES) + group * (tn // LANES)),
                  pl.BlockSpec((tm, LANES), lambda n, m: (m % pos_blocks, 0)),
                  pl.BlockSpec((tm, LANES), lambda n, m: (m % pos_blocks, 0))],
        out_specs=pl.BlockSpec((None, dil, tm // dil, tn), lambda n, m: (m // pos_blocks, 0, m % pos_blocks, n)),
        out_shape=jax.ShapeDtypeStruct((bsz, dil, seq // dil, 3 * tn), BF16),
        scratch_shapes=scratch,
        compiler_params=_params("parallel", "arbitrary"),
        name="proj_dilated",
    )(xb, w_stack, cos, sin)


def _rope_tables(seq, half):
    inv = ROPE_THETA ** (-jnp.arange(half, dtype=F32) / half)
    ang = jnp.arange(seq, dtype=jnp.int32).astype(F32)[:, None] * inv[None, :]
    cos, sin = jnp.cos(ang), jnp.sin(ang)
    reps = LANES // (2 * half)
    return (jnp.tile(jnp.concatenate([cos, cos], axis=1), (1, reps)),
            jnp.tile(jnp.concatenate([-sin, sin], axis=1), (1, reps)))


def _banded_body(hq, group, halo, tq, win, length, has_sink, want_lse, *refs):
    refs = list(refs)
    q_ref, k_ref, v_ref = refs[:3]
    pos = 3
    sink_ref = None
    if has_sink:
        sink_ref = refs[pos]
        pos += 1
    o_ref = refs[pos]
    lse_ref = refs[pos + 1] if want_lse else None

    lane = lax.broadcasted_iota(jnp.int32, (tq, LANES), 1)
    for u in range(q_ref.shape[0] // tq):
        i = pl.program_id(2) * (q_ref.shape[0] // tq) + u
        rows = slice(u * tq, (u + 1) * tq)
        start = pl.multiple_of(jnp.clip(i * tq - halo, 0, length - win), 64)
        qpos = i * tq + lax.broadcasted_iota(jnp.int32, (tq, win), 0)
        kpos = start + lax.broadcasted_iota(jnp.int32, (tq, win), 1)
        valid = jnp.abs(qpos - kpos) <= halo
        lse_acc = jnp.zeros((tq, LANES), F32)
        for h in range(hq):
            kh = h // group
            q = q_ref[rows, h * HEAD_DIM:(h + 1) * HEAD_DIM]
            k = k_ref[pl.ds(start, win), kh * HEAD_DIM:(kh + 1) * HEAD_DIM]
            v = v_ref[pl.ds(start, win), kh * HEAD_DIM:(kh + 1) * HEAD_DIM]
            s = jnp.where(valid, _dot_nt(q, k), NEG_INF)
            m = jnp.max(s, axis=-1, keepdims=True)
            if has_sink:
                sk = sink_ref[h] * LOG2_E
                m = jnp.maximum(m, sk)
            p = jnp.exp2(s - m)
            d = jnp.sum(p, axis=-1, keepdims=True)
            if has_sink:
                d = d + jnp.exp2(sk - m)
            o = jnp.dot(p.astype(BF16), v, preferred_element_type=F32) / d
            o_ref[rows, h * HEAD_DIM:(h + 1) * HEAD_DIM] = o.astype(o_ref.dtype)
            if want_lse:
                lse_acc = jnp.where(lane == h, (m + jnp.log2(d)) * LN_2, lse_acc)
        if want_lse:
            lse_ref[rows, :] = lse_acc


def _banded(q4, k4, v4, qcol, kcol, vcol, *, hq, group, halo, sink=None, want_lse=False):
    bsz, dil, length, _ = q4.shape
    hkv = hq // group
    tq = min(256, length)
    win = min(length, tq + 2 * halo)
    qw, kw = hq * HEAD_DIM, hkv * HEAD_DIM
    step_rows = tq * min(BANDED_TILES_PER_STEP, length // tq)
    in_specs = [pl.BlockSpec((None, None, step_rows, qw), lambda b, j, i: (b, j, i, qcol)),
                pl.BlockSpec((None, None, length, kw), lambda b, j, i: (b, j, 0, kcol)),
                pl.BlockSpec((None, None, length, kw), lambda b, j, i: (b, j, 0, vcol))]
    args = [q4, k4, v4]
    if sink is not None:
        in_specs.append(pl.BlockSpec(memory_space=pltpu.SMEM))
        args.append(sink)
    out_specs = [pl.BlockSpec((None, None, step_rows, qw), lambda b, j, i: (b, j, i, 0))]
    out_shape = [jax.ShapeDtypeStruct((bsz, dil, length, qw), BF16)]
    if want_lse:
        out_specs.append(pl.BlockSpec((None, None, step_rows, LANES), lambda b, j, i: (b, j, i, 0)))
        out_shape.append(jax.ShapeDtypeStruct((bsz, dil, length, LANES), F32))
    return pl.pallas_call(
        functools.partial(_banded_body, hq, group, halo, tq, win, length, sink is not None, want_lse),
        grid=(bsz, dil, length // step_rows),
        in_specs=in_specs,
        out_specs=out_specs,
        out_shape=out_shape,
        compiler_params=_params("parallel", "parallel", "parallel"),
        name="banded_attention",
    )(*args)


def _group_mix_body(dils, *refs):
    n = len(dils)
    o_refs, l_refs, out_ref = refs[:n], refs[n:2 * n], refs[2 * n]
    scratch = list(refs[2 * n + 1:])
    tm = out_ref.shape[0]
    heads, lses = [], []
    for g, dil in enumerate(dils):
        if dil == 1:
            heads.append(functools.partial(
                lambda h, ref: ref[0, :, h * HEAD_DIM:(h + 1) * HEAD_DIM].astype(F32), ref=o_refs[g]))
            lses.append(l_refs[g][0])
            continue
        o_scr, l_scr = scratch.pop(0), scratch.pop(0)
        rows = tm // dil
        for j in range(dil):
            l_scr[pl.ds(j, rows, stride=dil), :] = l_refs[g][j]
            for h in range(A_HEADS_PER_GROUP):
                o_scr.at[h][pl.ds(j, rows, stride=dil), :] = (
                    o_refs[g][j, :, h * HEAD_DIM:(h + 1) * HEAD_DIM].astype(F32))
        heads.append(functools.partial(lambda h, ref: ref[h], ref=o_scr))
        lses.append(l_scr[...])
    m = functools.reduce(jnp.maximum, lses)
    es = [jnp.exp(l - m) for l in lses]
    tot = functools.reduce(lambda a, b: a + b, es)
    ws = [e / tot for e in es]
    for h in range(A_HEADS_PER_GROUP):
        acc = ws[0][:, h:h + 1] * heads[0](h)
        for g in range(1, n):
            acc = acc + ws[g][:, h:h + 1] * heads[g](h)
        out_ref[:, h * HEAD_DIM:(h + 1) * HEAD_DIM] = acc.astype(out_ref.dtype)


def _group_mix(outs, lses, seq):
    bsz = outs[0].shape[0]
    dils = tuple(o.shape[1] for o in outs)
    tm = min(512, seq)
    pos_blocks = seq // tm

    def spec(dil, width):
        return pl.BlockSpec((None, dil, tm // dil, width), lambda m: (m // pos_blocks, 0, m % pos_blocks, 0))

    scratch = []
    for dil in dils:
        if dil > 1:
            scratch += [pltpu.VMEM((A_HEADS_PER_GROUP, tm, HEAD_DIM), F32), pltpu.VMEM((tm, LANES), F32)]
    return pl.pallas_call(
        functools.partial(_group_mix_body, dils),
        grid=(bsz * pos_blocks,),
        in_specs=[spec(dil, A_OUT_W) for dil in dils] + [spec(dil, LANES) for dil in dils],
        out_specs=pl.BlockSpec((tm, A_OUT_W), lambda m: (m, 0)),
        out_shape=jax.ShapeDtypeStruct((bsz * seq, A_OUT_W), BF16),
        scratch_shapes=scratch,
        compiler_params=_params("parallel"),
        name="dilation_group_mix",
    )(*outs, *lses)


def _diff_body(lambda_init, lam_ref, subln_ref, q_ref, k_ref, v_ref, o_ref):
    lam = lam_ref[...]
    dot1 = jnp.sum(lam[0:1, :] * lam[1:2, :], axis=1, keepdims=True)
    dot2 = jnp.sum(lam[2:3, :] * lam[3:4, :], axis=1, keepdims=True)
    lmbda = jnp.exp(dot1) - jnp.exp(dot2) + lambda_init
    k = k_ref[...]
    v = v_ref[...]
    sub = min(DIFF_SUB_ROWS, q_ref.shape[0])
    lane = lax.broadcasted_iota(jnp.int32, (sub, 2 * B_DIM), 1)
    zero = jnp.zeros((sub, 2 * B_DIM), BF16)

    def softmax_map_times_v(qm):
        s = _dot_nt(qm, k)
        p = jnp.exp2(s - jnp.max(s, axis=-1, keepdims=True))
        return jnp.dot(p.astype(BF16), v, preferred_element_type=F32) / jnp.sum(p, axis=-1, keepdims=True)

    for r0 in range(0, q_ref.shape[0], sub):
        q = q_ref[r0:r0 + sub, :]
        o = (softmax_map_times_v(jnp.where(lane < B_DIM, q, zero))
             - lmbda * softmax_map_times_v(jnp.where(lane >= B_DIM, q, zero)))
        o = o * lax.rsqrt(jnp.mean(o * o, axis=-1, keepdims=True) + SUBLN_EPS) * subln_ref[...]
        o_ref[r0:r0 + sub, :] = (o * (1.0 - lambda_init)).astype(o_ref.dtype)


def _diff_attention(qk3, v3, lam, subln, lambda_init):
    bsz, seq, _ = qk3.shape
    tq = min(DIFF_Q_ROWS, seq)
    hw = 2 * B_DIM
    return pl.pallas_call(
        functools.partial(_diff_body, lambda_init),
        grid=(bsz, B_HEADS, seq // tq),
        in_specs=[pl.BlockSpec((4, B_DIM), lambda b, h, i: (0, 0)),
                  pl.BlockSpec((1, hw), lambda b, h, i: (0, 0)),
                  pl.BlockSpec((None, tq, hw), lambda b, h, i: (b, i, h)),
                  pl.BlockSpec((None, seq, hw), lambda b, h, i: (b, 0, B_HEADS + h)),
                  pl.BlockSpec((None, seq, hw), lambda b, h, i: (b, 0, h))],
        out_specs=pl.BlockSpec((None, tq, hw), lambda b, h, i: (b, i, h)),
        out_shape=jax.ShapeDtypeStruct((bsz, seq, B_W), BF16),
        compiler_params=_params("parallel", "parallel", "parallel"),
        name="diff_attention",
    )(lam, subln, qk3, qk3, v3)


def _na_body(rows, k_rows, blocks, q_ref, k_ref, v_ref, bias_ref, o_ref):
    n_rb = rows // NA_Q_ROWS
    tq = NA_Q_ROWS * GRID_W
    for u in range(blocks):
        rb = pl.program_id(2) * blocks + u
        w0 = jnp.clip(rb * NA_Q_ROWS - NA_WIN_H // 2, 0, rows - k_rows) * GRID_W
        w0 = pl.multiple_of(w0, 256)
        kind = jnp.where(rb == 0, 0, jnp.where(rb == n_rb - 1, 2, 1))
        k = k_ref[pl.ds(w0, k_rows * GRID_W), :]
        v = v_ref[pl.ds(w0, k_rows * GRID_W), :]
        s = _dot_nt(q_ref[u * tq:(u + 1) * tq, :], k) + bias_ref[kind]
        p = jnp.exp2(s - jnp.max(s, axis=-1, keepdims=True))
        d = jnp.sum(p, axis=-1, keepdims=True)
        o = jnp.dot(p.astype(BF16), v, preferred_element_type=F32) / d
        o_ref[u * tq:(u + 1) * tq, :] = o.astype(o_ref.dtype)


def _na_bias_tables(rpb, rows):
    kh = min(NA_WIN_H, rows)
    k_rows = min(NA_K_ROWS, rows)
    n_rb = rows // NA_Q_ROWS
    n_heads = rpb.shape[0]
    edge = GRID_W - NA_WIN_W
    ext = jnp.pad(rpb.astype(F32) * LOG2_E, ((0, 0), (0, 0), (edge, edge)), mode="edge")
    wide = jnp.pad(ext, ((0, 0), (0, 0), (0, 1)))
    skew = jnp.broadcast_to(wide[:, :, None, :], wide.shape[:2] + (GRID_W, 2 * GRID_W))
    skew = skew.reshape(wide.shape[:2] + (2 * GRID_W * GRID_W,))[:, :, :GRID_W * (2 * GRID_W - 1)]
    toeplitz = skew.reshape(wide.shape[:2] + (GRID_W, 2 * GRID_W - 1))[:, :, :, GRID_W - 1:]
    c = jnp.arange(GRID_W)
    col_start = jnp.clip(c - NA_WIN_W // 2, 0, GRID_W - NA_WIN_W)
    col_ok = (c[None, :] >= col_start[:, None]) & (c[None, :] < col_start[:, None] + NA_WIN_W)
    toeplitz = jnp.where(col_ok[None, None], toeplitz, NEG_INF)
    tabs = []
    for rb in (0, min(1, n_rb - 1), n_rb - 1):
        r0 = rb * NA_Q_ROWS
        w0 = min(max(r0 - NA_WIN_H // 2, 0), rows - k_rows)
        strips = []
        for a in range(NA_Q_ROWS):
            qrow = r0 + a
            start = min(max(qrow - kh // 2, 0), rows - kh)
            first = start - qrow + (NA_WIN_H - 1)
            blk = toeplitz[:, first:first + kh].transpose(0, 2, 1, 3)
            blk = jnp.pad(blk, ((0, 0), (0, 0), (start - w0, k_rows - kh - (start - w0)), (0, 0)),
                          constant_values=NEG_INF)
            strips.append(blk.reshape(n_heads, GRID_W, k_rows * GRID_W))
        tabs.append(jnp.concatenate(strips, axis=1))
    return jnp.stack(tabs, axis=1)


def _neighborhood_attention(qkv3, rpb):
    bsz, seq, _ = qkv3.shape
    rows = seq // GRID_W
    k_rows = min(NA_K_ROWS, rows)
    tq = NA_Q_ROWS * GRID_W
    blocks = min(NA_BLOCKS_PER_STEP, rows // NA_Q_ROWS)
    step_rows = blocks * tq
    bias = _na_bias_tables(rpb, rows)
    return pl.pallas_call(
        functools.partial(_na_body, rows, k_rows, blocks),
        grid=(bsz, D_HEADS, seq // step_rows),
        in_specs=[pl.BlockSpec((None, step_rows, HEAD_DIM), lambda b, h, r: (b, r, h)),
                  pl.BlockSpec((None, seq, HEAD_DIM), lambda b, h, r: (b, 0, D_HEADS + h)),
                  pl.BlockSpec((None, seq, HEAD_DIM), lambda b, h, r: (b, 0, 2 * D_HEADS + h)),
                  pl.BlockSpec((None, 3, tq, k_rows * GRID_W), lambda b, h, r: (h, 0, 0, 0))],
        out_specs=pl.BlockSpec((None, step_rows, HEAD_DIM), lambda b, h, r: (b, r, h)),
        out_shape=jax.ShapeDtypeStruct((bsz, seq, D_W), BF16),
        compiler_params=_params("parallel", "parallel", "parallel"),
        name="neighborhood_attention",
    )(qkv3, qkv3, qkv3, bias)


def _merge_body(oa_ref, ob_ref, oc_ref, od_ref, wa_ref, wb_ref, wc_ref, wd_ref,
                ga_ref, gb_ref, gc_ref, gd_ref, out_ref):
    acc = ga_ref[...].astype(F32) * jnp.dot(oa_ref[...], wa_ref[...], preferred_element_type=F32)
    acc = acc + gb_ref[...].astype(F32) * jnp.dot(ob_ref[...], wb_ref[...], preferred_element_type=F32)
    acc = acc + gc_ref[...].astype(F32) * jnp.dot(oc_ref[...], wc_ref[...], preferred_element_type=F32)
    acc = acc + gd_ref[...].astype(F32) * jnp.dot(od_ref[...], wd_ref[...], preferred_element_type=F32)
    out_ref[...] = acc.astype(out_ref.dtype)


def _merge(branches, weights, gates, seq):
    m_tot = branches[0].shape[0]
    d = weights[0].shape[1]
    tm = min(1024, seq)
    tn = min(512, d)
    n_blocks = d // tn
    in_specs = [pl.BlockSpec((tm, o.shape[1]), lambda m, n: (m, 0)) for o in branches]
    in_specs += [pl.BlockSpec((w.shape[0], tn), lambda m, n: (0, n)) for w in weights]
    in_specs += [pl.BlockSpec((tm, tn), functools.partial(lambda m, n, i: (m, i * n_blocks + n), i=i))
                 for i in range(N_BRANCHES)]
    return pl.pallas_call(
        _merge_body,
        grid=(m_tot // tm, n_blocks),
        in_specs=in_specs,
        out_specs=pl.BlockSpec((tm, tn), lambda m, n: (m, n)),
        out_shape=jax.ShapeDtypeStruct((m_tot, d), BF16),
        compiler_params=_params("parallel", "parallel"),
        name="branch_merge",
    )(*branches, *weights, *([gates] * N_BRANCHES))


def _layer_norm_rows(y, gain, bias):
    mu = jnp.mean(y, axis=-1, keepdims=True)
    var = jnp.mean(jnp.square(y - mu), axis=-1, keepdims=True)
    return (y - mu) * lax.rsqrt(var + LN_EPS) * gain + bias


def _out_proj_body(alpha, mixed_ref, w_ref, x_ref, gain_ref, bias_ref, xb_ref, state_ref):
    d = x_ref.shape[1]
    y = alpha * x_ref[...] + jnp.dot(mixed_ref[...], w_ref[...], preferred_element_type=F32)
    out = _layer_norm_rows(y, gain_ref[...], bias_ref[...])
    xb_ref[...] = out.astype(BF16)
    state_ref[:, 0:d] = out
    state_ref[:, d:2 * d] = alpha * out
    state_ref[:, 2 * d:] = jnp.zeros((out.shape[0], EXPERT_PAD), F32)


def _out_proj(mixed, w, x, gain, bias, alpha, seq):
    m_tot, d = x.shape
    tm = min(512, seq)
    row = pl.BlockSpec((tm, d), lambda m: (m, 0))
    vec = pl.BlockSpec((1, d), lambda m: (0, 0))
    w_spec = pl.BlockSpec((d, d), lambda m: (0, 0), pipeline_mode=pl.Buffered(1))
    return pl.pallas_call(
        functools.partial(_out_proj_body, alpha),
        grid=(m_tot // tm,),
        in_specs=[row, w_spec, row, vec, vec],
        out_specs=[row, pl.BlockSpec((tm, 2 * d + EXPERT_PAD), lambda m: (m, 0))],
        out_shape=[jax.ShapeDtypeStruct((m_tot, d), BF16),
                   jax.ShapeDtypeStruct((m_tot, 2 * d + EXPERT_PAD), F32)],
        compiler_params=_params("parallel"),
        name="out_proj_layernorm",
    )(mixed, w, x, gain, bias)


def _router_body(cap, x_ref, wrt_ref, tok_ref, state_in_hbm, idx_ref, gate_ref):
    del state_in_hbm
    seq = x_ref.shape[0]
    logits = _dot_nt(wrt_ref[...], x_ref[...])
    e = jnp.exp(logits - jnp.max(logits, axis=0, keepdims=True))
    aff = e / jnp.sum(e, axis=0, keepdims=True)
    bits = lax.bitcast_convert_type(aff, jnp.int32)
    thr = jnp.zeros((N_EXPERTS, 1), jnp.int32)
    for bit in range(30, -1, -1):
        cand = thr | (1 << bit)
        cnt = jnp.sum((bits >= cand).astype(F32), axis=1, keepdims=True)
        thr = jnp.where(cnt >= cap, cand, thr)
    above = bits > thr
    tied = bits == thr
    need = cap - jnp.sum(above.astype(F32), axis=1, keepdims=True)

    chunk = min(512, seq)
    upper = (lax.broadcasted_iota(jnp.int32, (chunk, chunk), 0)
             <= lax.broadcasted_iota(jnp.int32, (chunk, chunk), 1)).astype(BF16)

    def prefix_count(mask):
        parts, carry = [], jnp.zeros((N_EXPERTS, 1), F32)
        for c in range(seq // chunk):
            part = jnp.dot(mask[:, c * chunk:(c + 1) * chunk].astype(BF16), upper,
                           preferred_element_type=F32) + carry
            parts.append(part)
            carry = part[:, chunk - 1:chunk]
        return jnp.concatenate(parts, axis=1)

    sel = above | (tied & (prefix_count(tied) <= need))
    slot = jnp.where(sel, prefix_count(sel) - 1.0, -1.0)
    gate = jnp.where(sel, aff, 0.0)
    pad = EXPERT_PAD - N_EXPERTS
    gate_ref[...] = jnp.concatenate([gate, jnp.zeros((pad, seq), F32)], axis=0).T

    slot_i = slot.astype(jnp.int32)
    ck = min(1024, seq)
    slot_iota = lax.broadcasted_iota(jnp.int32, (cap, ck), 0)
    lane = lax.broadcasted_iota(jnp.int32, (cap, EXPERT_PAD), 1)
    idx_cols = jnp.zeros((cap, EXPERT_PAD), F32)
    for e in range(N_EXPERTS):
        r = jnp.zeros((cap, LANES), F32)
        for c in range(seq // ck):
            onehot = (slot_iota == slot_i[e:e + 1, c * ck:(c + 1) * ck]).astype(BF16)
            r = r + jnp.dot(onehot, tok_ref[c * ck:(c + 1) * ck, :], preferred_element_type=F32)
        idx_cols = jnp.where(lane == e, r[:, 0:1] * 64.0 + r[:, 1:2], idx_cols)
    idx_ref[...] = idx_cols.T[0:N_EXPERTS, :].astype(jnp.int32) + pl.program_id(0) * seq


def _router(xb3, wrt, state, cap):
    bsz, seq, d = xb3.shape
    tok = lax.broadcasted_iota(jnp.int32, (seq, LANES), 0)
    lane = lax.broadcasted_iota(jnp.int32, (seq, LANES), 1)
    tok_tab = jnp.where(lane == 0, tok // 64, jnp.where(lane == 1, tok % 64, 0)).astype(F32)
    return pl.pallas_call(
        functools.partial(_router_body, cap),
        grid=(bsz,),
        in_specs=[pl.BlockSpec((None, seq, d), lambda b: (b, 0, 0)),
                  pl.BlockSpec((N_EXPERTS, d), lambda b: (0, 0)),
                  pl.BlockSpec((seq, LANES), lambda b: (0, 0)),
                  pl.BlockSpec(memory_space=pl.ANY)],
        out_specs=[pl.BlockSpec((None, N_EXPERTS, cap), lambda b: (b, 0, 0)),
                   pl.BlockSpec((seq, EXPERT_PAD), lambda b: (b, 2 * d // EXPERT_PAD))],
        out_shape=[jax.ShapeDtypeStruct((bsz, N_EXPERTS, cap), jnp.int32),
                   jax.ShapeDtypeStruct(state.shape, F32)],
        input_output_aliases={3: 1},
        compiler_params=_params("parallel"),
        name="router_topk",
    )(xb3, wrt, tok_tab.astype(BF16), state)


def _expert_body(d, idx_ref, state_in_hbm, wg_ref, wu_ref, wd_ref, state_hbm, rows_buf, sems):
    del state_in_hbm
    e = pl.program_id(0)
    n_chunks, rows, _ = rows_buf.shape

    def for_each_slot(fn):
        for p in range(rows):
            fn(p)

    def gather(c, p):
        row = idx_ref[0, c * rows + p]
        return pltpu.make_async_copy(state_hbm.at[pl.ds(row, 1), :], rows_buf.at[c].at[pl.ds(p, 1), :],
                                     sems.at[c, 0])

    def scatter(c, p):
        row = idx_ref[0, c * rows + p]
        return pltpu.make_async_copy(rows_buf.at[c].at[pl.ds(p, 1), pl.ds(d, d)],
                                     state_hbm.at[pl.ds(row, 1), pl.ds(d, d)], sems.at[c, 1])

    for c in range(n_chunks):
        for_each_slot(lambda p, c=c: gather(c, p).start(priority=p % 2))
    for c in range(n_chunks):
        for_each_slot(lambda p, c=c: gather(c, p).wait())
        x = rows_buf[c, :, 0:d].astype(BF16)
        g = jnp.dot(x, wg_ref[...], preferred_element_type=F32)
        u = jnp.dot(x, wu_ref[...], preferred_element_type=F32)
        h = (g / (1.0 + jnp.exp(-g))) * u
        y = jnp.dot(h.astype(BF16), wd_ref[...], preferred_element_type=F32)
        lane = lax.broadcasted_iota(jnp.int32, (rows, EXPERT_PAD), 1)
        gates = rows_buf[c, :, 2 * d:2 * d + EXPERT_PAD]
        gate_col = jnp.sum(jnp.where(lane == e, gates, 0.0), axis=1, keepdims=True)
        rows_buf[c, :, d:2 * d] = rows_buf[c, :, d:2 * d] + gate_col * y
        for_each_slot(lambda p, c=c: scatter(c, p).start(priority=p % 2))
    for c in range(n_chunks):
        for_each_slot(lambda p, c=c: scatter(c, p).wait())


def _expert_ffn(idx, state, layer, wg, wu, wd):
    bsz, n_exp, cap = idx.shape
    m_tot, width = state.shape
    d = (width - EXPERT_PAD) // 2
    f = wg.shape[3]
    n_chunks = EXPERT_SLOT_CHUNKS
    any_spec = pl.BlockSpec(memory_space=pl.ANY)
    return pl.pallas_call(
        functools.partial(_expert_body, d),
        grid=(n_exp, bsz),
        in_specs=[pl.BlockSpec((None, 1, cap), lambda e, b: (b * n_exp + e, 0, 0), memory_space=pltpu.SMEM),
                  any_spec,
                  pl.BlockSpec((None, None, d, f), lambda e, b: (layer, e, 0, 0)),
                  pl.BlockSpec((None, None, d, f), lambda e, b: (layer, e, 0, 0)),
                  pl.BlockSpec((None, None, f, d), lambda e, b: (layer, e, 0, 0))],
        out_specs=any_spec,
        out_shape=jax.ShapeDtypeStruct((m_tot, width), F32),
        scratch_shapes=[pltpu.VMEM((n_chunks, cap // n_chunks, width), F32),
                        pltpu.SemaphoreType.DMA((n_chunks, 2))],
        input_output_aliases={1: 0},
        compiler_params=_params("arbitrary", "arbitrary"),
        name="expert_swiglu_scatter",
    )(idx.reshape(bsz * n_exp, 1, cap), state, wg, wu, wd)


def _final_norm_body(acc_ref, gain_ref, bias_ref, xo_ref, xb_ref):
    out = _layer_norm_rows(acc_ref[...], gain_ref[...], bias_ref[...])
    xo_ref[...] = out
    xb_ref[...] = out.astype(BF16)


def _final_norm(state, gain, bias, seq):
    m_tot = state.shape[0]
    d = gain.shape[1]
    tm = min(512, seq)
    row = pl.BlockSpec((tm, d), lambda m: (m, 0))
    vec = pl.BlockSpec((1, d), lambda m: (0, 0))
    return pl.pallas_call(
        _final_norm_body,
        grid=(m_tot // tm,),
        in_specs=[pl.BlockSpec((tm, d), lambda m: (m, 1)), vec, vec],
        out_specs=[row, row],
        out_shape=[jax.ShapeDtypeStruct((m_tot, d), F32), jax.ShapeDtypeStruct((m_tot, d), BF16)],
        compiler_params=_params("parallel"),
        name="channel_mixer_layernorm",
    )(state, gain, bias)


def _offsets(widths):
    offs = [0]
    for w in widths:
        offs.append(offs[-1] + w)
    return offs


def kernel(x, w_in, b_gate, w_branch, w_out, diff_lambda, diff_subln, sink_logit, na_rpb,
           w_router, w_exp_gate, w_exp_up, w_exp_down, ln_gain, ln_bias):
    bsz, seq, d = x.shape
    depth = w_in.shape[0]
    m_tot = bsz * seq
    alpha = (2.0 * depth) ** 0.25
    cap = EC_CAPACITY_FACTOR * seq // N_EXPERTS
    in_widths = (A_W, A_W, A_W, B_W, B_W, B_W, C_Q_W, C_KV_W, C_KV_W, D_W, D_W, D_W, N_BRANCHES * d)
    io = _offsets(in_widths)
    bo = _offsets((A_OUT_W, B_W, C_Q_W, D_W))
    cos128, sin128 = _rope_tables(seq, HEAD_DIM // 2)
    cos64, sin64 = _rope_tables(seq, B_DIM // 2)

    wg_stack, wu_stack, wd_stack = (w.astype(BF16) for w in (w_exp_gate, w_exp_up, w_exp_down))
    xf = x.reshape(m_tot, d)
    xb = xf.astype(BF16)
    for l in range(depth):
        lambda_init = 0.8 - 0.6 * math.exp(-0.3 * l)
        def proj(a, b, mode, tn, extra=(), **q_fold):
            return _proj(xb, w_in, l, io[a], io[b] - io[a], mode, tn, seq, extra, **q_fold)

        qk_b = proj(3, 5, "rope64", B_W, (cos64, sin64), q_cols=B_W, q_scale=DIFF_Q_SCALE)
        v_b = proj(5, 6, "plain", B_W)
        qk_c = proj(6, 8, "rope128", C_Q_W + C_KV_W, (cos128, sin128), q_cols=C_Q_W, q_scale=ATTN_Q_SCALE)
        v_c = proj(8, 9, "plain", C_KV_W)
        qkv_d = proj(9, 12, "plain", D_W, q_cols=D_W, q_scale=ATTN_Q_SCALE)
        gates = proj(12, 13, "gate", min(1024, d), (b_gate[l].reshape(1, -1),))

        outs, lses = [], []
        for g, (w, r) in enumerate(DIL_PAIRS):
            qkv_g = _proj_dilated(xb, w_in, l, g, r, bsz, seq, cos128, sin128)
            o_g, lse_g = _banded(qkv_g, qkv_g, qkv_g, 0, 1, 2, hq=A_HEADS_PER_GROUP, group=1,
                                 halo=(w // 2) // r, want_lse=True)
            outs.append(o_g)
            lses.append(lse_g)
        o_a = _group_mix(outs, lses, seq)

        o_b = _diff_attention(qk_b.reshape(bsz, seq, 2 * B_W), v_b.reshape(bsz, seq, B_W),
                              diff_lambda[l], diff_subln[l].reshape(1, -1), lambda_init)

        qk_c4 = qk_c.reshape(bsz, 1, seq, C_Q_W + C_KV_W)
        (o_c,) = _banded(qk_c4, qk_c4, v_c.reshape(bsz, 1, seq, C_KV_W), 0, C_Q_W // C_KV_W, 0,
                         hq=C_Q_HEADS, group=C_Q_HEADS // C_KV_HEADS, halo=C_HALF_WINDOW, sink=sink_logit[l])

        o_d = _neighborhood_attention(qkv_d.reshape(bsz, seq, 3 * D_W), na_rpb[l])

        wb = w_branch[l].astype(BF16)
        mixed = _merge([o_a, o_b.reshape(m_tot, B_W), o_c.reshape(m_tot, C_Q_W), o_d.reshape(m_tot, D_W)],
                       [wb[bo[i]:bo[i + 1]] for i in range(N_BRANCHES)], gates, seq)
        xb, state = _out_proj(mixed, w_out[l].astype(BF16), xf, ln_gain[l, 0].reshape(1, -1),
                              ln_bias[l, 0].reshape(1, -1), alpha, seq)

        idx, state = _router(xb.reshape(bsz, seq, d), w_router[l].T.astype(BF16), state, cap)
        state = _expert_ffn(idx, state, l, wg_stack, wu_stack, wd_stack)
        xf, xb = _final_norm(state, ln_gain[l, 1].reshape(1, -1), ln_bias[l, 1].reshape(1, -1), seq)
    return xf.reshape(bsz, seq, d)
```

```python
import functools
import math

import jax
import jax.numpy as jnp
from jax import lax
from jax.experimental import pallas as pl
from jax.experimental.pallas import tpu as pltpu

F32 = jnp.float32
BF16 = jnp.bfloat16

LANES = 128
HEAD_DIM = 128
ROPE_THETA = 10000.0
NEG_INF = -1e30
LN_EPS = 1e-5
SUBLN_EPS = 1e-5
DIL_PAIRS = ((128, 1), (512, 4), (2048, 16))
A_HEADS_PER_GROUP = 6
A_GROUPS = len(DIL_PAIRS)
A_HEADS = A_GROUPS * A_HEADS_PER_GROUP
B_HEADS = 8
B_DIM = 64
C_Q_HEADS = 8
C_KV_HEADS = 2
C_HALF_WINDOW = 128
D_HEADS = 8
GRID_W = 64
NA_WIN_H = 8
NA_WIN_W = 16
BANDED_TILES_PER_STEP = 2
NA_Q_ROWS = 8
NA_K_ROWS = 16
NA_BLOCKS_PER_STEP = 8
DIFF_Q_ROWS = 1024
DIFF_SUB_ROWS = 256
LOG2_E = math.log2(math.e)
LN_2 = math.log(2.0)
DIFF_Q_SCALE = (B_DIM ** -0.5) * LOG2_E
ATTN_Q_SCALE = (HEAD_DIM ** -0.5) * LOG2_E
N_BRANCHES = 4
N_EXPERTS = 16
EC_CAPACITY_FACTOR = 2
EXPERT_PAD = 128
EXPERT_SLOT_CHUNKS = 2

A_W = A_HEADS * HEAD_DIM
B_W = B_HEADS * 2 * B_DIM
C_Q_W = C_Q_HEADS * HEAD_DIM
C_KV_W = C_KV_HEADS * HEAD_DIM
D_W = D_HEADS * HEAD_DIM
A_OUT_W = A_HEADS_PER_GROUP * HEAD_DIM

VMEM_LIMIT = 56 * 1024 * 1024


def _params(*sem):
    return pltpu.CompilerParams(dimension_semantics=sem, vmem_limit_bytes=VMEM_LIMIT)


def _dot_nt(a, b):
    return lax.dot_general(a, b, (((1,), (1,)), ((), ())), preferred_element_type=F32)


def _cast_weight_tile(w_ref, wb_ref):
    @pl.when(pl.program_id(1) == 0)
    def _():
        wb_ref[...] = w_ref[...].astype(BF16)
    return wb_ref[...]


def _proj_body(mode, q_cols, q_scale, x_ref, w_ref, *rest):
    rest, wb_ref = rest[:-1], rest[-1]
    acc = jnp.dot(x_ref[...], _cast_weight_tile(w_ref, wb_ref), preferred_element_type=F32)
    tn = acc.shape[1]

    def chunk_scale(c):
        return jnp.where(pl.program_id(0) * tn + c * LANES < q_cols, q_scale, 1.0).astype(F32)

    if mode == "plain":
        (o_ref,) = rest
        if q_cols:
            for c in range(tn // LANES):
                cs = slice(c * LANES, (c + 1) * LANES)
                o_ref[:, cs] = (acc[:, cs] * chunk_scale(c)).astype(o_ref.dtype)
        else:
            o_ref[...] = acc.astype(o_ref.dtype)
    elif mode == "gate":
        b_ref, o_ref = rest
        z = acc + b_ref[...]
        o_ref[...] = (1.0 / (1.0 + jnp.exp(-z))).astype(o_ref.dtype)
    else:
        cos_ref, sin_ref, o_ref = rest
        cos = cos_ref[...]
        sin = sin_ref[...]
        lane = lax.broadcasted_iota(jnp.int32, cos.shape, 1)
        for c in range(tn // LANES):
            a = acc[:, c * LANES:(c + 1) * LANES]
            if mode == "rope128":
                rot = pltpu.roll(a, 64, 1)
            else:
                rot = jnp.where((lane % 64) < 32, pltpu.roll(a, 96, 1), pltpu.roll(a, 32, 1))
            qs = chunk_scale(c)
            o_ref[:, c * LANES:(c + 1) * LANES] = (a * (cos * qs) + rot * (sin * qs)).astype(o_ref.dtype)


def _weight_tile_spec(k, tn, layer, lane_tile_of):
    return pl.BlockSpec((pl.Squeezed(), pl.Element(k), pl.Element(tn)),
                        lambda n, m: (layer, 0, lane_tile_of(n) * LANES))


def _proj(xb, w_stack, layer, col0, n_tot, mode, tn, seq, extra=(), q_cols=0, q_scale=1.0):
    m_tot, k = xb.shape
    tm = min(1024, seq)
    pos_blocks = seq // tm
    in_specs = [pl.BlockSpec((tm, k), lambda n, m: (m, 0)),
                _weight_tile_spec(k, tn, layer, lambda n: col0 // LANES + n * (tn // LANES))]
    if mode == "gate":
        in_specs.append(pl.BlockSpec((1, tn), lambda n, m: (0, n)))
    elif mode != "plain":
        in_specs += [pl.BlockSpec((tm, LANES), lambda n, m: (m % pos_blocks, 0))] * 2
    return pl.pallas_call(
        functools.partial(_proj_body, mode, q_cols, q_scale),
        grid=(n_tot // tn, m_tot // tm),
        in_specs=in_specs,
        out_specs=pl.BlockSpec((tm, tn), lambda n, m: (m, n)),
        out_shape=jax.ShapeDtypeStruct((m_tot, n_tot), BF16),
        scratch_shapes=[pltpu.VMEM((k, tn), BF16)],
        compiler_params=_params("parallel", "arbitrary"),
        name="proj_" + mode,
    )(xb, w_stack, *extra)


def _proj_dilated_body(dil, x_ref, w_ref, cos_ref, sin_ref, o_ref, wb_ref, *scratch):
    n = pl.program_id(0)
    tm = x_ref.shape[0]
    rows = tm // dil
    acc = jnp.dot(x_ref[...], _cast_weight_tile(w_ref, wb_ref), preferred_element_type=F32)
    dst = scratch[0] if dil > 1 else None

    def emit(c, val):
        if dil > 1:
            dst[c] = val
        else:
            o_ref[0, :, c * LANES:(c + 1) * LANES] = val.astype(o_ref.dtype)

    @pl.when(n < 2)
    def _():
        qs = jnp.where(n == 0, ATTN_Q_SCALE, 1.0).astype(F32)
        cos = cos_ref[...] * qs
        sin = sin_ref[...] * qs
        for c in range(acc.shape[1] // LANES):
            a = acc[:, c * LANES:(c + 1) * LANES]
            emit(c, a * cos + pltpu.roll(a, 64, 1) * sin)

    @pl.when(n >= 2)
    def _():
        for c in range(acc.shape[1] // LANES):
            emit(c, acc[:, c * LANES:(c + 1) * LANES])

    if dil > 1:
        for c in range(acc.shape[1] // LANES):
            for j in range(dil):
                o_ref[j, :, c * LANES:(c + 1) * LANES] = (
                    dst.at[c][pl.ds(j, rows, stride=dil), :].astype(o_ref.dtype))


def _proj_dilated(xb, w_stack, layer, group, dil, bsz, seq, cos, sin):
    m_tot, k = xb.shape
    tm = min(1024, seq)
    tn = A_OUT_W
    pos_blocks = seq // tm
    scratch = [pltpu.VMEM((k, tn), BF16)]
    if dil > 1:
        scratch.append(pltpu.VMEM((tn // LANES, tm, LANES), F32))
    return pl.pallas_call(
        functools.partial(_proj_dilated_body, dil),
        grid=(3, m_tot // tm),
        in_specs=[pl.BlockSpec((tm, k), lambda n, m: (m, 0)),
                  _weight_tile_spec(k, tn, layer, lambda n: n * (A_W // LANES) + group * (tn // LANES)),
                  pl.BlockSpec((tm, LANES), lambda n, m: (m % pos_blocks, 0)),
                  pl.BlockSpec((tm, LANES), lambda n, m: (m % pos_blocks, 0))],
        out_specs=pl.BlockSpec((None, dil, tm // dil, tn), lambda n, m: (m // pos_blocks, 0, m % pos_blocks, n)),
        out_shape=jax.ShapeDtypeStruct((bsz, dil, seq // dil, 3 * tn), BF16),
        scratch_shapes=scratch,
        compiler_params=_params("parallel", "arbitrary"),
        name="proj_dilated",
    )(xb, w_stack, cos, sin)


def _rope_tables(seq, half):
    inv = ROPE_THETA ** (-jnp.arange(half, dtype=F32) / half)
    ang = jnp.arange(seq, dtype=jnp.int32).astype(F32)[:, None] * inv[None, :]
    cos, sin = jnp.cos(ang), jnp.sin(ang)
    reps = LANES // (2 * half)
    return (jnp.tile(jnp.concatenate([cos, cos], axis=1), (1, reps)),
            jnp.tile(jnp.concatenate([-sin, sin], axis=1), (1, reps)))


def _banded_body(hq, group, halo, tq, win, length, has_sink, want_lse, *refs):
    refs = list(refs)
    q_ref, k_ref, v_ref = refs[:3]
    pos = 3
    sink_ref = None
    if has_sink:
        sink_ref = refs[pos]
        pos += 1
    o_ref = refs[pos]
    lse_ref = refs[pos + 1] if want_lse else None

    lane = lax.broadcasted_iota(jnp.int32, (tq, LANES), 1)
    for u in range(q_ref.shape[0] // tq):
        i = pl.program_id(2) * (q_ref.shape[0] // tq) + u
        rows = slice(u * tq, (u + 1) * tq)
        start = pl.multiple_of(jnp.clip(i * tq - halo, 0, length - win), 64)
        qpos = i * tq + lax.broadcasted_iota(jnp.int32, (tq, win), 0)
        kpos = start + lax.broadcasted_iota(jnp.int32, (tq, win), 1)
        valid = jnp.abs(qpos - kpos) <= halo
        lse_acc = jnp.zeros((tq, LANES), F32)
        for h in range(hq):
            kh = h // group
            q = q_ref[rows, h * HEAD_DIM:(h + 1) * HEAD_DIM]
            k = k_ref[pl.ds(start, win), kh * HEAD_DIM:(kh + 1) * HEAD_DIM]
            v = v_ref[pl.ds(start, win), kh * HEAD_DIM:(kh + 1) * HEAD_DIM]
            s = jnp.where(valid, _dot_nt(q, k), NEG_INF)
            m = jnp.max(s, axis=-1, keepdims=True)
            if has_sink:
                sk = sink_ref[h] * LOG2_E
                m = jnp.maximum(m, sk)
            p = jnp.exp2(s - m)
            d = jnp.sum(p, axis=-1, keepdims=True)
            if has_sink:
                d = d + jnp.exp2(sk - m)
            o = jnp.dot(p.astype(BF16), v, preferred_element_type=F32) / d
            o_ref[rows, h * HEAD_DIM:(h + 1) * HEAD_DIM] = o.astype(o_ref.dtype)
            if want_lse:
                lse_acc = jnp.where(lane == h, (m + jnp.log2(d)) * LN_2, lse_acc)
        if want_lse:
            lse_ref[rows, :] = lse_acc


def _banded(q4, k4, v4, qcol, kcol, vcol, *, hq, group, halo, sink=None, want_lse=False):
    bsz, dil, length, _ = q4.shape
    hkv = hq // group
    tq = min(256, length)
    win = min(length, tq + 2 * halo)
    qw, kw = hq * HEAD_DIM, hkv * HEAD_DIM
    step_rows = tq * min(BANDED_TILES_PER_STEP, length // tq)
    in_specs = [pl.BlockSpec((None, None, step_rows, qw), lambda b, j, i: (b, j, i, qcol)),
                pl.BlockSpec((None, None, length, kw), lambda b, j, i: (b, j, 0, kcol)),
                pl.BlockSpec((None, None, length, kw), lambda b, j, i: (b, j, 0, vcol))]
    args = [q4, k4, v4]
    if sink is not None:
        in_specs.append(pl.BlockSpec(memory_space=pltpu.SMEM))
        args.append(sink)
    out_specs = [pl.BlockSpec((None, None, step_rows, qw), lambda b, j, i: (b, j, i, 0))]
    out_shape = [jax.ShapeDtypeStruct((bsz, dil, length, qw), BF16)]
    if want_lse:
        out_specs.append(pl.BlockSpec((None, None, step_rows, LANES), lambda b, j, i: (b, j, i, 0)))
        out_shape.append(jax.ShapeDtypeStruct((bsz, dil, length, LANES), F32))
    return pl.pallas_call(
        functools.partial(_banded_body, hq, group, halo, tq, win, length, sink is not None, want_lse),
        grid=(bsz, dil, length // step_rows),
        in_specs=in_specs,
        out_specs=out_specs,
        out_shape=out_shape,
        compiler_params=_params("parallel", "parallel", "parallel"),
        name="banded_attention",
    )(*args)


def _group_mix_body(dils, *refs):
    n = len(dils)
    o_refs, l_refs, out_ref = refs[:n], refs[n:2 * n], refs[2 * n]
    scratch = list(refs[2 * n + 1:])
    tm = out_ref.shape[0]
    heads, lses = [], []
    for g, dil in enumerate(dils):
        if dil == 1:
            heads.append(functools.partial(
                lambda h, ref: ref[0, :, h * HEAD_DIM:(h + 1) * HEAD_DIM].astype(F32), ref=o_refs[g]))
            lses.append(l_refs[g][0])
            continue
        o_scr, l_scr = scratch.pop(0), scratch.pop(0)
        rows = tm // dil
        for j in range(dil):
            l_scr[pl.ds(j, rows, stride=dil), :] = l_refs[g][j]
            for h in range(A_HEADS_PER_GROUP):
                o_scr.at[h][pl.ds(j, rows, stride=dil), :] = (
                    o_refs[g][j, :, h * HEAD_DIM:(h + 1) * HEAD_DIM].astype(F32))
        heads.append(functools.partial(lambda h, ref: ref[h], ref=o_scr))
        lses.append(l_scr[...])
    m = functools.reduce(jnp.maximum, lses)
    es = [jnp.exp(l - m) for l in lses]
    tot = functools.reduce(lambda a, b: a + b, es)
    ws = [e / tot for e in es]
    for h in range(A_HEADS_PER_GROUP):
        acc = ws[0][:, h:h + 1] * heads[0](h)
        for g in range(1, n):
            acc = acc + ws[g][:, h:h + 1] * heads[g](h)
        out_ref[:, h * HEAD_DIM:(h + 1) * HEAD_DIM] = acc.astype(out_ref.dtype)


def _group_mix(outs, lses, seq):
    bsz = outs[0].shape[0]
    dils = tuple(o.shape[1] for o in outs)
    tm = min(512, seq)
    pos_blocks = seq // tm

    def spec(dil, width):
        return pl.BlockSpec((None, dil, tm // dil, width), lambda m: (m // pos_blocks, 0, m % pos_blocks, 0))

    scratch = []
    for dil in dils:
        if dil > 1:
            scratch += [pltpu.VMEM((A_HEADS_PER_GROUP, tm, HEAD_DIM), F32), pltpu.VMEM((tm, LANES), F32)]
    return pl.pallas_call(
        functools.partial(_group_mix_body, dils),
        grid=(bsz * pos_blocks,),
        in_specs=[spec(dil, A_OUT_W) for dil in dils] + [spec(dil, LANES) for dil in dils],
        out_specs=pl.BlockSpec((tm, A_OUT_W), lambda m: (m, 0)),
        out_shape=jax.ShapeDtypeStruct((bsz * seq, A_OUT_W), BF16),
        scratch_shapes=scratch,
        compiler_params=_params("parallel"),
        name="dilation_group_mix",
    )(*outs, *lses)


def _diff_body(lambda_init, lam_ref, subln_ref, q_ref, k_ref, v_ref, o_ref):
    lam = lam_ref[...]
    dot1 = jnp.sum(lam[0:1, :] * lam[1:2, :], axis=1, keepdims=True)
    dot2 = jnp.sum(lam[2:3, :] * lam[3:4, :], axis=1, keepdims=True)
    lmbda = jnp.exp(dot1) - jnp.exp(dot2) + lambda_init
    k = k_ref[...]
    v = v_ref[...]
    sub = min(DIFF_SUB_ROWS, q_ref.shape[0])
    lane = lax.broadcasted_iota(jnp.int32, (sub, 2 * B_DIM), 1)
    zero = jnp.zeros((sub, 2 * B_DIM), BF16)

    def softmax_map_times_v(qm):
        s = _dot_nt(qm, k)
        p = jnp.exp2(s - jnp.max(s, axis=-1, keepdims=True))
        return jnp.dot(p.astype(BF16), v, preferred_element_type=F32) / jnp.sum(p, axis=-1, keepdims=True)

    for r0 in range(0, q_ref.shape[0], sub):
        q = q_ref[r0:r0 + sub, :]
        o = (softmax_map_times_v(jnp.where(lane < B_DIM, q, zero))
             - lmbda * softmax_map_times_v(jnp.where(lane >= B_DIM, q, zero)))
        o = o * lax.rsqrt(jnp.mean(o * o, axis=-1, keepdims=True) + SUBLN_EPS) * subln_ref[...]
        o_ref[r0:r0 + sub, :] = (o * (1.0 - lambda_init)).astype(o_ref.dtype)


def _diff_attention(qk3, v3, lam, subln, lambda_init):
    bsz, seq, _ = qk3.shape
    tq = min(DIFF_Q_ROWS, seq)
    hw = 2 * B_DIM
    return pl.pallas_call(
        functools.partial(_diff_body, lambda_init),
        grid=(bsz, B_HEADS, seq // tq),
        in_specs=[pl.BlockSpec((4, B_DIM), lambda b, h, i: (0, 0)),
                  pl.BlockSpec((1, hw), lambda b, h, i: (0, 0)),
                  pl.BlockSpec((None, tq, hw), lambda b, h, i: (b, i, h)),
                  pl.BlockSpec((None, seq, hw), lambda b, h, i: (b, 0, B_HEADS + h)),
                  pl.BlockSpec((None, seq, hw), lambda b, h, i: (b, 0, h))],
        out_specs=pl.BlockSpec((None, tq, hw), lambda b, h, i: (b, i, h)),
        out_shape=jax.ShapeDtypeStruct((bsz, seq, B_W), BF16),
        compiler_params=_params("parallel", "parallel", "parallel"),
        name="diff_attention",
    )(lam, subln, qk3, qk3, v3)


def _na_body(rows, k_rows, blocks, q_ref, k_ref, v_ref, bias_ref, o_ref):
    n_rb = rows // NA_Q_ROWS
    tq = NA_Q_ROWS * GRID_W
    for u in range(blocks):
        rb = pl.program_id(2) * blocks + u
        w0 = jnp.clip(rb * NA_Q_ROWS - NA_WIN_H // 2, 0, rows - k_rows) * GRID_W
        w0 = pl.multiple_of(w0, 256)
        kind = jnp.where(rb == 0, 0, jnp.where(rb == n_rb - 1, 2, 1))
        k = k_ref[pl.ds(w0, k_rows * GRID_W), :]
        v = v_ref[pl.ds(w0, k_rows * GRID_W), :]
        s = _dot_nt(q_ref[u * tq:(u + 1) * tq, :], k) + bias_ref[kind]
        p = jnp.exp2(s - jnp.max(s, axis=-1, keepdims=True))
        d = jnp.sum(p, axis=-1, keepdims=True)
        o = jnp.dot(p.astype(BF16), v, preferred_element_type=F32) / d
        o_ref[u * tq:(u + 1) * tq, :] = o.astype(o_ref.dtype)


def _na_bias_tables(rpb, rows):
    kh = min(NA_WIN_H, rows)
    k_rows = min(NA_K_ROWS, rows)
    n_rb = rows // NA_Q_ROWS
    n_heads = rpb.shape[0]
    edge = GRID_W - NA_WIN_W
    ext = jnp.pad(rpb.astype(F32) * LOG2_E, ((0, 0), (0, 0), (edge, edge)), mode="edge")
    toeplitz = jnp.stack([ext[:, :, GRID_W - 1 - qc:2 * GRID_W - 1 - qc] for qc in range(GRID_W)], axis=2)
    c = jnp.arange(GRID_W)
    col_start = jnp.clip(c - NA_WIN_W // 2, 0, GRID_W - NA_WIN_W)
    col_ok = (c[None, :] >= col_start[:, None]) & (c[None, :] < col_start[:, None] + NA_WIN_W)
    toeplitz = jnp.where(col_ok[None, None], toeplitz, NEG_INF)
    tabs = []
    for rb in (0, min(1, n_rb - 1), n_rb - 1):
        r0 = rb * NA_Q_ROWS
        w0 = min(max(r0 - NA_WIN_H // 2, 0), rows - k_rows)
        strips = []
        for a in range(NA_Q_ROWS):
            qrow = r0 + a
            start = min(max(qrow - kh // 2, 0), rows - kh)
            first = start - qrow + (NA_WIN_H - 1)
            blk = toeplitz[:, first:first + kh].transpose(0, 2, 1, 3)
            blk = jnp.pad(blk, ((0, 0), (0, 0), (start - w0, k_rows - kh - (start - w0)), (0, 0)),
                          constant_values=NEG_INF)
            strips.append(blk.reshape(n_heads, GRID_W, k_rows * GRID_W))
        tabs.append(jnp.concatenate(strips, axis=1))
    return jnp.stack(tabs, axis=1)


def _neighborhood_attention(qkv3, bias, layer):
    bsz, seq, _ = qkv3.shape
    rows = seq // GRID_W
    k_rows = min(NA_K_ROWS, rows)
    tq = NA_Q_ROWS * GRID_W
    blocks = min(NA_BLOCKS_PER_STEP, rows // NA_Q_ROWS)
    step_rows = blocks * tq
    return pl.pallas_call(
        functools.partial(_na_body, rows, k_rows, blocks),
        grid=(bsz, D_HEADS, seq // step_rows),
        in_specs=[pl.BlockSpec((None, step_rows, HEAD_DIM), lambda b, h, r: (b, r, h)),
                  pl.BlockSpec((None, seq, HEAD_DIM), lambda b, h, r: (b, 0, D_HEADS + h)),
                  pl.BlockSpec((None, seq, HEAD_DIM), lambda b, h, r: (b, 0, 2 * D_HEADS + h)),
                  pl.BlockSpec((None, 3, tq, k_rows * GRID_W), lambda b, h, r: (layer * D_HEADS + h, 0, 0, 0))],
        out_specs=pl.BlockSpec((None, step_rows, HEAD_DIM), lambda b, h, r: (b, r, h)),
        out_shape=jax.ShapeDtypeStruct((bsz, seq, D_W), BF16),
        compiler_params=_params("parallel", "parallel", "parallel"),
        name="neighborhood_attention",
    )(qkv3, qkv3, qkv3, bias)


def _merge_body(oa_ref, ob_ref, oc_ref, od_ref, wa_ref, wb_ref, wc_ref, wd_ref,
                ga_ref, gb_ref, gc_ref, gd_ref, out_ref):
    acc = ga_ref[...].astype(F32) * jnp.dot(oa_ref[...], wa_ref[...], preferred_element_type=F32)
    acc = acc + gb_ref[...].astype(F32) * jnp.dot(ob_ref[...], wb_ref[...], preferred_element_type=F32)
    acc = acc + gc_ref[...].astype(F32) * jnp.dot(oc_ref[...], wc_ref[...], preferred_element_type=F32)
    acc = acc + gd_ref[...].astype(F32) * jnp.dot(od_ref[...], wd_ref[...], preferred_element_type=F32)
    out_ref[...] = acc.astype(out_ref.dtype)


def _merge(branches, weights, gates, seq):
    m_tot = branches[0].shape[0]
    d = weights[0].shape[1]
    tm = min(1024, seq)
    tn = min(512, d)
    n_blocks = d // tn
    in_specs = [pl.BlockSpec((tm, o.shape[1]), lambda m, n: (m, 0)) for o in branches]
    in_specs += [pl.BlockSpec((w.shape[0], tn), lambda m, n: (0, n)) for w in weights]
    in_specs += [pl.BlockSpec((tm, tn), functools.partial(lambda m, n, i: (m, i * n_blocks + n), i=i))
                 for i in range(N_BRANCHES)]
    return pl.pallas_call(
        _merge_body,
        grid=(m_tot // tm, n_blocks),
        in_specs=in_specs,
        out_specs=pl.BlockSpec((tm, tn), lambda m, n: (m, n)),
        out_shape=jax.ShapeDtypeStruct((m_tot, d), BF16),
        compiler_params=_params("parallel", "parallel"),
        name="branch_merge",
    )(*branches, *weights, *([gates] * N_BRANCHES))


def _layer_norm_rows(y, gain, bias):
    mu = jnp.mean(y, axis=-1, keepdims=True)
    var = jnp.mean(jnp.square(y - mu), axis=-1, keepdims=True)
    return (y - mu) * lax.rsqrt(var + LN_EPS) * gain + bias


def _out_proj_body(alpha, mixed_ref, w_ref, x_ref, gain_ref, bias_ref, xb_ref, state_ref):
    d = x_ref.shape[1]
    y = alpha * x_ref[...] + jnp.dot(mixed_ref[...], w_ref[...], preferred_element_type=F32)
    out = _layer_norm_rows(y, gain_ref[...], bias_ref[...])
    xb_ref[...] = out.astype(BF16)
    state_ref[:, 0:d] = out
    state_ref[:, d:2 * d] = alpha * out
    state_ref[:, 2 * d:] = jnp.zeros((out.shape[0], EXPERT_PAD), F32)


def _out_proj(mixed, w, x, gain, bias, alpha, seq):
    m_tot, d = x.shape
    tm = min(512, seq)
    row = pl.BlockSpec((tm, d), lambda m: (m, 0))
    vec = pl.BlockSpec((1, d), lambda m: (0, 0))
    w_spec = pl.BlockSpec((d, d), lambda m: (0, 0), pipeline_mode=pl.Buffered(1))
    return pl.pallas_call(
        functools.partial(_out_proj_body, alpha),
        grid=(m_tot // tm,),
        in_specs=[row, w_spec, row, vec, vec],
        out_specs=[row, pl.BlockSpec((tm, 2 * d + EXPERT_PAD), lambda m: (m, 0))],
        out_shape=[jax.ShapeDtypeStruct((m_tot, d), BF16),
                   jax.ShapeDtypeStruct((m_tot, 2 * d + EXPERT_PAD), F32)],
        compiler_params=_params("parallel"),
        name="out_proj_layernorm",
    )(mixed, w, x, gain, bias)


def _router_body(cap, x_ref, wrt_ref, tok_ref, state_in_hbm, idx_ref, gate_ref):
    del state_in_hbm
    seq = x_ref.shape[0]
    logits = _dot_nt(wrt_ref[...], x_ref[...])
    e = jnp.exp(logits - jnp.max(logits, axis=0, keepdims=True))
    aff = e / jnp.sum(e, axis=0, keepdims=True)
    bits = lax.bitcast_convert_type(aff, jnp.int32)
    thr = jnp.zeros((N_EXPERTS, 1), jnp.int32)
    for bit in range(30, -1, -1):
        cand = thr | (1 << bit)
        cnt = jnp.sum((bits >= cand).astype(F32), axis=1, keepdims=True)
        thr = jnp.where(cnt >= cap, cand, thr)
    above = bits > thr
    tied = bits == thr
    need = cap - jnp.sum(above.astype(F32), axis=1, keepdims=True)

    chunk = min(512, seq)
    upper = (lax.broadcasted_iota(jnp.int32, (chunk, chunk), 0)
             <= lax.broadcasted_iota(jnp.int32, (chunk, chunk), 1)).astype(BF16)

    def prefix_count(mask):
        parts, carry = [], jnp.zeros((N_EXPERTS, 1), F32)
        for c in range(seq // chunk):
            part = jnp.dot(mask[:, c * chunk:(c + 1) * chunk].astype(BF16), upper,
                           preferred_element_type=F32) + carry
            parts.append(part)
            carry = part[:, chunk - 1:chunk]
        return jnp.concatenate(parts, axis=1)

    sel = above | (tied & (prefix_count(tied) <= need))
    slot = jnp.where(sel, prefix_count(sel) - 1.0, -1.0)
    gate = jnp.where(sel, aff, 0.0)
    pad = EXPERT_PAD - N_EXPERTS
    gate_ref[...] = jnp.concatenate([gate, jnp.zeros((pad, seq), F32)], axis=0).T

    slot_i = slot.astype(jnp.int32)
    ck = min(1024, seq)
    slot_iota = lax.broadcasted_iota(jnp.int32, (cap, ck), 0)
    lane = lax.broadcasted_iota(jnp.int32, (cap, EXPERT_PAD), 1)
    idx_cols = jnp.zeros((cap, EXPERT_PAD), F32)
    for e in range(N_EXPERTS):
        r = jnp.zeros((cap, LANES), F32)
        for c in range(seq // ck):
            onehot = (slot_iota == slot_i[e:e + 1, c * ck:(c + 1) * ck]).astype(BF16)
            r = r + jnp.dot(onehot, tok_ref[c * ck:(c + 1) * ck, :], preferred_element_type=F32)
        idx_cols = jnp.where(lane == e, r[:, 0:1] * 64.0 + r[:, 1:2], idx_cols)
    idx_ref[...] = idx_cols.T[0:N_EXPERTS, :].astype(jnp.int32) + pl.program_id(0) * seq


def _router(xb3, wrt, state, cap):
    bsz, seq, d = xb3.shape
    tok = jnp.arange(seq, dtype=jnp.int32)
    tok_tab = jnp.zeros((seq, LANES), F32).at[:, 0].set((tok // 64).astype(F32)).at[:, 1].set((tok % 64).astype(F32))
    return pl.pallas_call(
        functools.partial(_router_body, cap),
        grid=(bsz,),
        in_specs=[pl.BlockSpec((None, seq, d), lambda b: (b, 0, 0)),
                  pl.BlockSpec((N_EXPERTS, d), lambda b: (0, 0)),
                  pl.BlockSpec((seq, LANES), lambda b: (0, 0)),
                  pl.BlockSpec(memory_space=pl.ANY)],
        out_specs=[pl.BlockSpec((None, N_EXPERTS, cap), lambda b: (b, 0, 0)),
                   pl.BlockSpec((seq, EXPERT_PAD), lambda b: (b, 2 * d // EXPERT_PAD))],
        out_shape=[jax.ShapeDtypeStruct((bsz, N_EXPERTS, cap), jnp.int32),
                   jax.ShapeDtypeStruct(state.shape, F32)],
        input_output_aliases={3: 1},
        compiler_params=_params("parallel"),
        name="router_topk",
    )(xb3, wrt, tok_tab.astype(BF16), state)


def _expert_body(d, idx_ref, state_in_hbm, wg_ref, wu_ref, wd_ref, state_hbm, rows_buf, sems):
    del state_in_hbm
    e = pl.program_id(0)
    n_chunks, rows, _ = rows_buf.shape

    def for_each_slot(fn):
        for p in range(rows):
            fn(p)

    def gather(c, p):
        row = idx_ref[0, c * rows + p]
        return pltpu.make_async_copy(state_hbm.at[pl.ds(row, 1), :], rows_buf.at[c].at[pl.ds(p, 1), :],
                                     sems.at[c, 0])

    def scatter(c, p):
        row = idx_ref[0, c * rows + p]
        return pltpu.make_async_copy(rows_buf.at[c].at[pl.ds(p, 1), pl.ds(d, d)],
                                     state_hbm.at[pl.ds(row, 1), pl.ds(d, d)], sems.at[c, 1])

    for c in range(n_chunks):
        for_each_slot(lambda p, c=c: gather(c, p).start(priority=p % 2))
    for c in range(n_chunks):
        for_each_slot(lambda p, c=c: gather(c, p).wait())
        x = rows_buf[c, :, 0:d].astype(BF16)
        g = jnp.dot(x, wg_ref[...], preferred_element_type=F32)
        u = jnp.dot(x, wu_ref[...], preferred_element_type=F32)
        h = (g / (1.0 + jnp.exp(-g))) * u
        y = jnp.dot(h.astype(BF16), wd_ref[...], preferred_element_type=F32)
        lane = lax.broadcasted_iota(jnp.int32, (rows, EXPERT_PAD), 1)
        gates = rows_buf[c, :, 2 * d:2 * d + EXPERT_PAD]
        gate_col = jnp.sum(jnp.where(lane == e, gates, 0.0), axis=1, keepdims=True)
        rows_buf[c, :, d:2 * d] = rows_buf[c, :, d:2 * d] + gate_col * y
        for_each_slot(lambda p, c=c: scatter(c, p).start(priority=p % 2))
    for c in range(n_chunks):
        for_each_slot(lambda p, c=c: scatter(c, p).wait())


def _expert_ffn(idx, state, layer, wg, wu, wd):
    bsz, n_exp, cap = idx.shape
    m_tot, width = state.shape
    d = (width - EXPERT_PAD) // 2
    f = wg.shape[3]
    n_chunks = EXPERT_SLOT_CHUNKS
    any_spec = pl.BlockSpec(memory_space=pl.ANY)
    return pl.pallas_call(
        functools.partial(_expert_body, d),
        grid=(n_exp, bsz),
        in_specs=[pl.BlockSpec((None, 1, cap), lambda e, b: (b * n_exp + e, 0, 0), memory_space=pltpu.SMEM),
                  any_spec,
                  pl.BlockSpec((None, None, d, f), lambda e, b: (layer, e, 0, 0)),
                  pl.BlockSpec((None, None, d, f), lambda e, b: (layer, e, 0, 0)),
                  pl.BlockSpec((None, None, f, d), lambda e, b: (layer, e, 0, 0))],
        out_specs=any_spec,
        out_shape=jax.ShapeDtypeStruct((m_tot, width), F32),
        scratch_shapes=[pltpu.VMEM((n_chunks, cap // n_chunks, width), F32),
                        pltpu.SemaphoreType.DMA((n_chunks, 2))],
        input_output_aliases={1: 0},
        compiler_params=_params("arbitrary", "arbitrary"),
        name="expert_swiglu_scatter",
    )(idx.reshape(bsz * n_exp, 1, cap), state, wg, wu, wd)


def _final_norm_body(acc_ref, gain_ref, bias_ref, xo_ref, xb_ref):
    out = _layer_norm_rows(acc_ref[...], gain_ref[...], bias_ref[...])
    xo_ref[...] = out
    xb_ref[...] = out.astype(BF16)


def _final_norm(state, gain, bias, seq):
    m_tot = state.shape[0]
    d = gain.shape[1]
    tm = min(512, seq)
    row = pl.BlockSpec((tm, d), lambda m: (m, 0))
    vec = pl.BlockSpec((1, d), lambda m: (0, 0))
    return pl.pallas_call(
        _final_norm_body,
        grid=(m_tot // tm,),
        in_specs=[pl.BlockSpec((tm, d), lambda m: (m, 1)), vec, vec],
        out_specs=[row, row],
        out_shape=[jax.ShapeDtypeStruct((m_tot, d), F32), jax.ShapeDtypeStruct((m_tot, d), BF16)],
        compiler_params=_params("parallel"),
        name="channel_mixer_layernorm",
    )(state, gain, bias)


def _offsets(widths):
    offs = [0]
    for w in widths:
        offs.append(offs[-1] + w)
    return offs


def kernel(x, w_in, b_gate, w_branch, w_out, diff_lambda, diff_subln, sink_logit, na_rpb,
           w_router, w_exp_gate, w_exp_up, w_exp_down, ln_gain, ln_bias):
    bsz, seq, d = x.shape
    depth = w_in.shape[0]
    m_tot = bsz * seq
    alpha = (2.0 * depth) ** 0.25
    cap = EC_CAPACITY_FACTOR * seq // N_EXPERTS
    in_widths = (A_W, A_W, A_W, B_W, B_W, B_W, C_Q_W, C_KV_W, C_KV_W, D_W, D_W, D_W, N_BRANCHES * d)
    io = _offsets(in_widths)
    bo = _offsets((A_OUT_W, B_W, C_Q_W, D_W))
    cos128, sin128 = _rope_tables(seq, HEAD_DIM // 2)
    cos64, sin64 = _rope_tables(seq, B_DIM // 2)

    wg_stack, wu_stack, wd_stack = (w.astype(BF16) for w in (w_exp_gate, w_exp_up, w_exp_down))
    na_bias = _na_bias_tables(na_rpb.reshape((depth * D_HEADS,) + na_rpb.shape[2:]), seq // GRID_W)
    xf = x.reshape(m_tot, d)
    xb = xf.astype(BF16)
    for l in range(depth):
        lambda_init = 0.8 - 0.6 * math.exp(-0.3 * l)
        def proj(a, b, mode, tn, extra=(), **q_fold):
            return _proj(xb, w_in, l, io[a], io[b] - io[a], mode, tn, seq, extra, **q_fold)

        qk_b = proj(3, 5, "rope64", B_W, (cos64, sin64), q_cols=B_W, q_scale=DIFF_Q_SCALE)
        v_b = proj(5, 6, "plain", B_W)
        qk_c = proj(6, 8, "rope128", C_Q_W + C_KV_W, (cos128, sin128), q_cols=C_Q_W, q_scale=ATTN_Q_SCALE)
        v_c = proj(8, 9, "plain", C_KV_W)
        qkv_d = proj(9, 12, "plain", D_W, q_cols=D_W, q_scale=ATTN_Q_SCALE)
        gates = proj(12, 13, "gate", min(1024, d), (b_gate[l].reshape(1, -1),))

        outs, lses = [], []
        for g, (w, r) in enumerate(DIL_PAIRS):
            qkv_g = _proj_dilated(xb, w_in, l, g, r, bsz, seq, cos128, sin128)
            o_g, lse_g = _banded(qkv_g, qkv_g, qkv_g, 0, 1, 2, hq=A_HEADS_PER_GROUP, group=1,
                                 halo=(w // 2) // r, want_lse=True)
            outs.append(o_g)
            lses.append(lse_g)
        o_a = _group_mix(outs, lses, seq)

        o_b = _diff_attention(qk_b.reshape(bsz, seq, 2 * B_W), v_b.reshape(bsz, seq, B_W),
                              diff_lambda[l], diff_subln[l].reshape(1, -1), lambda_init)

        qk_c4 = qk_c.reshape(bsz, 1, seq, C_Q_W + C_KV_W)
        (o_c,) = _banded(qk_c4, qk_c4, v_c.reshape(bsz, 1, seq, C_KV_W), 0, C_Q_W // C_KV_W, 0,
                         hq=C_Q_HEADS, group=C_Q_HEADS // C_KV_HEADS, halo=C_HALF_WINDOW, sink=sink_logit[l])

        o_d = _neighborhood_attention(qkv_d.reshape(bsz, seq, 3 * D_W), na_bias, l)

        wb = w_branch[l].astype(BF16)
        mixed = _merge([o_a, o_b.reshape(m_tot, B_W), o_c.reshape(m_tot, C_Q_W), o_d.reshape(m_tot, D_W)],
                       [wb[bo[i]:bo[i + 1]] for i in range(N_BRANCHES)], gates, seq)
        xb, state = _out_proj(mixed, w_out[l].astype(BF16), xf, ln_gain[l, 0].reshape(1, -1),
                              ln_bias[l, 0].reshape(1, -1), alpha, seq)

        idx, state = _router(xb.reshape(bsz, seq, d), w_router[l].T.astype(BF16), state, cap)
        state = _expert_ffn(idx, state, l, wg_stack, wu_stack, wd_stack)
        xf, xb = _final_norm(state, ln_gain[l, 1].reshape(1, -1), ln_bias[l, 1].reshape(1, -1), seq)
    return xf.reshape(bsz, seq, d)
```

```python
import functools
import math

import jax
import jax.numpy as jnp
from jax import lax
from jax.experimental import pallas as pl
from jax.experimental.pallas import tpu as pltpu

F32 = jnp.float32
BF16 = jnp.bfloat16

LANES = 128
HEAD_DIM = 128
ROPE_THETA = 10000.0
NEG_INF = -1e30
LN_EPS = 1e-5
SUBLN_EPS = 1e-5
DIL_PAIRS = ((128, 1), (512, 4), (2048, 16))
A_HEADS_PER_GROUP = 6
A_GROUPS = len(DIL_PAIRS)
A_HEADS = A_GROUPS * A_HEADS_PER_GROUP
B_HEADS = 8
B_DIM = 64
C_Q_HEADS = 8
C_KV_HEADS = 2
C_HALF_WINDOW = 128
D_HEADS = 8
GRID_W = 64
NA_WIN_H = 8
NA_WIN_W = 16
BANDED_TILES_PER_STEP = 2
NA_Q_ROWS = 8
NA_K_ROWS = 16
NA_BLOCKS_PER_STEP = 8
DIFF_Q_ROWS = 1024
DIFF_SUB_ROWS = 256
LOG2_E = math.log2(math.e)
LN_2 = math.log(2.0)
DIFF_Q_SCALE = (B_DIM ** -0.5) * LOG2_E
ATTN_Q_SCALE = (HEAD_DIM ** -0.5) * LOG2_E
N_BRANCHES = 4
N_EXPERTS = 16
EC_CAPACITY_FACTOR = 2
EXPERT_PAD = 128
EXPERT_SLOT_CHUNKS = 2

A_W = A_HEADS * HEAD_DIM
B_W = B_HEADS * 2 * B_DIM
C_Q_W = C_Q_HEADS * HEAD_DIM
C_KV_W = C_KV_HEADS * HEAD_DIM
D_W = D_HEADS * HEAD_DIM
A_OUT_W = A_HEADS_PER_GROUP * HEAD_DIM

VMEM_LIMIT = 56 * 1024 * 1024


def _params(*sem):
    return pltpu.CompilerParams(dimension_semantics=sem, vmem_limit_bytes=VMEM_LIMIT)


def _dot_nt(a, b):
    return lax.dot_general(a, b, (((1,), (1,)), ((), ())), preferred_element_type=F32)


def _cast_weight_tile(w_ref, wb_ref):
    @pl.when(pl.program_id(1) == 0)
    def _():
        wb_ref[...] = w_ref[...].astype(BF16)
    return wb_ref[...]


def _proj_body(mode, q_cols, q_scale, x_ref, w_ref, *rest):
    rest, wb_ref = rest[:-1], rest[-1]
    acc = jnp.dot(x_ref[...], _cast_weight_tile(w_ref, wb_ref), preferred_element_type=F32)
    tn = acc.shape[1]

    def chunk_scale(c):
        return jnp.where(pl.program_id(0) * tn + c * LANES < q_cols, q_scale, 1.0).astype(F32)

    if mode == "plain":
        (o_ref,) = rest
        if q_cols:
            for c in range(tn // LANES):
                cs = slice(c * LANES, (c + 1) * LANES)
                o_ref[:, cs] = (acc[:, cs] * chunk_scale(c)).astype(o_ref.dtype)
        else:
            o_ref[...] = acc.astype(o_ref.dtype)
    elif mode == "gate":
        b_ref, o_ref = rest
        z = acc + b_ref[...]
        o_ref[...] = (1.0 / (1.0 + jnp.exp(-z))).astype(o_ref.dtype)
    else:
        cos_ref, sin_ref, o_ref = rest
        cos = cos_ref[...]
        sin = sin_ref[...]
        lane = lax.broadcasted_iota(jnp.int32, cos.shape, 1)
        for c in range(tn // LANES):
            a = acc[:, c * LANES:(c + 1) * LANES]
            if mode == "rope128":
                rot = pltpu.roll(a, 64, 1)
            else:
                rot = jnp.where((lane % 64) < 32, pltpu.roll(a, 96, 1), pltpu.roll(a, 32, 1))
            qs = chunk_scale(c)
            o_ref[:, c * LANES:(c + 1) * LANES] = (a * (cos * qs) + rot * (sin * qs)).astype(o_ref.dtype)


def _weight_tile_spec(k, tn, layer, lane_tile_of):
    return pl.BlockSpec((pl.Squeezed(), pl.Element(k), pl.Element(tn)),
                        lambda n, m: (layer, 0, lane_tile_of(n) * LANES))


def _proj(xb, w_stack, layer, col0, n_tot, mode, tn, seq, extra=(), q_cols=0, q_scale=1.0):
    m_tot, k = xb.shape
    tm = min(1024, seq)
    pos_blocks = seq // tm
    in_specs = [pl.BlockSpec((tm, k), lambda n, m: (m, 0)),
                _weight_tile_spec(k, tn, layer, lambda n: col0 // LANES + n * (tn // LANES))]
    if mode == "gate":
        in_specs.append(pl.BlockSpec((1, tn), lambda n, m: (0, n)))
    elif mode != "plain":
        in_specs += [pl.BlockSpec((tm, LANES), lambda n, m: (m % pos_blocks, 0))] * 2
    return pl.pallas_call(
        functools.partial(_proj_body, mode, q_cols, q_scale),
        grid=(n_tot // tn, m_tot // tm),
        in_specs=in_specs,
        out_specs=pl.BlockSpec((tm, tn), lambda n, m: (m, n)),
        out_shape=jax.ShapeDtypeStruct((m_tot, n_tot), BF16),
        scratch_shapes=[pltpu.VMEM((k, tn), BF16)],
        compiler_params=_params("parallel", "arbitrary"),
        name="proj_" + mode,
    )(xb, w_stack, *extra)


def _proj_dilated_body(dil, x_ref, w_ref, cos_ref, sin_ref, o_ref, wb_ref, *scratch):
    n = pl.program_id(0)
    tm = x_ref.shape[0]
    rows = tm // dil
    acc = jnp.dot(x_ref[...], _cast_weight_tile(w_ref, wb_ref), preferred_element_type=F32)
    dst = scratch[0] if dil > 1 else None

    def emit(c, val):
        if dil > 1:
            dst[c] = val
        else:
            o_ref[0, :, c * LANES:(c + 1) * LANES] = val.astype(o_ref.dtype)

    @pl.when(n < 2)
    def _():
        qs = jnp.where(n == 0, ATTN_Q_SCALE, 1.0).astype(F32)
        cos = cos_ref[...] * qs
        sin = sin_ref[...] * qs
        for c in range(acc.shape[1] // LANES):
            a = acc[:, c * LANES:(c + 1) * LANES]
            emit(c, a * cos + pltpu.roll(a, 64, 1) * sin)

    @pl.when(n >= 2)
    def _():
        for c in range(acc.shape[1] // LANES):
            emit(c, acc[:, c * LANES:(c + 1) * LANES])

    if dil > 1:
        for c in range(acc.shape[1] // LANES):
            for j in range(dil):
                o_ref[j, :, c * LANES:(c + 1) * LANES] = (
                    dst.at[c][pl.ds(j, rows, stride=dil), :].astype(o_ref.dtype))


def _proj_dilated(xb, w_stack, layer, group, dil, bsz, seq, cos, sin):
    m_tot, k = xb.shape
    tm = min(1024, seq)
    tn = A_OUT_W
    pos_blocks = seq // tm
    scratch = [pltpu.VMEM((k, tn), BF16)]
    if dil > 1:
        scratch.append(pltpu.VMEM((tn // LANES, tm, LANES), F32))
    return pl.pallas_call(
        functools.partial(_proj_dilated_body, dil),
        grid=(3, m_tot // tm),
        in_specs=[pl.BlockSpec((tm, k), lambda n, m: (m, 0)),
                  _weight_tile_spec(k, tn, layer, lambda n: n * (A_W // LANES) + group * (tn // LANES)),
                  pl.BlockSpec((tm, LANES), lambda n, m: (m % pos_blocks, 0)),
                  pl.BlockSpec((tm, LANES), lambda n, m: (m % pos_blocks, 0))],
        out_specs=pl.BlockSpec((None, dil, tm // dil, tn), lambda n, m: (m // pos_blocks, 0, m % pos_blocks, n)),
        out_shape=jax.ShapeDtypeStruct((bsz, dil, seq // dil, 3 * tn), BF16),
        scratch_shapes=scratch,
        compiler_params=_params("parallel", "arbitrary"),
        name="proj_dilated",
    )(xb, w_stack, cos, sin)


def _rope_tables(seq, half):
    inv = ROPE_THETA ** (-jnp.arange(half, dtype=F32) / half)
    ang = jnp.arange(seq, dtype=jnp.int32).astype(F32)[:, None] * inv[None, :]
    cos, sin = jnp.cos(ang), jnp.sin(ang)
    reps = LANES // (2 * half)
    return (jnp.tile(jnp.concatenate([cos, cos], axis=1), (1, reps)),
            jnp.tile(jnp.concatenate([-sin, sin], axis=1), (1, reps)))


def _banded_body(hq, group, halo, tq, win, length, has_sink, want_lse, *refs):
    refs = list(refs)
    q_ref, k_ref, v_ref = refs[:3]
    pos = 3
    sink_ref = None
    if has_sink:
        sink_ref = refs[pos]
        pos += 1
    o_ref = refs[pos]
    lse_ref = refs[pos + 1] if want_lse else None

    lane = lax.broadcasted_iota(jnp.int32, (tq, LANES), 1)
    for u in range(q_ref.shape[0] // tq):
        i = pl.program_id(2) * (q_ref.shape[0] // tq) + u
        rows = slice(u * tq, (u + 1) * tq)
        start = pl.multiple_of(jnp.clip(i * tq - halo, 0, length - win), 64)
        qpos = i * tq + lax.broadcasted_iota(jnp.int32, (tq, win), 0)
        kpos = start + lax.broadcasted_iota(jnp.int32, (tq, win), 1)
        valid = jnp.abs(qpos - kpos) <= halo
        lse_acc = jnp.zeros((tq, LANES), F32)
        for h in range(hq):
            kh = h // group
            q = q_ref[rows, h * HEAD_DIM:(h + 1) * HEAD_DIM]
            k = k_ref[pl.ds(start, win), kh * HEAD_DIM:(kh + 1) * HEAD_DIM]
            v = v_ref[pl.ds(start, win), kh * HEAD_DIM:(kh + 1) * HEAD_DIM]
            s = jnp.where(valid, _dot_nt(q, k), NEG_INF)
            m = jnp.max(s, axis=-1, keepdims=True)
            if has_sink:
                sk = sink_ref[h] * LOG2_E
                m = jnp.maximum(m, sk)
            p = jnp.exp2(s - m)
            d = jnp.sum(p, axis=-1, keepdims=True)
            if has_sink:
                d = d + jnp.exp2(sk - m)
            o = jnp.dot(p.astype(BF16), v, preferred_element_type=F32) / d
            o_ref[rows, h * HEAD_DIM:(h + 1) * HEAD_DIM] = o.astype(o_ref.dtype)
            if want_lse:
                lse_acc = jnp.where(lane == h, (m + jnp.log2(d)) * LN_2, lse_acc)
        if want_lse:
            lse_ref[rows, :] = lse_acc


def _banded(q4, k4, v4, qcol, kcol, vcol, *, hq, group, halo, sink=None, want_lse=False):
    bsz, dil, length, _ = q4.shape
    hkv = hq // group
    tq = min(256, length)
    win = min(length, tq + 2 * halo)
    qw, kw = hq * HEAD_DIM, hkv * HEAD_DIM
    step_rows = tq * min(BANDED_TILES_PER_STEP, length // tq)
    in_specs = [pl.BlockSpec((None, None, step_rows, qw), lambda b, j, i: (b, j, i, qcol)),
                pl.BlockSpec((None, None, length, kw), lambda b, j, i: (b, j, 0, kcol)),
                pl.BlockSpec((None, None, length, kw), lambda b, j, i: (b, j, 0, vcol))]
    args = [q4, k4, v4]
    if sink is not None:
        in_specs.append(pl.BlockSpec(memory_space=pltpu.SMEM))
        args.append(sink)
    out_specs = [pl.BlockSpec((None, None, step_rows, qw), lambda b, j, i: (b, j, i, 0))]
    out_shape = [jax.ShapeDtypeStruct((bsz, dil, length, qw), BF16)]
    if want_lse:
        out_specs.append(pl.BlockSpec((None, None, step_rows, LANES), lambda b, j, i: (b, j, i, 0)))
        out_shape.append(jax.ShapeDtypeStruct((bsz, dil, length, LANES), F32))
    return pl.pallas_call(
        functools.partial(_banded_body, hq, group, halo, tq, win, length, sink is not None, want_lse),
        grid=(bsz, dil, length // step_rows),
        in_specs=in_specs,
        out_specs=out_specs,
        out_shape=out_shape,
        compiler_params=_params("parallel", "parallel", "parallel"),
        name="banded_attention",
    )(*args)


def _group_mix_body(dils, *refs):
    n = len(dils)
    o_refs, l_refs, out_ref = refs[:n], refs[n:2 * n], refs[2 * n]
    scratch = list(refs[2 * n + 1:])
    tm = out_ref.shape[0]
    heads, lses = [], []
    for g, dil in enumerate(dils):
        if dil == 1:
            heads.append(functools.partial(
                lambda h, ref: ref[0, :, h * HEAD_DIM:(h + 1) * HEAD_DIM].astype(F32), ref=o_refs[g]))
            lses.append(l_refs[g][0])
            continue
        o_scr, l_scr = scratch.pop(0), scratch.pop(0)
        rows = tm // dil
        for j in range(dil):
            l_scr[pl.ds(j, rows, stride=dil), :] = l_refs[g][j]
            for h in range(A_HEADS_PER_GROUP):
                o_scr.at[h][pl.ds(j, rows, stride=dil), :] = (
                    o_refs[g][j, :, h * HEAD_DIM:(h + 1) * HEAD_DIM].astype(F32))
        heads.append(functools.partial(lambda h, ref: ref[h], ref=o_scr))
        lses.append(l_scr[...])
    m = functools.reduce(jnp.maximum, lses)
    es = [jnp.exp(l - m) for l in lses]
    tot = functools.reduce(lambda a, b: a + b, es)
    ws = [e / tot for e in es]
    for h in range(A_HEADS_PER_GROUP):
        acc = ws[0][:, h:h + 1] * heads[0](h)
        for g in range(1, n):
            acc = acc + ws[g][:, h:h + 1] * heads[g](h)
        out_ref[:, h * HEAD_DIM:(h + 1) * HEAD_DIM] = acc.astype(out_ref.dtype)


def _group_mix(outs, lses, seq):
    bsz = outs[0].shape[0]
    dils = tuple(o.shape[1] for o in outs)
    tm = min(512, seq)
    pos_blocks = seq // tm

    def spec(dil, width):
        return pl.BlockSpec((None, dil, tm // dil, width), lambda m: (m // pos_blocks, 0, m % pos_blocks, 0))

    scratch = []
    for dil in dils:
        if dil > 1:
            scratch += [pltpu.VMEM((A_HEADS_PER_GROUP, tm, HEAD_DIM), F32), pltpu.VMEM((tm, LANES), F32)]
    return pl.pallas_call(
        functools.partial(_group_mix_body, dils),
        grid=(bsz * pos_blocks,),
        in_specs=[spec(dil, A_OUT_W) for dil in dils] + [spec(dil, LANES) for dil in dils],
        out_specs=pl.BlockSpec((tm, A_OUT_W), lambda m: (m, 0)),
        out_shape=jax.ShapeDtypeStruct((bsz * seq, A_OUT_W), BF16),
        scratch_shapes=scratch,
        compiler_params=_params("parallel"),
        name="dilation_group_mix",
    )(*outs, *lses)


def _diff_body(lambda_init, lam_ref, subln_ref, q_ref, k_ref, v_ref, o_ref):
    lam = lam_ref[...]
    dot1 = jnp.sum(lam[0:1, :] * lam[1:2, :], axis=1, keepdims=True)
    dot2 = jnp.sum(lam[2:3, :] * lam[3:4, :], axis=1, keepdims=True)
    lmbda = jnp.exp(dot1) - jnp.exp(dot2) + lambda_init
    k = k_ref[...]
    v = v_ref[...]
    sub = min(DIFF_SUB_ROWS, q_ref.shape[0])
    lane = lax.broadcasted_iota(jnp.int32, (sub, 2 * B_DIM), 1)
    zero = jnp.zeros((sub, 2 * B_DIM), BF16)

    def softmax_map_times_v(qm):
        s = _dot_nt(qm, k)
        p = jnp.exp2(s - jnp.max(s, axis=-1, keepdims=True))
        return jnp.dot(p.astype(BF16), v, preferred_element_type=F32) / jnp.sum(p, axis=-1, keepdims=True)

    for r0 in range(0, q_ref.shape[0], sub):
        q = q_ref[r0:r0 + sub, :]
        o = (softmax_map_times_v(jnp.where(lane < B_DIM, q, zero))
             - lmbda * softmax_map_times_v(jnp.where(lane >= B_DIM, q, zero)))
        o = o * lax.rsqrt(jnp.mean(o * o, axis=-1, keepdims=True) + SUBLN_EPS) * subln_ref[...]
        o_ref[r0:r0 + sub, :] = (o * (1.0 - lambda_init)).astype(o_ref.dtype)


def _diff_attention(qk3, v3, lam, subln, lambda_init):
    bsz, seq, _ = qk3.shape
    tq = min(DIFF_Q_ROWS, seq)
    hw = 2 * B_DIM
    return pl.pallas_call(
        functools.partial(_diff_body, lambda_init),
        grid=(bsz, B_HEADS, seq // tq),
        in_specs=[pl.BlockSpec((4, B_DIM), lambda b, h, i: (0, 0)),
                  pl.BlockSpec((1, hw), lambda b, h, i: (0, 0)),
                  pl.BlockSpec((None, tq, hw), lambda b, h, i: (b, i, h)),
                  pl.BlockSpec((None, seq, hw), lambda b, h, i: (b, 0, B_HEADS + h)),
                  pl.BlockSpec((None, seq, hw), lambda b, h, i: (b, 0, h))],
        out_specs=pl.BlockSpec((None, tq, hw), lambda b, h, i: (b, i, h)),
        out_shape=jax.ShapeDtypeStruct((bsz, seq, B_W), BF16),
        compiler_params=_params("parallel", "parallel", "parallel"),
        name="diff_attention",
    )(lam, subln, qk3, qk3, v3)


def _na_body(rows, k_rows, blocks, q_ref, k_ref, v_ref, bias_ref, o_ref):
    n_rb = rows // NA_Q_ROWS
    tq = NA_Q_ROWS * GRID_W
    for u in range(blocks):
        rb = pl.program_id(2) * blocks + u
        w0 = jnp.clip(rb * NA_Q_ROWS - NA_WIN_H // 2, 0, rows - k_rows) * GRID_W
        w0 = pl.multiple_of(w0, 256)
        kind = jnp.where(rb == 0, 0, jnp.where(rb == n_rb - 1, 2, 1))
        k = k_ref[pl.ds(w0, k_rows * GRID_W), :]
        v = v_ref[pl.ds(w0, k_rows * GRID_W), :]
        s = _dot_nt(q_ref[u * tq:(u + 1) * tq, :], k) + bias_ref[kind]
        p = jnp.exp2(s - jnp.max(s, axis=-1, keepdims=True))
        d = jnp.sum(p, axis=-1, keepdims=True)
        o = jnp.dot(p.astype(BF16), v, preferred_element_type=F32) / d
        o_ref[u * tq:(u + 1) * tq, :] = o.astype(o_ref.dtype)


def _na_bias_tables(rpb, rows):
    kh = min(NA_WIN_H, rows)
    k_rows = min(NA_K_ROWS, rows)
    n_rb = rows // NA_Q_ROWS
    n_heads = rpb.shape[0]
    edge = GRID_W - NA_WIN_W
    ext = jnp.pad(rpb.astype(F32) * LOG2_E, ((0, 0), (0, 0), (edge, edge)), mode="edge")
    toeplitz = jnp.stack([ext[:, :, GRID_W - 1 - qc:2 * GRID_W - 1 - qc] for qc in range(GRID_W)], axis=2)
    c = jnp.arange(GRID_W)
    col_start = jnp.clip(c - NA_WIN_W // 2, 0, GRID_W - NA_WIN_W)
    col_ok = (c[None, :] >= col_start[:, None]) & (c[None, :] < col_start[:, None] + NA_WIN_W)
    toeplitz = jnp.where(col_ok[None, None], toeplitz, NEG_INF)
    tabs = []
    for rb in (0, min(1, n_rb - 1), n_rb - 1):
        r0 = rb * NA_Q_ROWS
        w0 = min(max(r0 - NA_WIN_H // 2, 0), rows - k_rows)
        strips = []
        for a in range(NA_Q_ROWS):
            qrow = r0 + a
            start = min(max(qrow - kh // 2, 0), rows - kh)
            first = start - qrow + (NA_WIN_H - 1)
            blk = toeplitz[:, first:first + kh].transpose(0, 2, 1, 3)
            blk = jnp.pad(blk, ((0, 0), (0, 0), (start - w0, k_rows - kh - (start - w0)), (0, 0)),
                          constant_values=NEG_INF)
            strips.append(blk.reshape(n_heads, GRID_W, k_rows * GRID_W))
        tabs.append(jnp.concatenate(strips, axis=1))
    return jnp.stack(tabs, axis=1)


def _neighborhood_attention(qkv3, bias, layer):
    bsz, seq, _ = qkv3.shape
    rows = seq // GRID_W
    k_rows = min(NA_K_ROWS, rows)
    tq = NA_Q_ROWS * GRID_W
    blocks = min(NA_BLOCKS_PER_STEP, rows // NA_Q_ROWS)
    step_rows = blocks * tq
    return pl.pallas_call(
        functools.partial(_na_body, rows, k_rows, blocks),
        grid=(bsz, D_HEADS, seq // step_rows),
        in_specs=[pl.BlockSpec((None, step_rows, HEAD_DIM), lambda b, h, r: (b, r, h)),
                  pl.BlockSpec((None, seq, HEAD_DIM), lambda b, h, r: (b, 0, D_HEADS + h)),
                  pl.BlockSpec((None, seq, HEAD_DIM), lambda b, h, r: (b, 0, 2 * D_HEADS + h)),
                  pl.BlockSpec((None, 3, tq, k_rows * GRID_W), lambda b, h, r: (layer * D_HEADS + h, 0, 0, 0))],
        out_specs=pl.BlockSpec((None, step_rows, HEAD_DIM), lambda b, h, r: (b, r, h)),
        out_shape=jax.ShapeDtypeStruct((bsz, seq, D_W), BF16),
        compiler_params=_params("parallel", "parallel", "parallel"),
        name="neighborhood_attention",
    )(qkv3, qkv3, qkv3, bias)


def _merge_body(oa_ref, ob_ref, oc_ref, od_ref, wa_ref, wb_ref, wc_ref, wd_ref,
                ga_ref, gb_ref, gc_ref, gd_ref, out_ref):
    acc = ga_ref[...].astype(F32) * jnp.dot(oa_ref[...], wa_ref[...], preferred_element_type=F32)
    acc = acc + gb_ref[...].astype(F32) * jnp.dot(ob_ref[...], wb_ref[...], preferred_element_type=F32)
    acc = acc + gc_ref[...].astype(F32) * jnp.dot(oc_ref[...], wc_ref[...], preferred_element_type=F32)
    acc = acc + gd_ref[...].astype(F32) * jnp.dot(od_ref[...], wd_ref[...], preferred_element_type=F32)
    out_ref[...] = acc.astype(out_ref.dtype)


def _merge(branches, weights, gates, seq):
    m_tot = branches[0].shape[0]
    d = weights[0].shape[1]
    tm = min(1024, seq)
    tn = min(512, d)
    n_blocks = d // tn
    in_specs = [pl.BlockSpec((tm, o.shape[1]), lambda m, n: (m, 0)) for o in branches]
    in_specs += [pl.BlockSpec((w.shape[0], tn), lambda m, n: (0, n)) for w in weights]
    in_specs += [pl.BlockSpec((tm, tn), functools.partial(lambda m, n, i: (m, i * n_blocks + n), i=i))
                 for i in range(N_BRANCHES)]
    return pl.pallas_call(
        _merge_body,
        grid=(m_tot // tm, n_blocks),
        in_specs=in_specs,
        out_specs=pl.BlockSpec((tm, tn), lambda m, n: (m, n)),
        out_shape=jax.ShapeDtypeStruct((m_tot, d), BF16),
        compiler_params=_params("parallel", "parallel"),
        name="branch_merge",
    )(*branches, *weights, *([gates] * N_BRANCHES))


def _layer_norm_rows(y, gain, bias):
    mu = jnp.mean(y, axis=-1, keepdims=True)
    var = jnp.mean(jnp.square(y - mu), axis=-1, keepdims=True)
    return (y - mu) * lax.rsqrt(var + LN_EPS) * gain + bias


def _out_proj_body(alpha, mixed_ref, w_ref, x_ref, gain_ref, bias_ref, xb_ref, state_ref):
    d = x_ref.shape[1]
    y = alpha * x_ref[...] + jnp.dot(mixed_ref[...], w_ref[...], preferred_element_type=F32)
    out = _layer_norm_rows(y, gain_ref[...], bias_ref[...])
    xb_ref[...] = out.astype(BF16)
    state_ref[:, 0:d] = out
    state_ref[:, d:2 * d] = alpha * out
    state_ref[:, 2 * d:] = jnp.zeros((out.shape[0], EXPERT_PAD), F32)


def _out_proj(mixed, w, x, gain, bias, alpha, seq):
    m_tot, d = x.shape
    tm = min(512, seq)
    row = pl.BlockSpec((tm, d), lambda m: (m, 0))
    vec = pl.BlockSpec((1, d), lambda m: (0, 0))
    w_spec = pl.BlockSpec((d, d), lambda m: (0, 0), pipeline_mode=pl.Buffered(1))
    return pl.pallas_call(
        functools.partial(_out_proj_body, alpha),
        grid=(m_tot // tm,),
        in_specs=[row, w_spec, row, vec, vec],
        out_specs=[row, pl.BlockSpec((tm, 2 * d + EXPERT_PAD), lambda m: (m, 0))],
        out_shape=[jax.ShapeDtypeStruct((m_tot, d), BF16),
                   jax.ShapeDtypeStruct((m_tot, 2 * d + EXPERT_PAD), F32)],
        compiler_params=_params("parallel"),
        name="out_proj_layernorm",
    )(mixed, w, x, gain, bias)


def _router_body(cap, x_ref, wrt_ref, tok_ref, state_in_hbm, idx_ref, gate_ref):
    del state_in_hbm
    seq = x_ref.shape[0]
    logits = _dot_nt(wrt_ref[...], x_ref[...])
    e = jnp.exp(logits - jnp.max(logits, axis=0, keepdims=True))
    aff = e / jnp.sum(e, axis=0, keepdims=True)
    bits = lax.bitcast_convert_type(aff, jnp.int32)
    thr = jnp.zeros((N_EXPERTS, 1), jnp.int32)
    for bit in range(30, -1, -1):
        cand = thr | (1 << bit)
        cnt = jnp.sum((bits >= cand).astype(F32), axis=1, keepdims=True)
        thr = jnp.where(cnt >= cap, cand, thr)
    above = bits > thr
    tied = bits == thr
    need = cap - jnp.sum(above.astype(F32), axis=1, keepdims=True)

    chunk = min(512, seq)
    upper = (lax.broadcasted_iota(jnp.int32, (chunk, chunk), 0)
             <= lax.broadcasted_iota(jnp.int32, (chunk, chunk), 1)).astype(BF16)

    def prefix_count(mask):
        parts, carry = [], jnp.zeros((N_EXPERTS, 1), F32)
        for c in range(seq // chunk):
            part = jnp.dot(mask[:, c * chunk:(c + 1) * chunk].astype(BF16), upper,
                           preferred_element_type=F32) + carry
            parts.append(part)
            carry = part[:, chunk - 1:chunk]
        return jnp.concatenate(parts, axis=1)

    sel = above | (tied & (prefix_count(tied) <= need))
    slot = jnp.where(sel, prefix_count(sel) - 1.0, -1.0)
    gate = jnp.where(sel, aff, 0.0)
    pad = EXPERT_PAD - N_EXPERTS
    gate_ref[...] = jnp.concatenate([gate, jnp.zeros((pad, seq), F32)], axis=0).T

    slot_i = slot.astype(jnp.int32)
    ck = min(1024, seq)
    slot_iota = lax.broadcasted_iota(jnp.int32, (cap, ck), 0)
    lane = lax.broadcasted_iota(jnp.int32, (cap, EXPERT_PAD), 1)
    idx_cols = jnp.zeros((cap, EXPERT_PAD), F32)
    for e in range(N_EXPERTS):
        r = jnp.zeros((cap, LANES), F32)
        for c in range(seq // ck):
            onehot = (slot_iota == slot_i[e:e + 1, c * ck:(c + 1) * ck]).astype(BF16)
            r = r + jnp.dot(onehot, tok_ref[c * ck:(c + 1) * ck, :], preferred_element_type=F32)
        idx_cols = jnp.where(lane == e, r[:, 0:1] * 64.0 + r[:, 1:2], idx_cols)
    idx_ref[...] = idx_cols.T[0:N_EXPERTS, :].astype(jnp.int32) + pl.program_id(0) * seq


def _router(xb3, wrt, state, cap):
    bsz, seq, d = xb3.shape
    tok = jnp.arange(seq, dtype=jnp.int32)
    tok_tab = jnp.zeros((seq, LANES), F32).at[:, 0].set((tok // 64).astype(F32)).at[:, 1].set((tok % 64).astype(F32))
    return pl.pallas_call(
        functools.partial(_router_body, cap),
        grid=(bsz,),
        in_specs=[pl.BlockSpec((None, seq, d), lambda b: (b, 0, 0)),
                  pl.BlockSpec((N_EXPERTS, d), lambda b: (0, 0)),
                  pl.BlockSpec((seq, LANES), lambda b: (0, 0)),
                  pl.BlockSpec(memory_space=pl.ANY)],
        out_specs=[pl.BlockSpec((None, N_EXPERTS, cap), lambda b: (b, 0, 0)),
                   pl.BlockSpec((seq, EXPERT_PAD), lambda b: (b, 2 * d // EXPERT_PAD))],
        out_shape=[jax.ShapeDtypeStruct((bsz, N_EXPERTS, cap), jnp.int32),
                   jax.ShapeDtypeStruct(state.shape, F32)],
        input_output_aliases={3: 1},
        compiler_params=_params("parallel"),
        name="router_topk",
    )(xb3, wrt, tok_tab.astype(BF16), state)


def _expert_body(d, layer, idx_ref, state_in_hbm, wg_hbm, wu_hbm, wd_hbm, state_hbm,
                 rows_buf, sems, wfg_ref, wfu_ref, wfd_ref, wbg_ref, wbu_ref, wbd_ref, wsems):
    del state_in_hbm
    wf_refs = (wfg_ref, wfu_ref, wfd_ref)
    wb_refs = (wbg_ref, wbu_ref, wbd_ref)
    e = pl.program_id(0)
    b = pl.program_id(1)
    n_exp = pl.num_programs(0)
    n_chunks, rows, _ = rows_buf.shape

    def weight_copies(expert):
        return [pltpu.make_async_copy(w.at[layer, expert], wf, wsems.at[i])
                for i, (w, wf) in enumerate(zip((wg_hbm, wu_hbm, wd_hbm), wf_refs))]

    @pl.when((e == 0) & (b == 0))
    def _():
        for cp in weight_copies(0):
            cp.start()

    @pl.when(b == 0)
    def _():
        for cp, wf, wb in zip(weight_copies(e), wf_refs, wb_refs):
            cp.wait()
            wb[...] = wf[...].astype(BF16)

        @pl.when(e + 1 < n_exp)
        def _():
            for cp in weight_copies(e + 1):
                cp.start()

    wg_ref, wu_ref, wd_ref = wb_refs

    def for_each_slot(fn):
        for p in range(rows):
            fn(p)

    def gather(c, p):
        row = idx_ref[0, c * rows + p]
        return pltpu.make_async_copy(state_hbm.at[pl.ds(row, 1), :], rows_buf.at[c].at[pl.ds(p, 1), :],
                                     sems.at[c, 0])

    def scatter(c, p):
        row = idx_ref[0, c * rows + p]
        return pltpu.make_async_copy(rows_buf.at[c].at[pl.ds(p, 1), pl.ds(d, d)],
                                     state_hbm.at[pl.ds(row, 1), pl.ds(d, d)], sems.at[c, 1])

    for c in range(n_chunks):
        for_each_slot(lambda p, c=c: gather(c, p).start(priority=p % 2))
    for c in range(n_chunks):
        for_each_slot(lambda p, c=c: gather(c, p).wait())
        x = rows_buf[c, :, 0:d].astype(BF16)
        g = jnp.dot(x, wg_ref[...], preferred_element_type=F32)
        u = jnp.dot(x, wu_ref[...], preferred_element_type=F32)
        h = (g / (1.0 + jnp.exp(-g))) * u
        y = jnp.dot(h.astype(BF16), wd_ref[...], preferred_element_type=F32)
        lane = lax.broadcasted_iota(jnp.int32, (rows, EXPERT_PAD), 1)
        gates = rows_buf[c, :, 2 * d:2 * d + EXPERT_PAD]
        gate_col = jnp.sum(jnp.where(lane == e, gates, 0.0), axis=1, keepdims=True)
        rows_buf[c, :, d:2 * d] = rows_buf[c, :, d:2 * d] + gate_col * y
        for_each_slot(lambda p, c=c: scatter(c, p).start(priority=p % 2))
    for c in range(n_chunks):
        for_each_slot(lambda p, c=c: scatter(c, p).wait())


def _expert_ffn(idx, state, layer, wg, wu, wd):
    bsz, n_exp, cap = idx.shape
    m_tot, width = state.shape
    d = (width - EXPERT_PAD) // 2
    f = wg.shape[3]
    n_chunks = EXPERT_SLOT_CHUNKS
    any_spec = pl.BlockSpec(memory_space=pl.ANY)
    return pl.pallas_call(
        functools.partial(_expert_body, d, layer),
        grid=(n_exp, bsz),
        in_specs=[pl.BlockSpec((None, 1, cap), lambda e, b: (b * n_exp + e, 0, 0), memory_space=pltpu.SMEM),
                  any_spec, any_spec, any_spec, any_spec],
        out_specs=any_spec,
        out_shape=jax.ShapeDtypeStruct((m_tot, width), F32),
        scratch_shapes=[pltpu.VMEM((n_chunks, cap // n_chunks, width), F32),
                        pltpu.SemaphoreType.DMA((n_chunks, 2)),
                        pltpu.VMEM((d, f), F32), pltpu.VMEM((d, f), F32), pltpu.VMEM((f, d), F32),
                        pltpu.VMEM((d, f), BF16), pltpu.VMEM((d, f), BF16), pltpu.VMEM((f, d), BF16),
                        pltpu.SemaphoreType.DMA((3,))],
        input_output_aliases={1: 0},
        compiler_params=_params("arbitrary", "arbitrary"),
        name="expert_swiglu_scatter",
    )(idx.reshape(bsz * n_exp, 1, cap), state, wg, wu, wd)


def _final_norm_body(acc_ref, gain_ref, bias_ref, xo_ref, xb_ref):
    out = _layer_norm_rows(acc_ref[...], gain_ref[...], bias_ref[...])
    xo_ref[...] = out
    xb_ref[...] = out.astype(BF16)


def _final_norm(state, gain, bias, seq):
    m_tot = state.shape[0]
    d = gain.shape[1]
    tm = min(512, seq)
    row = pl.BlockSpec((tm, d), lambda m: (m, 0))
    vec = pl.BlockSpec((1, d), lambda m: (0, 0))
    return pl.pallas_call(
        _final_norm_body,
        grid=(m_tot // tm,),
        in_specs=[pl.BlockSpec((tm, d), lambda m: (m, 1)), vec, vec],
        out_specs=[row, row],
        out_shape=[jax.ShapeDtypeStruct((m_tot, d), F32), jax.ShapeDtypeStruct((m_tot, d), BF16)],
        compiler_params=_params("parallel"),
        name="channel_mixer_layernorm",
    )(state, gain, bias)


def _offsets(widths):
    offs = [0]
    for w in widths:
        offs.append(offs[-1] + w)
    return offs


def kernel(x, w_in, b_gate, w_branch, w_out, diff_lambda, diff_subln, sink_logit, na_rpb,
           w_router, w_exp_gate, w_exp_up, w_exp_down, ln_gain, ln_bias):
    bsz, seq, d = x.shape
    depth = w_in.shape[0]
    m_tot = bsz * seq
    alpha = (2.0 * depth) ** 0.25
    cap = EC_CAPACITY_FACTOR * seq // N_EXPERTS
    in_widths = (A_W, A_W, A_W, B_W, B_W, B_W, C_Q_W, C_KV_W, C_KV_W, D_W, D_W, D_W, N_BRANCHES * d)
    io = _offsets(in_widths)
    bo = _offsets((A_OUT_W, B_W, C_Q_W, D_W))
    cos128, sin128 = _rope_tables(seq, HEAD_DIM // 2)
    cos64, sin64 = _rope_tables(seq, B_DIM // 2)

    na_bias = _na_bias_tables(na_rpb.reshape((depth * D_HEADS,) + na_rpb.shape[2:]), seq // GRID_W)
    xf = x.reshape(m_tot, d)
    xb = xf.astype(BF16)
    for l in range(depth):
        lambda_init = 0.8 - 0.6 * math.exp(-0.3 * l)
        def proj(a, b, mode, tn, extra=(), **q_fold):
            return _proj(xb, w_in, l, io[a], io[b] - io[a], mode, tn, seq, extra, **q_fold)

        qk_b = proj(3, 5, "rope64", B_W, (cos64, sin64), q_cols=B_W, q_scale=DIFF_Q_SCALE)
        v_b = proj(5, 6, "plain", B_W)
        qk_c = proj(6, 8, "rope128", C_Q_W + C_KV_W, (cos128, sin128), q_cols=C_Q_W, q_scale=ATTN_Q_SCALE)
        v_c = proj(8, 9, "plain", C_KV_W)
        qkv_d = proj(9, 12, "plain", D_W, q_cols=D_W, q_scale=ATTN_Q_SCALE)
        gates = proj(12, 13, "gate", min(1024, d), (b_gate[l].reshape(1, -1),))

        outs, lses = [], []
        for g, (w, r) in enumerate(DIL_PAIRS):
            qkv_g = _proj_dilated(xb, w_in, l, g, r, bsz, seq, cos128, sin128)
            o_g, lse_g = _banded(qkv_g, qkv_g, qkv_g, 0, 1, 2, hq=A_HEADS_PER_GROUP, group=1,
                                 halo=(w // 2) // r, want_lse=True)
            outs.append(o_g)
            lses.append(lse_g)
        o_a = _group_mix(outs, lses, seq)

        o_b = _diff_attention(qk_b.reshape(bsz, seq, 2 * B_W), v_b.reshape(bsz, seq, B_W),
                              diff_lambda[l], diff_subln[l].reshape(1, -1), lambda_init)

        qk_c4 = qk_c.reshape(bsz, 1, seq, C_Q_W + C_KV_W)
        (o_c,) = _banded(qk_c4, qk_c4, v_c.reshape(bsz, 1, seq, C_KV_W), 0, C_Q_W // C_KV_W, 0,
                         hq=C_Q_HEADS, group=C_Q_HEADS // C_KV_HEADS, halo=C_HALF_WINDOW, sink=sink_logit[l])

        o_d = _neighborhood_attention(qkv_d.reshape(bsz, seq, 3 * D_W), na_bias, l)

        wb = w_branch[l].astype(BF16)
        mixed = _merge([o_a, o_b.reshape(m_tot, B_W), o_c.reshape(m_tot, C_Q_W), o_d.reshape(m_tot, D_W)],
                       [wb[bo[i]:bo[i + 1]] for i in range(N_BRANCHES)], gates, seq)
        xb, state = _out_proj(mixed, w_out[l].astype(BF16), xf, ln_gain[l, 0].reshape(1, -1),
                              ln_bias[l, 0].reshape(1, -1), alpha, seq)

        idx, state = _router(xb.reshape(bsz, seq, d), w_router[l].T.astype(BF16), state, cap)
        state = _expert_ffn(idx, state, l, w_exp_gate, w_exp_up, w_exp_down)
        xf, xb = _final_norm(state, ln_gain[l, 1].reshape(1, -1), ln_bias[l, 1].reshape(1, -1), seq)
    return xf.reshape(bsz, seq, d)
```
